```python
import math
import jax
import jax.numpy as jnp
from jax import lax
import numpy as np

D_MODEL = 1024
BATCH = 4
SEQ = 4096
DEPTH = 4

HEAD_DIM = 64
ROPE_DIM = HEAD_DIM // 4
ROPE_THETA = 500000.0
NORM_EPS = 1e-6
NEG_INF = -1e30
FORCE_SCORE = 1e6

MOBA_HEADS = D_MODEL // (2 * HEAD_DIM)
MOBA_BLOCK = 256
MOBA_TOPK = 3
MOBA_QCHUNK = 32
DIFF_HEADS = D_MODEL // (4 * HEAD_DIM)
DIFF_VDIM = 2 * HEAD_DIM
DIFF_QCHUNK = 128
MOBA_W = MOBA_HEADS * HEAD_DIM
DIFF_QK_W = DIFF_HEADS * 2 * HEAD_DIM
DIFF_V_W = DIFF_HEADS * DIFF_VDIM
EVEN_IN = 3 * MOBA_W + 2 * DIFF_QK_W + DIFF_V_W
EVEN_MIX = MOBA_W + DIFF_V_W

NSA_HEADS = D_MODEL // HEAD_DIM
NSA_GROUP = 4
NSA_KV_HEADS = NSA_HEADS // NSA_GROUP
CMP_BLOCK = 32
CMP_STRIDE = 16
CMP_HIDDEN = 256
SLC_BLOCK = 64
SLC_TOPN = 16
WINDOW = 512
NSA_QCHUNK = 16
NSA_Q_W = NSA_HEADS * HEAD_DIM
NSA_KV_W = NSA_KV_HEADS * HEAD_DIM
ODD_IN = NSA_Q_W + 6 * NSA_KV_W + 3 * NSA_HEADS
ODD_MIX = NSA_Q_W

MOE_GROUPS = 4
MOE_PER_GROUP = 4
MOE_EXPERTS = MOE_GROUPS * MOE_PER_GROUP
MOE_TOPK = 2
MOE_FF = 256

N_EVEN = (DEPTH + 1) // 2
N_ODD = DEPTH // 2

kernel_name = 'hybrid_moba_diff_nsa_hmoe'


def rms_norm(x, g):
    xf = x.astype(jnp.float32)
    y = xf * lax.rsqrt(jnp.mean(xf * xf, axis=-1, keepdims=True) + NORM_EPS)
    return (y * g.astype(jnp.float32)).astype(x.dtype)


def rope_tables(seq):
    pos = jnp.arange(seq, dtype=jnp.float32)
    inv = ROPE_THETA ** (-jnp.arange(0, ROPE_DIM, 2, dtype=jnp.float32) / ROPE_DIM)
    ang = pos[:, None] * inv[None, :]
    return jnp.cos(ang), jnp.sin(ang)


def apply_rope(t, cos, sin):
    half = ROPE_DIM // 2
    c = cos.astype(t.dtype)
    s = sin.astype(t.dtype)
    t1 = t[..., :half]
    t2 = t[..., half:ROPE_DIM]
    return jnp.concatenate([t1 * c - t2 * s, t2 * c + t1 * s, t[..., ROPE_DIM:]], axis=-1)


def split_heads(t, n):
    b, s, _ = t.shape
    return t.reshape(b, s, n, -1).transpose(0, 2, 1, 3)


def merge_heads(t):
    b, h, s, d = t.shape
    return t.transpose(0, 2, 1, 3).reshape(b, s, h * d)


def gather_blocks(blocks, idx):
    return jax.vmap(jax.vmap(lambda bl, ix: bl[ix]))(blocks, idx)


def moba_attention(q, k, v):
    b, h, s, d = q.shape
    nb = -(-s // MOBA_BLOCK)
    pad = nb * MOBA_BLOCK - s
    kp = jnp.pad(k, ((0, 0), (0, 0), (0, pad), (0, 0)))
    vp = jnp.pad(v, ((0, 0), (0, 0), (0, pad), (0, 0)))
    kb = kp.reshape(b, h, nb, MOBA_BLOCK, d)
    vb = vp.reshape(b, h, nb, MOBA_BLOCK, d)
    kmean = jnp.mean(kb.astype(jnp.float32), axis=3)
    ksel = max(1, min(MOBA_TOPK, nb - 1))
    scale = d ** -0.5
    nq = s // MOBA_QCHUNK
    qc = jnp.moveaxis(q.reshape(b, h, nq, MOBA_QCHUNK, d), 2, 0)

    def chunk(args):
        qi, ci = args
        q0 = ci * MOBA_QCHUNK
        qpos = q0 + jnp.arange(MOBA_QCHUNK)
        own = q0 // MOBA_BLOCK
        gs = jnp.einsum('bhqd,bhnd->bhqn', qi.astype(jnp.float32), kmean)
        gs = jnp.where(jnp.arange(nb) < own, gs, NEG_INF)
        _, idx = lax.top_k(gs, ksel)
        valid = idx < own
        k_sel = gather_blocks(kb, idx)
        v_sel = gather_blocks(vb, idx)
        s_sel = jnp.einsum('bhqd,bhqnld->bhqnl', qi, k_sel).astype(jnp.float32) * scale
        s_sel = jnp.where(valid[..., None], s_sel, NEG_INF).reshape(b, h, MOBA_QCHUNK, ksel * MOBA_BLOCK)
        k_own = lax.dynamic_slice_in_dim(kp, own * MOBA_BLOCK, MOBA_BLOCK, axis=2)
        v_own = lax.dynamic_slice_in_dim(vp, own * MOBA_BLOCK, MOBA_BLOCK, axis=2)
        kpos = own * MOBA_BLOCK + jnp.arange(MOBA_BLOCK)
        s_own = jnp.einsum('bhqd,bhld->bhql', qi, k_own).astype(jnp.float32) * scale
        s_own = jnp.where(kpos[None, :] <= qpos[:, None], s_own, NEG_INF)
        p = jax.nn.softmax(jnp.concatenate([s_sel, s_own], axis=-1), axis=-1).astype(v.dtype)
        p_sel = p[..., :ksel * MOBA_BLOCK].reshape(b, h, MOBA_QCHUNK, ksel, MOBA_BLOCK)
        p_own = p[..., ksel * MOBA_BLOCK:]
        return (jnp.einsum('bhqnl,bhqnld->bhqd', p_sel, v_sel)
                + jnp.einsum('bhql,bhld->bhqd', p_own, v_own))

    o = lax.map(chunk, (qc, jnp.arange(nq)))
    return jnp.moveaxis(o, 0, 2).reshape(b, h, s, d)


def diff_attention(q, k, v, lam):
    b, h, _, s, d = q.shape
    nq = s // DIFF_QCHUNK
    scale = d ** -0.5
    kpos = jnp.arange(s)
    qc = jnp.moveaxis(q.reshape(b, h, 2, nq, DIFF_QCHUNK, d), 3, 0)

    def chunk(args):
        qi, ci = args
        qpos = ci * DIFF_QCHUNK + jnp.arange(DIFF_QCHUNK)
        sc = jnp.einsum('bhmqd,bhmkd->bhmqk', qi, k).astype(jnp.float32) * scale
        sc = jnp.where(kpos[None, :] <= qpos[:, None], sc, NEG_INF)
        p = jax.nn.softmax(sc, axis=-1)
        w = p[:, :, 0] - lam * p[:, :, 1]
        return jnp.einsum('bhqk,bhke->bhqe', w.astype(v.dtype), v)

    o = lax.map(chunk, (qc, jnp.arange(nq)))
    return jnp.moveaxis(o, 0, 2).reshape(b, h, s, -1)


def compress_blocks(t, pos_emb, w1, b1, w2, b2):
    b, g, s, d = t.shape
    nc = (s - CMP_BLOCK) // CMP_STRIDE + 1
    idx = jnp.arange(nc)[:, None] * CMP_STRIDE + jnp.arange(CMP_BLOCK)[None, :]
    blocks = t[:, :, idx, :] + pos_emb.astype(t.dtype)
    hid = jax.nn.gelu(jnp.einsum('bgnld,ldf->bgnf', blocks, w1.reshape(CMP_BLOCK, d, -1)) + b1)
    return jnp.einsum('bgnf,fe->bgne', hid, w2) + b2


def nsa_attention(q, kc, vc, ks, vs, kw, vw, gates):
    b, hq, s, d = q.shape
    g = NSA_KV_HEADS
    r = NSA_GROUP
    nc = kc.shape[2]
    ns = s // SLC_BLOCK
    nsel = min(SLC_TOPN, ns)
    scale = d ** -0.5
    cmp_start = jnp.arange(nc) * CMP_STRIDE
    cmp_end = cmp_start + CMP_BLOCK - 1
    slc_start = jnp.arange(ns) * SLC_BLOCK
    overlap = ((cmp_start[:, None] <= slc_start[None, :] + SLC_BLOCK - 1)
               & (cmp_end[:, None] >= slc_start[None, :])).astype(jnp.float32)
    ksb = ks.reshape(b, g, ns, SLC_BLOCK, d)
    vsb = vs.reshape(b, g, ns, SLC_BLOCK, d)
    kwp = jnp.pad(kw, ((0, 0), (0, 0), (WINDOW, 0), (0, 0)))
    vwp = jnp.pad(vw, ((0, 0), (0, 0), (WINDOW, 0), (0, 0)))
    nq = s // NSA_QCHUNK
    qc = jnp.moveaxis(q.reshape(b, hq, nq, NSA_QCHUNK, d), 2, 0)
    gc = jnp.moveaxis(gates.reshape(b, hq, nq, NSA_QCHUNK, 3), 2, 0)
    blk = jnp.arange(ns)

    def chunk(args):
        qi, gi, ci = args
        q0 = ci * NSA_QCHUNK
        qpos = q0 + jnp.arange(NSA_QCHUNK)
        qg = qi.reshape(b, g, r, NSA_QCHUNK, d)
        s_c = jnp.einsum('bgrqd,bgnd->bgrqn', qg, kc).astype(jnp.float32) * scale
        s_c = jnp.where(cmp_end[None, :] <= qpos[:, None], s_c, NEG_INF)
        p_c = jax.nn.softmax(s_c, axis=-1)
        p_c = jnp.where((qpos >= CMP_BLOCK - 1)[:, None], p_c, 0.0)
        o_c = jnp.einsum('bgrqn,bgnd->bgrqd', p_c.astype(vc.dtype), vc)
        imp = jnp.einsum('bgrqn,nj->bgqj', p_c, overlap)
        own = qpos // SLC_BLOCK
        started = slc_start[None, :] <= qpos[:, None]
        forced = (blk[None, :] == 0) | (blk[None, :] == own[:, None]) | (blk[None, :] == own[:, None] - 1)
        imp = jnp.where(started, jnp.where(forced, FORCE_SCORE, imp), NEG_INF)
        _, sidx = lax.top_k(imp, nsel)
        k_sel = gather_blocks(ksb, sidx)
        v_sel = gather_blocks(vsb, sidx)
        s_s = jnp.einsum('bgrqd,bgqnld->bgrqnl', qg, k_sel).astype(jnp.float32) * scale
        tok = sidx[..., None] * SLC_BLOCK + jnp.arange(SLC_BLOCK)
        s_s = jnp.where((tok <= qpos[:, None, None])[:, :, None], s_s, NEG_INF)
        p_s = jax.nn.softmax(s_s.reshape(b, g, r, NSA_QCHUNK, -1), axis=-1).reshape(s_s.shape)
        o_s = jnp.einsum('bgrqnl,bgqnld->bgrqd', p_s.astype(v_sel.dtype), v_sel)
        k_w = lax.dynamic_slice_in_dim(kwp, q0, WINDOW + NSA_QCHUNK, axis=2)
        v_w = lax.dynamic_slice_in_dim(vwp, q0, WINDOW + NSA_QCHUNK, axis=2)
        kpos = q0 - WINDOW + jnp.arange(WINDOW + NSA_QCHUNK)
        band = ((kpos[None, :] <= qpos[:, None]) & (kpos[None, :] > qpos[:, None] - WINDOW)
                & (kpos[None, :] >= 0))
        s_w = jnp.einsum('bgrqd,bgkd->bgrqk', qg, k_w).astype(jnp.float32) * scale
        p_w = jax.nn.softmax(jnp.where(band, s_w, NEG_INF), axis=-1)
        o_w = jnp.einsum('bgrqk,bgkd->bgrqd', p_w.astype(v_w.dtype), v_w)
        gt = gi.reshape(b, g, r, NSA_QCHUNK, 3).astype(o_c.dtype)
        o = gt[..., 0:1] * o_c + gt[..., 1:2] * o_s + gt[..., 2:3] * o_w
        return o.reshape(b, hq, NSA_QCHUNK, d)

    o = lax.map(chunk, (qc, gc, jnp.arange(nq)))
    return jnp.moveaxis(o, 0, 2).reshape(b, hq, s, d)


def even_mixer(h, w_in, w_out, lam_p, subln_g, lambda_init, cos, sin):
    b, s, _ = h.shape
    p = h @ w_in
    c1 = MOBA_W
    c2 = 2 * MOBA_W
    c3 = 3 * MOBA_W
    c4 = c3 + DIFF_QK_W
    c5 = c4 + DIFF_QK_W
    qa, ka, va, qb, kb, vb = jnp.split(p, [c1, c2, c3, c4, c5], axis=-1)
    qa = apply_rope(split_heads(qa, MOBA_HEADS), cos, sin)
    ka = apply_rope(split_heads(ka, MOBA_HEADS), cos, sin)
    oa = moba_attention(qa, ka, split_heads(va, MOBA_HEADS))
    qb = apply_rope(qb.reshape(b, s, DIFF_HEADS, 2, HEAD_DIM).transpose(0, 2, 3, 1, 4), cos, sin)
    kb = apply_rope(kb.reshape(b, s, DIFF_HEADS, 2, HEAD_DIM).transpose(0, 2, 3, 1, 4), cos, sin)
    lp = lam_p.astype(jnp.float32)
    lam = jnp.exp(jnp.sum(lp[0] * lp[1])) - jnp.exp(jnp.sum(lp[2] * lp[3])) + lambda_init
    ob = diff_attention(qb, kb, split_heads(vb, DIFF_HEADS), lam)
    ob = rms_norm(ob, subln_g) * (1.0 - lambda_init)
    o = jnp.concatenate([merge_heads(oa), merge_heads(ob)], axis=-1)
    return o @ w_out


def odd_mixer(h, w_in, w_out, cmp_pos, cmp_w1, cmp_b1, cmp_w2, cmp_b2, cos, sin):
    b, s, _ = h.shape
    p = h @ w_in
    cuts = [NSA_Q_W + i * NSA_KV_W for i in range(7)]
    q, kc, vc, ks, vs, kw, vw, gl = jnp.split(p, cuts, axis=-1)
    q = apply_rope(split_heads(q, NSA_HEADS), cos, sin)
    kc = apply_rope(split_heads(kc, NSA_KV_HEADS), cos, sin)
    ks = apply_rope(split_heads(ks, NSA_KV_HEADS), cos, sin)
    kw = apply_rope(split_heads(kw, NSA_KV_HEADS), cos, sin)
    vc = split_heads(vc, NSA_KV_HEADS)
    vs = split_heads(vs, NSA_KV_HEADS)
    vw = split_heads(vw, NSA_KV_HEADS)
    gates = jax.nn.sigmoid(gl.astype(jnp.float32)).reshape(b, s, NSA_HEADS, 3).transpose(0, 2, 1, 3)
    k_cmp = compress_blocks(kc, cmp_pos[0], cmp_w1[0], cmp_b1[0], cmp_w2[0], cmp_b2[0])
    v_cmp = compress_blocks(vc, cmp_pos[1], cmp_w1[1], cmp_b1[1], cmp_w2[1], cmp_b2[1])
    o = nsa_attention(q, k_cmp, v_cmp, ks, vs, kw, vw, gates)
    return merge_heads(o) @ w_out


def hier_moe(h, wg, bg, we, be, w_gate, w_up, w_down):
    b, s, d = h.shape
    t = h.reshape(-1, d)
    n = t.shape[0]
    gl = (t @ wg + bg).astype(jnp.float32)
    gp = jax.nn.softmax(gl, axis=-1)
    _, gsel = lax.top_k(gl, 1)
    g_oh = jax.nn.one_hot(gsel[:, 0], MOE_GROUPS, dtype=jnp.float32)
    gw = jnp.sum(gp * g_oh, axis=-1, keepdims=True)
    el = (t @ we + be).astype(jnp.float32).reshape(n, MOE_GROUPS, MOE_PER_GROUP)
    el_g = jnp.sum(el * g_oh[:, :, None], axis=1)
    tv, ti = lax.top_k(el_g, MOE_TOPK)
    tw = jax.nn.softmax(tv, axis=-1)
    w_grp = jnp.sum(jax.nn.one_hot(ti, MOE_PER_GROUP, dtype=jnp.float32) * tw[..., None], axis=1)
    comb = ((g_oh * gw)[:, :, None] * w_grp[:, None, :]).reshape(n, MOE_EXPERTS)
    a = jax.nn.silu(jnp.einsum('td,edf->tef', t, w_gate)) * jnp.einsum('td,edf->tef', t, w_up)
    y = jnp.einsum('tef,efd->td', a * comb[:, :, None].astype(a.dtype), w_down)
    return y.reshape(b, s, d)


def setup_inputs(seed: int = 0) -> dict:
    key = jax.random.key(seed)
    ks = jax.random.split(key, 28)
    D = D_MODEL

    def nrm(k, shape, std):
        return jax.random.normal(k, shape, jnp.float32) * std

    return {
        'x': nrm(ks[0], (BATCH, SEQ, D), 1.0),
        'c': nrm(ks[1], (BATCH, D), 1.0),
        'norm1_g': 1.0 + nrm(ks[2], (DEPTH, D), 0.02),
        'norm2_g': 1.0 + nrm(ks[3], (DEPTH, D), 0.02),
        'final_g': 1.0 + nrm(ks[4], (D,), 0.02),
        'ada_w': nrm(ks[5], (DEPTH, D, 6 * D), 0.5 * D ** -0.5),
        'ada_b': nrm(ks[6], (DEPTH, 6 * D), 0.02),
        'ev_w_in': nrm(ks[7], (N_EVEN, D, EVEN_IN), D ** -0.5),
        'ev_w_out': nrm(ks[8], (N_EVEN, EVEN_MIX, D), EVEN_MIX ** -0.5),
        'ev_lambda': nrm(ks[9], (N_EVEN, 4, HEAD_DIM), 0.1),
        'ev_subln_g': 1.0 + nrm(ks[10], (N_EVEN, DIFF_VDIM), 0.02),
        'od_w_in': nrm(ks[11], (N_ODD, D, ODD_IN), D ** -0.5),
        'od_w_out': nrm(ks[12], (N_ODD, ODD_MIX, D), ODD_MIX ** -0.5),
        'od_cmp_pos': nrm(ks[13], (N_ODD, 2, CMP_BLOCK, HEAD_DIM), 0.1),
        'od_cmp_w1': nrm(ks[14], (N_ODD, 2, CMP_BLOCK * HEAD_DIM, CMP_HIDDEN), (CMP_BLOCK * HEAD_DIM) ** -0.5),
        'od_cmp_b1': nrm(ks[15], (N_ODD, 2, CMP_HIDDEN), 0.01),
        'od_cmp_w2': nrm(ks[16], (N_ODD, 2, CMP_HIDDEN, HEAD_DIM), CMP_HIDDEN ** -0.5),
        'od_cmp_b2': nrm(ks[17], (N_ODD, 2, HEAD_DIM), 0.01),
        'moe_wg': nrm(ks[18], (DEPTH, D, MOE_GROUPS), D ** -0.5),
        'moe_bg': nrm(ks[19], (DEPTH, MOE_GROUPS), 0.01),
        'moe_we': nrm(ks[20], (DEPTH, D, MOE_EXPERTS), D ** -0.5),
        'moe_be': nrm(ks[21], (DEPTH, MOE_EXPERTS), 0.01),
        'moe_w_gate': nrm(ks[22], (DEPTH, MOE_EXPERTS, D, MOE_FF), D ** -0.5),
        'moe_w_up': nrm(ks[23], (DEPTH, MOE_EXPERTS, D, MOE_FF), D ** -0.5),
        'moe_w_down': nrm(ks[24], (DEPTH, MOE_EXPERTS, MOE_FF, D), MOE_FF ** -0.5),
    }


def reference(x, c, norm1_g, norm2_g, final_g, ada_w, ada_b, ev_w_in, ev_w_out, ev_lambda,
              ev_subln_g, od_w_in, od_w_out, od_cmp_pos, od_cmp_w1, od_cmp_b1, od_cmp_w2,
              od_cmp_b2, moe_wg, moe_bg, moe_we, moe_be, moe_w_gate, moe_w_up, moe_w_down):
    s = x.shape[1]
    cos, sin = rope_tables(s)
    cs = jax.nn.silu(c)
    for l in range(DEPTH):
        mod = (cs @ ada_w[l] + ada_b[l])[:, None, :]
        sh1, sc1, g1, sh2, sc2, g2 = jnp.split(mod, 6, axis=-1)
        h = rms_norm(x, norm1_g[l]) * (1.0 + sc1) + sh1
        if l % 2 == 0:
            i = l // 2
            lambda_init = 0.8 - 0.6 * math.exp(-0.3 * l)
            y = even_mixer(h, ev_w_in[i], ev_w_out[i], ev_lambda[i], ev_subln_g[i], lambda_init, cos, sin)
        else:
            i = l // 2
            y = odd_mixer(h, od_w_in[i], od_w_out[i], od_cmp_pos[i], od_cmp_w1[i], od_cmp_b1[i],
                          od_cmp_w2[i], od_cmp_b2[i], cos, sin)
        x = x + g1 * y
        h = rms_norm(x, norm2_g[l]) * (1.0 + sc2) + sh2
        x = x + g2 * hier_moe(h, moe_wg[l], moe_bg[l], moe_we[l], moe_be[l],
                              moe_w_gate[l], moe_w_up[l], moe_w_down[l])
    return rms_norm(x, final_g)
```

```python
import functools
import math

import jax
import jax.numpy as jnp
from jax import lax
from jax.experimental import pallas as pl
from jax.experimental.pallas import tpu as pltpu

F32 = jnp.float32
BF16 = jnp.bfloat16

D_MODEL = 1024
DEPTH = 4
HEAD_DIM = 64
ROPE_DIM = HEAD_DIM // 4
ROPE_HALF = ROPE_DIM // 2
ROPE_THETA = 500000.0
NORM_EPS = 1e-6
NEG_INF = -1e30
FORCE_SCORE = 1e6
QK_SCALE = HEAD_DIM ** -0.5

MOBA_HEADS = 8
MOBA_BLOCK = 256
MOBA_TOPK = 3
DIFF_HEADS = 4
MOBA_W = MOBA_HEADS * HEAD_DIM
DIFF_QK_W = DIFF_HEADS * 2 * HEAD_DIM
DIFF_V_W = DIFF_HEADS * 2 * HEAD_DIM
EVEN_IN = 3 * MOBA_W + 2 * DIFF_QK_W + DIFF_V_W

NSA_HEADS = 16
NSA_GROUP = 4
NSA_KV_HEADS = 4
CMP_BLOCK = 32
CMP_STRIDE = 16
CMP_HIDDEN = 256
SLC_BLOCK = 64
SLC_TOPN = 16
WINDOW = 512
NSA_Q_W = NSA_HEADS * HEAD_DIM
NSA_KV_W = NSA_KV_HEADS * HEAD_DIM
ODD_IN = NSA_Q_W + 6 * NSA_KV_W + 3 * NSA_HEADS
N_GATES = 3 * NSA_HEADS

MOE_GROUPS = 4
MOE_PER_GROUP = 4
MOE_EXPERTS = 16
MOE_FF = 256

LANES = 128
VMEM_LIMIT = 56 * 1024 * 1024

PROJ_TM = 512
ATT_TQ = 256
NSA_TQ = 128
NSA_TK = 256


def _params(*sem):
    return pltpu.CompilerParams(dimension_semantics=sem, vmem_limit_bytes=VMEM_LIMIT)


def _dot(a, b):
    return jnp.dot(a, b, preferred_element_type=F32)


def _dot_nt(a, b):
    return lax.dot_general(a, b, (((1,), (1,)), ((), ())), preferred_element_type=F32)


def _split_bf16(x):
    hi = x.astype(BF16)
    lo = (x - hi.astype(F32)).astype(BF16)
    return hi, lo


def _norm_mod(x, g, sc, sh):
    y = x * lax.rsqrt(jnp.mean(x * x, axis=-1, keepdims=True) + NORM_EPS)
    return (y * g) * (1.0 + sc) + sh


def _rope(t, c, s1, s2):
    w = t.shape[1]
    k = w // LANES
    cw = jnp.concatenate([c] * k, axis=1) if k > 1 else c
    s1w = jnp.concatenate([s1] * k, axis=1) if k > 1 else s1
    s2w = jnp.concatenate([s2] * k, axis=1) if k > 1 else s2
    return t * cw + pltpu.roll(t, ROPE_HALF, 1) * s1w + pltpu.roll(t, w - ROPE_HALF, 1) * s2w


def _rope_tables(seq):
    pos = jnp.arange(seq, dtype=F32)
    inv = ROPE_THETA ** (-jnp.arange(0, ROPE_DIM, 2, dtype=F32) / ROPE_DIM)
    ang = pos[:, None] * inv[None, :]
    cos, sin = jnp.cos(ang), jnp.sin(ang)
    ones = jnp.ones((seq, HEAD_DIM - ROPE_DIM), F32)
    zeros8 = jnp.zeros((seq, ROPE_HALF), F32)
    zeros = jnp.zeros((seq, HEAD_DIM - ROPE_DIM), F32)
    c = jnp.concatenate([cos, cos, ones], axis=1)
    s1 = jnp.concatenate([zeros8, sin, zeros], axis=1)
    s2 = jnp.concatenate([-sin, zeros8, zeros], axis=1)
    rep = LANES // HEAD_DIM
    return tuple(jnp.tile(t, (1, rep)) for t in (c, s1, s2))


def _ada_body(c_ref, w_ref, b_ref, o_ref):
    c = c_ref[...]
    cs = c * (1.0 / (1.0 + jnp.exp(-c)))
    o_ref[0] = jnp.dot(cs, w_ref[0], preferred_element_type=F32,
                       precision=lax.Precision.HIGHEST) + b_ref[0]


def _ada_mod(c, ada_w, ada_b):
    b, d = c.shape
    depth, _, n = ada_w.shape
    rows = 8
    tn = 1536
    cp = jnp.pad(c, ((0, rows - b), (0, 0)))
    out = pl.pallas_call(
        _ada_body,
        grid=(depth, n // tn),
        in_specs=[pl.BlockSpec((rows, d), lambda l, j: (0, 0)),
                  pl.BlockSpec((1, d, tn), lambda l, j: (l, 0, j)),
                  pl.BlockSpec((1, 1, tn), lambda l, j: (l, 0, j))],
        out_specs=pl.BlockSpec((1, rows, tn), lambda l, j: (l, 0, j)),
        out_shape=jax.ShapeDtypeStruct((depth, rows, n), F32),
        compiler_params=_params("parallel", "parallel"),
        name="ada_mod",
    )(cp, ada_w, ada_b.reshape(depth, 1, n))
    return out[:, :b]


def _even_proj_body(x_ref, g_ref, sc_ref, sh_ref, w_ref, c_ref, s1_ref, s2_ref, p_ref, km_ref):
    h = _norm_mod(x_ref[0], g_ref[...], sc_ref[0], sh_ref[0]).astype(BF16)
    c, s1, s2 = c_ref[...], s1_ref[...], s2_ref[...]
    ch = MOBA_W
    for idx, kind in enumerate(("q", "k", "v", "q", "k", "v")):
        acc = _dot(h, w_ref[:, idx * ch:(idx + 1) * ch])
        if kind != "v":
            acc = _rope(acc, c, s1, s2)
        if kind == "q":
            acc = acc * QK_SCALE
        if idx == 1:
            nblk = acc.shape[0] // MOBA_BLOCK
            km_ref[0, 0] = jnp.concatenate(
                [jnp.mean(acc[n * MOBA_BLOCK:(n + 1) * MOBA_BLOCK], axis=0, keepdims=True)
                 for n in range(nblk)], axis=0)
        p_ref[0, :, idx * ch:(idx + 1) * ch] = acc.astype(BF16)


def _even_proj(x, g, sc, sh, w, tables):
    b, s, d = x.shape
    n = w.shape[1]
    tm = PROJ_TM
    nblk = tm // MOBA_BLOCK
    row = lambda bi, i: (bi, i, 0)
    vec = lambda bi, i: (bi, 0, 0)
    tab = pl.BlockSpec((tm, LANES), lambda bi, i: (i, 0))
    return pl.pallas_call(
        _even_proj_body,
        grid=(b, s // tm),
        in_specs=[pl.BlockSpec((1, tm, d), row),
                  pl.BlockSpec((1, d), lambda bi, i: (0, 0)),
                  pl.BlockSpec((1, 1, d), vec),
                  pl.BlockSpec((1, 1, d), vec),
                  pl.BlockSpec((d, n), lambda bi, i: (0, 0)),
                  tab, tab, tab],
        out_specs=[pl.BlockSpec((1, tm, n), row),
                   pl.BlockSpec((1, 1, nblk, MOBA_W), lambda bi, i: (bi, i, 0, 0))],
        out_shape=[jax.ShapeDtypeStruct((b, s, n), BF16),
                   jax.ShapeDtypeStruct((b, s // tm, nblk, MOBA_W), F32)],
        compiler_params=_params("parallel", "parallel"),
        name="even_proj",
    )(x, g, sc, sh, w, *tables)


def _flash_step(q, k, v, bias, m, l, acc):
    s = _dot_nt(q, k)
    if bias is not None:
        s = s + bias
    m_new = jnp.maximum(m, jnp.max(s, axis=1, keepdims=True))
    alpha = jnp.exp(m - m_new)
    p = jnp.exp(s - m_new)
    l = alpha * l + jnp.sum(p, axis=1, keepdims=True)
    acc = alpha * acc + _dot(p.astype(v.dtype), v)
    return m_new, l, acc


def _flash_first(q, k, v, mask):
    s = jnp.where(mask, _dot_nt(q, k), NEG_INF)
    m = jnp.max(s, axis=1, keepdims=True)
    p = jnp.exp(s - m)
    return m, jnp.sum(p, axis=1, keepdims=True), _dot(p.astype(v.dtype), v)


def _moba_body(q_ref, k_ref, v_ref, km_ref, o_ref):
    i = pl.program_id(2)
    tq = q_ref.shape[1]
    nb = km_ref.shape[1]
    heads = LANES // HEAD_DIM
    n_iota = lax.broadcasted_iota(jnp.int32, (tq, nb), 1)
    row = lax.broadcasted_iota(jnp.int32, (tq, tq), 0)
    col = lax.broadcasted_iota(jnp.int32, (tq, tq), 1)
    causal = col <= row
    d0 = pl.multiple_of(i * tq, tq)

    qs, biases, state = [], [], []
    for h in range(heads):
        lo = h * HEAD_DIM
        q = q_ref[0][:, lo:lo + HEAD_DIM]
        km_hi, km_lo = _split_bf16(km_ref[0][:, lo:lo + HEAD_DIM])
        gs = _dot_nt(q, km_hi) + _dot_nt(q, km_lo)
        gs = jnp.where(n_iota < i, gs, NEG_INF)
        cnt = jnp.zeros((tq, nb), jnp.int32)
        for m in range(nb):
            cm = gs[:, m:m + 1]
            cnt = cnt + ((cm > gs) | ((cm == gs) & (n_iota > m))).astype(jnp.int32)
        sel = (cnt < MOBA_TOPK) & (n_iota < i)
        biases.append(jnp.where(sel, 0.0, NEG_INF))
        qs.append(q)
        kd = k_ref[0, pl.ds(d0, tq), :][:, lo:lo + HEAD_DIM]
        vd = v_ref[0, pl.ds(d0, tq), :][:, lo:lo + HEAD_DIM]
        state.extend(_flash_first(q, kd, vd, causal))

    def body(j, carry):
        j0 = pl.multiple_of(j * MOBA_BLOCK, MOBA_BLOCK)
        kj = k_ref[0, pl.ds(j0, MOBA_BLOCK), :]
        vj = v_ref[0, pl.ds(j0, MOBA_BLOCK), :]
        out = []
        for h in range(heads):
            lo = h * HEAD_DIM
            bias = jnp.sum(jnp.where(n_iota == j, biases[h], 0.0), axis=1, keepdims=True)
            out.extend(_flash_step(qs[h], kj[:, lo:lo + HEAD_DIM], vj[:, lo:lo + HEAD_DIM], bias,
                                   *carry[3 * h:3 * h + 3]))
        return tuple(out)

    state = lax.fori_loop(0, i, body, tuple(state))
    o = [state[3 * h + 2] / state[3 * h + 1] for h in range(heads)]
    o_ref[0] = jnp.concatenate(o, axis=1).astype(o_ref.dtype)


def _moba_attention(p, kmean):
    b, s, _ = p.shape
    nb = s // MOBA_BLOCK
    pairs = MOBA_W // LANES
    tq = ATT_TQ
    assert tq == MOBA_BLOCK
    return pl.pallas_call(
        _moba_body,
        grid=(b, pairs, s // tq),
        in_specs=[pl.BlockSpec((1, tq, LANES), lambda bi, hp, i: (bi, i, hp)),
                  pl.BlockSpec((1, s, LANES), lambda bi, hp, i: (bi, 0, pairs + hp)),
                  pl.BlockSpec((1, s, LANES), lambda bi, hp, i: (bi, 0, 2 * pairs + hp)),
                  pl.BlockSpec((1, nb, LANES), lambda bi, hp, i: (bi, 0, hp))],
        out_specs=pl.BlockSpec((1, tq, LANES), lambda bi, hp, i: (bi, i, hp)),
        out_shape=jax.ShapeDtypeStruct((b, s, MOBA_W), BF16),
        compiler_params=_params("parallel", "parallel", "arbitrary"),
        name="moba_attn",
    )(p, p, p, kmean)


def _diff_body(lam_ref, g_ref, q_ref, k_ref, v_ref, o_ref, *, lambda_init):
    i = pl.program_id(2)
    tq = q_ref.shape[1]
    lp = lam_ref[...]
    lam = (jnp.exp(jnp.sum(lp[0:1] * lp[1:2], axis=1, keepdims=True))
           - jnp.exp(jnp.sum(lp[2:3] * lp[3:4], axis=1, keepdims=True)) + lambda_init)
    row = lax.broadcasted_iota(jnp.int32, (tq, tq), 0)
    col = lax.broadcasted_iota(jnp.int32, (tq, tq), 1)
    causal = col <= row
    d0 = pl.multiple_of(i * tq, tq)
    q = q_ref[0]
    qs = [q[:, m * HEAD_DIM:(m + 1) * HEAD_DIM] for m in range(2)]
    kd = k_ref[0, pl.ds(d0, tq), :]
    vd = v_ref[0, pl.ds(d0, tq), :]
    state = []
    for m in range(2):
        state.extend(_flash_first(qs[m], kd[:, m * HEAD_DIM:(m + 1) * HEAD_DIM], vd, causal))

    def body(j, carry):
        j0 = pl.multiple_of(j * tq, tq)
        kj = k_ref[0, pl.ds(j0, tq), :]
        vj = v_ref[0, pl.ds(j0, tq), :]
        out = []
        for m in range(2):
            out.extend(_flash_step(qs[m], kj[:, m * HEAD_DIM:(m + 1) * HEAD_DIM], vj, None,
                                   *carry[3 * m:3 * m + 3]))
        return tuple(out)

    state = lax.fori_loop(0, i, body, tuple(state))
    o = state[2] / state[1] - lam * (state[5] / state[4])
    y = o * lax.rsqrt(jnp.mean(o * o, axis=-1, keepdims=True) + NORM_EPS)
    o_ref[0] = ((y * g_ref[...]) * (1.0 - lambda_init)).astype(o_ref.dtype)


def _diff_attention(p, lam_p, subln_g, lambda_init):
    b, s, _ = p.shape
    tq = ATT_TQ
    qoff = 3 * MOBA_W // LANES
    koff = qoff + DIFF_QK_W // LANES
    voff = koff + DIFF_QK_W // LANES
    return pl.pallas_call(
        functools.partial(_diff_body, lambda_init=lambda_init),
        grid=(b, DIFF_HEADS, s // tq),
        in_specs=[pl.BlockSpec((4, HEAD_DIM), lambda bi, h, i: (0, 0)),
                  pl.BlockSpec((1, LANES), lambda bi, h, i: (0, 0)),
                  pl.BlockSpec((1, tq, LANES), lambda bi, h, i: (bi, i, qoff + h)),
                  pl.BlockSpec((1, s, LANES), lambda bi, h, i: (bi, 0, koff + h)),
                  pl.BlockSpec((1, s, LANES), lambda bi, h, i: (bi, 0, voff + h))],
        out_specs=pl.BlockSpec((1, tq, LANES), lambda bi, h, i: (bi, i, h)),
        out_shape=jax.ShapeDtypeStruct((b, s, DIFF_V_W), BF16),
        compiler_params=_params("parallel", "parallel", "arbitrary"),
        name="diff_attn",
    )(lam_p, subln_g.reshape(1, LANES), p, p, p)


def _out_proj_body(*refs, widths):
    x_ref, g_ref = refs[0], refs[1]
    o_refs = refs[2:2 + len(widths)]
    w_ref, out_ref = refs[2 + len(widths)], refs[3 + len(widths)]
    y = None
    r0 = 0
    for o_ref, wd in zip(o_refs, widths):
        t = _dot(o_ref[0], w_ref[r0:r0 + wd, :])
        y = t if y is None else y + t
        r0 += wd
    out_ref[0] = x_ref[0] + g_ref[0] * y


def _out_proj(x, gate, mixes, w):
    b, s, d = x.shape
    tm = PROJ_TM
    widths = tuple(m.shape[2] for m in mixes)
    row = lambda bi, i: (bi, i, 0)
    return pl.pallas_call(
        functools.partial(_out_proj_body, widths=widths),
        grid=(b, s // tm),
        in_specs=[pl.BlockSpec((1, tm, d), row),
                  pl.BlockSpec((1, 1, d), lambda bi, i: (bi, 0, 0))]
                 + [pl.BlockSpec((1, tm, wd), row) for wd in widths]
                 + [pl.BlockSpec(w.shape, lambda bi, i: (0, 0))],
        out_specs=pl.BlockSpec((1, tm, d), row),
        out_shape=jax.ShapeDtypeStruct((b, s, d), F32),
        compiler_params=_params("parallel", "parallel"),
        name="out_proj",
    )(x, gate, *mixes, w)


def _first_argmax_onehot(v, iota):
    mx = jnp.max(v, axis=1, keepdims=True)
    idx = jnp.min(jnp.where(v == mx, iota, float(v.shape[1])), axis=1, keepdims=True)
    return iota == idx, mx


def _moe_body(x_ref, g_ref, sc_ref, sh_ref, gate_ref, wr_hi_ref, wr_lo_ref, br_ref,
              wg_ref, wu_ref, wd_ref, o_ref, a_ref):
    x = x_ref[0]
    h = _norm_mod(x, g_ref[...], sc_ref[0], sh_ref[0])
    h_hi, h_lo = _split_bf16(h)
    wr_hi = wr_hi_ref[...]
    r = (_dot(h_hi, wr_hi) + (_dot(h_lo, wr_hi) + _dot(h_hi, wr_lo_ref[...]))) + br_ref[...]
    tm = x.shape[0]
    gl = r[:, 0:MOE_GROUPS]
    iota = lax.broadcasted_iota(jnp.int32, (tm, MOE_GROUPS), 1).astype(F32)
    g_oh, g_mx = _first_argmax_onehot(gl, iota)
    gw = 1.0 / jnp.sum(jnp.exp(gl - g_mx), axis=1, keepdims=True)
    el_g = jnp.zeros((tm, MOE_PER_GROUP), F32)
    for g in range(MOE_GROUPS):
        lo = MOE_GROUPS + g * MOE_PER_GROUP
        el_g = el_g + jnp.where(g_oh[:, g:g + 1], r[:, lo:lo + MOE_PER_GROUP], 0.0)
    oh1, v1 = _first_argmax_onehot(el_g, iota)
    oh2, v2 = _first_argmax_onehot(jnp.where(oh1, -jnp.inf, el_g), iota)
    e2 = jnp.exp(v2 - v1)
    den = 1.0 + e2
    w_grp = jnp.where(oh1, 1.0 / den, 0.0) + jnp.where(oh2, e2 / den, 0.0)
    gsc = jnp.where(g_oh, gw, 0.0)

    hb = h_hi
    for e in range(MOE_EXPERTS):
        g, k = divmod(e, MOE_PER_GROUP)
        comb = gsc[:, g:g + 1] * w_grp[:, k:k + 1]
        gt = _dot(hb, wg_ref[e])
        up = _dot(hb, wu_ref[e])
        a = ((gt * (1.0 / (1.0 + jnp.exp(-gt)))) * up) * comb
        a_ref[:, e * MOE_FF:(e + 1) * MOE_FF] = a.astype(BF16)
    y = _dot(a_ref[...], wd_ref[...])
    o_ref[0] = x + gate_ref[0] * y


def _moe(x, g, sc, sh, gate, wg, bg, we, be, w_gate, w_up, w_down):
    b, s, d = x.shape
    tm = PROJ_TM
    nr = MOE_GROUPS + MOE_EXPERTS
    wr = jnp.pad(jnp.concatenate([wg, we], axis=1), ((0, 0), (0, LANES - nr)))
    br = jnp.pad(jnp.concatenate([bg, be], axis=0), (0, LANES - nr)).reshape(1, LANES)
    wr_hi, wr_lo = _split_bf16(wr)
    row = lambda bi, i: (bi, i, 0)
    vec = lambda bi, i: (bi, 0, 0)
    const2 = lambda bi, i: (0, 0)
    const3 = lambda bi, i: (0, 0, 0)
    once = pl.Buffered(1)
    return pl.pallas_call(
        _moe_body,
        grid=(b, s // tm),
        in_specs=[pl.BlockSpec((1, tm, d), row),
                  pl.BlockSpec((1, d), const2),
                  pl.BlockSpec((1, 1, d), vec),
                  pl.BlockSpec((1, 1, d), vec),
                  pl.BlockSpec((1, 1, d), vec),
                  pl.BlockSpec((d, LANES), const2),
                  pl.BlockSpec((d, LANES), const2),
                  pl.BlockSpec((1, LANES), const2),
                  pl.BlockSpec((MOE_EXPERTS, d, MOE_FF), const3, pipeline_mode=once),
                  pl.BlockSpec((MOE_EXPERTS, d, MOE_FF), const3, pipeline_mode=once),
                  pl.BlockSpec((MOE_EXPERTS * MOE_FF, d), const2, pipeline_mode=once)],
        out_specs=pl.BlockSpec((1, tm, d), row),
        out_shape=jax.ShapeDtypeStruct((b, s, d), F32),
        scratch_shapes=[pltpu.VMEM((tm, MOE_EXPERTS * MOE_FF), BF16)],
        compiler_params=_params("parallel", "parallel"),
        name="hier_moe",
    )(x, g, sc, sh, gate, wr_hi, wr_lo, br, w_gate.astype(BF16), w_up.astype(BF16),
      w_down.astype(BF16).reshape(MOE_EXPERTS * MOE_FF, d))


ODD_PAD = 2688


def _odd_proj_body(x_ref, g_ref, sc_ref, sh_ref, w_ref, c_ref, s1_ref, s2_ref,
                   q_ref, kvf_ref, kvb_ref, gates_ref):
    h = _norm_mod(x_ref[0], g_ref[...], sc_ref[0], sh_ref[0]).astype(BF16)
    c, s1, s2 = c_ref[...], s1_ref[...], s2_ref[...]
    half = NSA_Q_W // 2
    for idx in range(2):
        acc = _dot(h, w_ref[:, idx * half:(idx + 1) * half])
        q_ref[0, :, idx * half:(idx + 1) * half] = (_rope(acc, c, s1, s2) * QK_SCALE).astype(BF16)
    for idx in range(6):
        c0 = NSA_Q_W + idx * NSA_KV_W
        acc = _dot(h, w_ref[:, c0:c0 + NSA_KV_W])
        if idx % 2 == 0:
            acc = _rope(acc, c, s1, s2)
        for g in range(NSA_KV_HEADS):
            piece = acc[:, g * HEAD_DIM:(g + 1) * HEAD_DIM]
            if idx < 2:
                kvf_ref[0, idx * NSA_KV_HEADS + g] = piece
            else:
                kvb_ref[0, (idx - 2) * NSA_KV_HEADS + g] = piece.astype(BF16)
    c0 = NSA_Q_W + 6 * NSA_KV_W
    gl = _dot(h, w_ref[:, c0:c0 + LANES])
    gates_ref[0] = 1.0 / (1.0 + jnp.exp(-gl))


def _odd_proj(x, g, sc, sh, w, tables):
    b, s, d = x.shape
    n = w.shape[1]
    tm = PROJ_TM
    row = lambda bi, i: (bi, i, 0)
    vec = lambda bi, i: (bi, 0, 0)
    tab = pl.BlockSpec((tm, LANES), lambda bi, i: (i, 0))
    hd = lambda bi, i: (bi, 0, i, 0)
    return pl.pallas_call(
        _odd_proj_body,
        grid=(b, s // tm),
        in_specs=[pl.BlockSpec((1, tm, d), row),
                  pl.BlockSpec((1, d), lambda bi, i: (0, 0)),
                  pl.BlockSpec((1, 1, d), vec),
                  pl.BlockSpec((1, 1, d), vec),
                  pl.BlockSpec((d, n), lambda bi, i: (0, 0)),
                  tab, tab, tab],
        out_specs=[pl.BlockSpec((1, tm, NSA_Q_W), row),
                   pl.BlockSpec((1, 2 * NSA_KV_HEADS, tm, HEAD_DIM), hd),
                   pl.BlockSpec((1, 4 * NSA_KV_HEADS, tm, HEAD_DIM), hd),
                   pl.BlockSpec((1, tm, LANES), row)],
        out_shape=[jax.ShapeDtypeStruct((b, s, NSA_Q_W), BF16),
                   jax.ShapeDtypeStruct((b, 2 * NSA_KV_HEADS, s, HEAD_DIM), F32),
                   jax.ShapeDtypeStruct((b, 4 * NSA_KV_HEADS, s, HEAD_DIM), BF16),
                   jax.ShapeDtypeStruct((b, s, LANES), F32)],
        compiler_params=_params("parallel", "parallel"),
        name="odd_proj",
    )(x, g, sc, sh, w, *tables)


def _compress_body(x_ref, pos_ref, w1_ref, b1_ref, w2_ref, b2_ref, o_ref):
    x = x_ref[0, 0]
    half = CMP_STRIDE * HEAD_DIM
    xa = (x + pos_ref[0, 0:1]).astype(BF16)
    xb = (x + pos_ref[0, 1:2]).astype(BF16)
    a = _dot(xa, w1_ref[0, 0:half])
    bm = _dot(xb, w1_ref[0, half:2 * half])
    nrow = x.shape[0]
    pre = (a + pltpu.roll(bm, nrow - 1, 0)) + b1_ref[0]
    hid = 0.5 * pre * (1.0 + jnp.tanh(math.sqrt(2.0 / math.pi) * (pre + 0.044715 * (pre * pre * pre))))
    o_ref[0, 0] = (_dot(hid.astype(BF16), w2_ref[0]) + b2_ref[0]).astype(o_ref.dtype)


def _compress(kvf, pos, w1, b1, w2, b2):
    b, n2, s, hd = kvf.shape
    g = n2 // 2
    nchunk = s // CMP_STRIDE
    half = CMP_STRIDE * hd
    x = kvf.reshape(b, n2, nchunk, half)
    kv = lambda bi, n: (n // g, 0, 0)
    return pl.pallas_call(
        _compress_body,
        grid=(b, n2),
        in_specs=[pl.BlockSpec((1, 1, nchunk, half), lambda bi, n: (bi, n, 0, 0)),
                  pl.BlockSpec((1, 2, half), kv),
                  pl.BlockSpec((1, 2 * half, CMP_HIDDEN), kv),
                  pl.BlockSpec((1, 1, CMP_HIDDEN), kv),
                  pl.BlockSpec((1, CMP_HIDDEN, hd), kv),
                  pl.BlockSpec((1, 1, hd), kv)],
        out_specs=pl.BlockSpec((1, 1, nchunk, hd), lambda bi, n: (bi, n, 0, 0)),
        out_shape=jax.ShapeDtypeStruct((b, n2, nchunk, hd), BF16),
        compiler_params=_params("parallel", "parallel"),
        name="nsa_compress",
    )(x, pos.reshape(2, 2, half), w1.astype(BF16), b1.reshape(2, 1, CMP_HIDDEN),
      w2.astype(BF16), b2.reshape(2, 1, hd))


def _softmax_rows(s):
    m = jnp.max(s, axis=1, keepdims=True)
    e = jnp.exp(s - m)
    return e / jnp.sum(e, axis=1, keepdims=True)


def _nsa_body(q_ref, kc_ref, vc_ref, ks_ref, vs_ref, kw_ref, vw_ref, gt_ref, ovt_ref, exp_ref,
              o_ref, bias_ref):
    i = pl.program_id(2)
    tq = q_ref.shape[1]
    r = NSA_GROUP
    q0 = i * tq
    qf = q_ref[0]
    q4 = jnp.concatenate([qf[:, h * HEAD_DIM:(h + 1) * HEAD_DIM] for h in range(r)], axis=0)
    rows = r * tq
    qpos_c = q0 + lax.broadcasted_iota(jnp.int32, (tq, 1), 0)
    qpos4 = jnp.concatenate([qpos_c] * r, axis=0)

    kc = kc_ref[0, 0]
    nc = kc.shape[0]
    s_c = _dot_nt(q4, kc)
    cmp_end = lax.broadcasted_iota(jnp.int32, (1, nc), 1) * CMP_STRIDE + (CMP_BLOCK - 1)
    p_c = _softmax_rows(jnp.where(cmp_end <= qpos4, s_c, NEG_INF))
    p_c = jnp.where(qpos4 >= CMP_BLOCK - 1, p_c, 0.0)
    o_c = _dot(p_c.astype(BF16), vc_ref[0, 0])

    p_sum = p_c[0:tq]
    for h in range(1, r):
        p_sum = p_sum + p_c[h * tq:(h + 1) * tq]
    ps_hi, ps_lo = _split_bf16(p_sum)
    ovt = ovt_ref[...]
    imp = _dot_nt(ovt, ps_hi) + _dot_nt(ovt, ps_lo)
    ns = imp.shape[0]
    blk = lax.broadcasted_iota(jnp.int32, (ns, tq), 0)
    qpos_r = q0 + lax.broadcasted_iota(jnp.int32, (ns, tq), 1)
    own = qpos_r // SLC_BLOCK
    started = blk * SLC_BLOCK <= qpos_r
    forced = (blk == 0) | (blk == own) | (blk == own - 1)
    imp = jnp.where(started, jnp.where(forced, FORCE_SCORE, imp), NEG_INF)
    cnt = jnp.zeros((ns, tq), jnp.int32)
    for m in range(ns):
        rm = imp[m:m + 1, :]
        cnt = cnt + ((rm > imp) | ((rm == imp) & (blk > m))).astype(jnp.int32)
    sel_t = jnp.where(cnt < SLC_TOPN, 1.0, 0.0)
    sel_t = jnp.concatenate([sel_t, jnp.zeros((LANES - ns, tq), F32)], axis=0)
    sel = jnp.transpose(sel_t)[:, 0:ns]
    bias_ref[...] = (_dot(sel.astype(BF16), exp_ref[...]) - 1.0) * (-NEG_INF)

    tk = NSA_TK
    jl = q0 // tk
    d0 = pl.multiple_of(jl * tk, tk)
    kpos = d0 + lax.broadcasted_iota(jnp.int32, (1, tk), 1)

    def sel_bias(k0):
        bt = bias_ref[:, pl.ds(k0, tk)]
        return jnp.concatenate([bt] * r, axis=0)

    s_d = _dot_nt(q4, ks_ref[0, 0, pl.ds(d0, tk), :]) + sel_bias(d0)
    s_d = jnp.where(kpos <= qpos4, s_d, NEG_INF)
    m_s = jnp.max(s_d, axis=1, keepdims=True)
    p_d = jnp.exp(s_d - m_s)
    l_s = jnp.sum(p_d, axis=1, keepdims=True)
    acc_s = _dot(p_d.astype(BF16), vs_ref[0, 0, pl.ds(d0, tk), :])

    def body(j, carry):
        k0 = pl.multiple_of(j * tk, tk)
        return _flash_step(q4, ks_ref[0, 0, pl.ds(k0, tk), :], vs_ref[0, 0, pl.ds(k0, tk), :],
                           sel_bias(k0), *carry)

    m_s, l_s, acc_s = lax.fori_loop(0, jl, body, (m_s, l_s, acc_s))
    o_s = acc_s / l_s

    span = WINDOW + tq
    w0 = pl.multiple_of(jnp.maximum(q0 - WINDOW, 0), tq)
    kpos_w = w0 + lax.broadcasted_iota(jnp.int32, (1, span), 1)
    band = (kpos_w <= qpos4) & (kpos_w > qpos4 - WINDOW)
    s_w = _dot_nt(q4, kw_ref[0, 0, pl.ds(w0, span), :])
    p_w = _softmax_rows(jnp.where(band, s_w, NEG_INF))
    o_w = _dot(p_w.astype(BF16), vw_ref[0, 0, pl.ds(w0, span), :])

    gt = gt_ref[0, 0]
    outs = []
    for h in range(r):
        sl = slice(h * tq, (h + 1) * tq)
        outs.append(gt[:, 3 * h:3 * h + 1] * o_c[sl] + gt[:, 3 * h + 1:3 * h + 2] * o_s[sl]
                    + gt[:, 3 * h + 2:3 * h + 3] * o_w[sl])
    o_ref[0] = jnp.concatenate(outs, axis=1).astype(o_ref.dtype)


def _nsa_attention(q, cmp, kvb, gates):
    b, s, _ = q.shape
    g = NSA_KV_HEADS
    tq = NSA_TQ
    nc = cmp.shape[2]
    ns = s // SLC_BLOCK
    cw = NSA_GROUP * HEAD_DIM
    cs = jnp.arange(nc)[None, :] * CMP_STRIDE
    ss = jnp.arange(ns)[:, None] * SLC_BLOCK
    ovt = ((cs <= ss + SLC_BLOCK - 1) & (cs + CMP_BLOCK - 1 >= ss)).astype(BF16)
    expand = (jnp.arange(s)[None, :] // SLC_BLOCK == jnp.arange(ns)[:, None]).astype(BF16)
    head = lambda off: (lambda bi, gi, i: (bi, off + gi, 0, 0))
    return pl.pallas_call(
        _nsa_body,
        grid=(b, g, s // tq),
        in_specs=[pl.BlockSpec((1, tq, cw), lambda bi, gi, i: (bi, i, gi)),
                  pl.BlockSpec((1, 1, nc, HEAD_DIM), head(0)),
                  pl.BlockSpec((1, 1, nc, HEAD_DIM), head(g)),
                  pl.BlockSpec((1, 1, s, HEAD_DIM), head(0)),
                  pl.BlockSpec((1, 1, s, HEAD_DIM), head(g)),
                  pl.BlockSpec((1, 1, s, HEAD_DIM), head(2 * g)),
                  pl.BlockSpec((1, 1, s, HEAD_DIM), head(3 * g)),
                  pl.BlockSpec((1, 1, tq, 3 * NSA_GROUP), lambda bi, gi, i: (bi, gi, i, 0)),
                  pl.BlockSpec((ns, nc), lambda bi, gi, i: (0, 0)),
                  pl.BlockSpec((ns, s), lambda bi, gi, i: (0, 0))],
        out_specs=pl.BlockSpec((1, tq, cw), lambda bi, gi, i: (bi, i, gi)),
        out_shape=jax.ShapeDtypeStruct((b, s, NSA_Q_W), BF16),
        scratch_shapes=[pltpu.VMEM((tq, s), F32)],
        compiler_params=_params("parallel", "parallel", "arbitrary"),
        name="nsa_attn",
    )(q, cmp, cmp, kvb, kvb, kvb, kvb, gates, ovt, expand)


def _final_norm_body(x_ref, g_ref, o_ref):
    x = x_ref[0]
    o_ref[0] = (x * lax.rsqrt(jnp.mean(x * x, axis=-1, keepdims=True) + NORM_EPS)) * g_ref[...]


def _final_norm(x, g):
    b, s, d = x.shape
    tm = PROJ_TM
    row = lambda bi, i: (bi, i, 0)
    return pl.pallas_call(
        _final_norm_body,
        grid=(b, s // tm),
        in_specs=[pl.BlockSpec((1, tm, d), row), pl.BlockSpec((1, d), lambda bi, i: (0, 0))],
        out_specs=pl.BlockSpec((1, tm, d), row),
        out_shape=jax.ShapeDtypeStruct((b, s, d), F32),
        compiler_params=_params("parallel", "parallel"),
        name="final_norm",
    )(x, g.reshape(1, d))


def kernel(x, c, norm1_g, norm2_g, final_g, ada_w, ada_b, ev_w_in, ev_w_out, ev_lambda, ev_subln_g,
           od_w_in, od_w_out, od_cmp_pos, od_cmp_w1, od_cmp_b1, od_cmp_w2, od_cmp_b2,
           moe_wg, moe_bg, moe_we, moe_be, moe_w_gate, moe_w_up, moe_w_down):
    b, s, d = x.shape
    tables = _rope_tables(s)
    mod = _ada_mod(c, ada_w, ada_b)
    for l in range(DEPTH):
        sh1, sc1, g1, sh2, sc2, g2 = (mod[l, :, None, k * d:(k + 1) * d] for k in range(6))
        i = l // 2
        if l % 2 == 0:
            lambda_init = 0.8 - 0.6 * math.exp(-0.3 * l)
            p, kmean = _even_proj(x, norm1_g[l].reshape(1, d), sc1, sh1, ev_w_in[i].astype(BF16), tables)
            oa = _moba_attention(p, kmean.reshape(b, s // MOBA_BLOCK, MOBA_W))
            ob = _diff_attention(p, ev_lambda[i], ev_subln_g[i], lambda_init)
            x = _out_proj(x, g1, (oa, ob), ev_w_out[i].astype(BF16))
        else:
            w = jnp.pad(od_w_in[i], ((0, 0), (0, ODD_PAD - ODD_IN))).astype(BF16)
            q, kvf, kvb, gates = _odd_proj(x, norm1_g[l].reshape(1, d), sc1, sh1, w, tables)
            cmp = _compress(kvf, od_cmp_pos[i], od_cmp_w1[i], od_cmp_b1[i], od_cmp_w2[i], od_cmp_b2[i])
            gt = gates[:, :, :N_GATES].reshape(b, s, NSA_KV_HEADS, 3 * NSA_GROUP).transpose(0, 2, 1, 3)
            o = _nsa_attention(q, cmp, kvb, gt)
            x = _out_proj(x, g1, (o,), od_w_out[i].astype(BF16))
        x = _moe(x, norm2_g[l].reshape(1, d), sc2, sh2, g2, moe_wg[l], moe_bg[l], moe_we[l], moe_be[l],
                 moe_w_gate[l], moe_w_up[l], moe_w_down[l])
    return _final_norm(x, final_g)
```

```python
import functools
import math

import jax
import jax.numpy as jnp
from jax import lax
from jax.experimental import pallas as pl
from jax.experimental.pallas import tpu as pltpu

F32 = jnp.float32
BF16 = jnp.bfloat16

D_MODEL = 1024
DEPTH = 4
HEAD_DIM = 64
ROPE_DIM = HEAD_DIM // 4
ROPE_HALF = ROPE_DIM // 2
ROPE_THETA = 500000.0
NORM_EPS = 1e-6
NEG_INF = -1e30
FORCE_SCORE = 1e6
QK_SCALE = HEAD_DIM ** -0.5

MOBA_HEADS = 8
MOBA_BLOCK = 256
MOBA_TOPK = 3
DIFF_HEADS = 4
MOBA_W = MOBA_HEADS * HEAD_DIM
DIFF_QK_W = DIFF_HEADS * 2 * HEAD_DIM
DIFF_V_W = DIFF_HEADS * 2 * HEAD_DIM
DIFF_W = 2 * DIFF_QK_W + DIFF_V_W
EVEN_IN = 3 * MOBA_W + DIFF_W

NSA_HEADS = 16
NSA_GROUP = 4
NSA_KV_HEADS = 4
CMP_BLOCK = 32
CMP_STRIDE = 16
CMP_HIDDEN = 256
SLC_BLOCK = 64
SLC_TOPN = 16
WINDOW = 512
NSA_Q_W = NSA_HEADS * HEAD_DIM
NSA_KV_W = NSA_KV_HEADS * HEAD_DIM
ODD_IN = NSA_Q_W + 6 * NSA_KV_W + 3 * NSA_HEADS
N_GATES = 3 * NSA_HEADS

MOE_GROUPS = 4
MOE_PER_GROUP = 4
MOE_EXPERTS = 16
MOE_FF = 256

LANES = 128
VMEM_LIMIT = 56 * 1024 * 1024

PROJ_TM = 512
ATT_TQ = 256
ATT_TK = 512
NSA_TQ = 128
NSA_TK = 256


def _params(*sem):
    return pltpu.CompilerParams(dimension_semantics=sem, vmem_limit_bytes=VMEM_LIMIT)


def _dot(a, b):
    return jnp.dot(a, b, preferred_element_type=F32)


def _dot_nt(a, b):
    return lax.dot_general(a, b, (((1,), (1,)), ((), ())), preferred_element_type=F32)


def _split_bf16(x):
    hi = x.astype(BF16)
    lo = (x - hi.astype(F32)).astype(BF16)
    return hi, lo


def _norm_mod(x, g, sc, sh):
    y = x * lax.rsqrt(jnp.mean(x * x, axis=-1, keepdims=True) + NORM_EPS)
    return (y * g) * (1.0 + sc) + sh


def _rope(t, c, s1, s2):
    w = t.shape[1]
    k = w // LANES
    cw = jnp.concatenate([c] * k, axis=1) if k > 1 else c
    s1w = jnp.concatenate([s1] * k, axis=1) if k > 1 else s1
    s2w = jnp.concatenate([s2] * k, axis=1) if k > 1 else s2
    return t * cw + pltpu.roll(t, ROPE_HALF, 1) * s1w + pltpu.roll(t, w - ROPE_HALF, 1) * s2w


def _rope_tables(seq):
    pos = jnp.arange(seq, dtype=F32)
    inv = ROPE_THETA ** (-jnp.arange(0, ROPE_DIM, 2, dtype=F32) / ROPE_DIM)
    ang = pos[:, None] * inv[None, :]
    cos, sin = jnp.cos(ang), jnp.sin(ang)
    ones = jnp.ones((seq, HEAD_DIM - ROPE_DIM), F32)
    zeros8 = jnp.zeros((seq, ROPE_HALF), F32)
    zeros = jnp.zeros((seq, HEAD_DIM - ROPE_DIM), F32)
    c = jnp.concatenate([cos, cos, ones], axis=1)
    s1 = jnp.concatenate([zeros8, sin, zeros], axis=1)
    s2 = jnp.concatenate([-sin, zeros8, zeros], axis=1)
    rep = LANES // HEAD_DIM
    return tuple(jnp.tile(t, (1, rep)) for t in (c, s1, s2))


def _rank_below(v, idx, k):
    cnt = jnp.zeros(v.shape, F32)
    for m in range(v.shape[0]):
        rm = v[m:m + 1, :]
        cnt = cnt + jnp.where(rm > v, 1.0, 0.0) + jnp.where(rm == v, (idx > m).astype(F32), 0.0)
    return cnt < k


def _online_step(q, k, v, mask, m, acc):
    s = _dot_nt(q, k)
    if mask is not None:
        s = jnp.where(mask, s, NEG_INF)
    m_new = jnp.max(s, axis=1, keepdims=True)
    if m is not None:
        m_new = jnp.maximum(m, m_new)
    p = jnp.exp(s - m_new).astype(v.dtype)
    pv = _dot(p, v)
    if m is None:
        return m_new, pv
    return m_new, jnp.exp(m - m_new) * acc + pv


def _ada_body(c_ref, w_ref, b_ref, o_ref):
    c = c_ref[...]
    cs = c * (1.0 / (1.0 + jnp.exp(-c)))
    o_ref[0] = jnp.dot(cs, w_ref[0], preferred_element_type=F32,
                       precision=lax.Precision.HIGHEST) + b_ref[0]


def _ada_mod(c, ada_w, ada_b):
    b, d = c.shape
    depth, _, n = ada_w.shape
    rows = 8
    tn = 1536
    cp = jnp.pad(c, ((0, rows - b), (0, 0)))
    out = pl.pallas_call(
        _ada_body,
        grid=(depth, n // tn),
        in_specs=[pl.BlockSpec((rows, d), lambda l, j: (0, 0)),
                  pl.BlockSpec((1, d, tn), lambda l, j: (l, 0, j)),
                  pl.BlockSpec((1, 1, tn), lambda l, j: (l, 0, j))],
        out_specs=pl.BlockSpec((1, rows, tn), lambda l, j: (l, 0, j)),
        out_shape=jax.ShapeDtypeStruct((depth, rows, n), F32),
        compiler_params=_params("parallel", "parallel"),
        name="ada_mod",
    )(cp, ada_w, ada_b.reshape(depth, 1, n))
    return out[:, :b]


def _even_proj_body(x_ref, g_ref, sc_ref, sh_ref, w_ref, c_ref, s1_ref, s2_ref,
                    q_ref, k_ref, v_ref, pb_ref, km_ref):
    tm = x_ref.shape[1]
    h = _norm_mod(x_ref[0], g_ref[...], sc_ref[0], sh_ref[0]).astype(BF16)
    c, s1, s2 = c_ref[...], s1_ref[...], s2_ref[...]
    ch = MOBA_W
    lane = lax.broadcasted_iota(jnp.int32, (tm, LANES), 1)
    low = lane < HEAD_DIM
    blk = (pl.program_id(1) * tm + lax.broadcasted_iota(jnp.int32, (tm, LANES), 0)) // MOBA_BLOCK
    ind_hi = jnp.where(lane == blk + HEAD_DIM, 1.0, 0.0)
    ind_lo = jnp.where(lane == blk, 1.0, 0.0)
    for idx, kind in enumerate(("q", "k", "v", "q", "k", "v")):
        acc = _dot(h, w_ref[:, idx * ch:(idx + 1) * ch])
        if kind != "v":
            acc = _rope(acc, c, s1, s2)
        if kind == "q":
            acc = acc * QK_SCALE
        if idx == 0:
            q_ref[0] = acc.astype(BF16)
        elif idx == 1:
            nblk = tm // MOBA_BLOCK
            km_ref[0, 0] = jnp.concatenate(
                [jnp.mean(acc[n * MOBA_BLOCK:(n + 1) * MOBA_BLOCK], axis=0, keepdims=True)
                 for n in range(nblk)], axis=0)
            for hp in range(ch // LANES):
                kp = acc[:, hp * LANES:(hp + 1) * LANES]
                k_ref[0, 2 * hp] = jnp.where(low, kp, ind_hi).astype(BF16)
                k_ref[0, 2 * hp + 1] = jnp.where(low, ind_lo, kp).astype(BF16)
        elif idx == 2:
            for hp in range(ch // LANES):
                vp = acc[:, hp * LANES:(hp + 1) * LANES]
                v_ref[0, 2 * hp] = jnp.where(low, vp, 1.0).astype(BF16)
                v_ref[0, 2 * hp + 1] = jnp.where(low, 1.0, vp).astype(BF16)
        else:
            pb_ref[0, :, (idx - 3) * ch:(idx - 2) * ch] = acc.astype(BF16)


def _even_proj(x, g, sc, sh, w, tables):
    b, s, d = x.shape
    n = w.shape[1]
    tm = PROJ_TM
    nblk = tm // MOBA_BLOCK
    row = lambda bi, i: (bi, i, 0)
    vec = lambda bi, i: (bi, 0, 0)
    hd = lambda bi, i: (bi, 0, i, 0)
    tab = pl.BlockSpec((tm, LANES), lambda bi, i: (i, 0))
    return pl.pallas_call(
        _even_proj_body,
        grid=(b, s // tm),
        in_specs=[pl.BlockSpec((1, tm, d), row),
                  pl.BlockSpec((1, d), lambda bi, i: (0, 0)),
                  pl.BlockSpec((1, 1, d), vec),
                  pl.BlockSpec((1, 1, d), vec),
                  pl.BlockSpec((d, n), lambda bi, i: (0, 0)),
                  tab, tab, tab],
        out_specs=[pl.BlockSpec((1, tm, MOBA_W), row),
                   pl.BlockSpec((1, MOBA_HEADS, tm, LANES), hd),
                   pl.BlockSpec((1, MOBA_HEADS, tm, LANES), hd),
                   pl.BlockSpec((1, tm, DIFF_W), row),
                   pl.BlockSpec((1, 1, nblk, MOBA_W), lambda bi, i: (bi, i, 0, 0))],
        out_shape=[jax.ShapeDtypeStruct((b, s, MOBA_W), BF16),
                   jax.ShapeDtypeStruct((b, MOBA_HEADS, s, LANES), BF16),
                   jax.ShapeDtypeStruct((b, MOBA_HEADS, s, LANES), BF16),
                   jax.ShapeDtypeStruct((b, s, DIFF_W), BF16),
                   jax.ShapeDtypeStruct((b, s // tm, nblk, MOBA_W), F32)],
        compiler_params=_params("parallel", "parallel"),
        name="even_proj",
    )(x, g, sc, sh, w, *tables)


def _moba_body(q_ref, k_ref, v_ref, km_ref, o_ref):
    i = pl.program_id(2)
    tq = q_ref.shape[1]
    nb = km_ref.shape[1]
    tk = ATT_TK
    q = q_ref[0]
    km = km_ref[0]
    lane = lax.broadcasted_iota(jnp.int32, (tq, LANES), 1)
    lane_k = lax.broadcasted_iota(jnp.int32, (nb, LANES), 1)
    n_idx = lax.broadcasted_iota(jnp.int32, (nb, tq), 0)
    last = i // 2
    c0 = pl.multiple_of(last * tk, tk)
    qpos = i * tq + lax.broadcasted_iota(jnp.int32, (tq, 1), 0)
    causal = c0 + lax.broadcasted_iota(jnp.int32, (1, tk), 1) <= qpos

    qs, state = [], []
    for e in range(2):
        own = (lane < HEAD_DIM) if e == 0 else (lane >= HEAD_DIM)
        own_k = (lane_k < HEAD_DIM) if e == 0 else (lane_k >= HEAD_DIM)
        qm = jnp.where(own, q, jnp.zeros_like(q))
        km_hi, km_lo = _split_bf16(jnp.where(own_k, km, 0.0))
        gs = _dot_nt(km_hi, qm) + _dot_nt(km_lo, qm)
        gs = jnp.where(n_idx < i, gs, NEG_INF)
        keep = (_rank_below(gs, n_idx, MOBA_TOPK) & (n_idx < i)) | (n_idx == i)
        bias_t = jnp.where(keep, 0.0, NEG_INF)
        off = HEAD_DIM * (1 - e)
        parts = [jnp.zeros((off, tq), F32)] if off else []
        pad = jnp.concatenate(parts + [bias_t, jnp.zeros((LANES - off - nb, tq), F32)], axis=0)
        bias = jnp.transpose(pad).astype(BF16)
        qe = jnp.where(own, q, bias)
        qs.append(qe)
        state.extend(_online_step(qe, k_ref[0, e, pl.ds(c0, tk), :], v_ref[0, e, pl.ds(c0, tk), :],
                                  causal, None, None))

    def body(j, carry):
        j0 = pl.multiple_of(j * tk, tk)
        out = []
        for e in range(2):
            out.extend(_online_step(qs[e], k_ref[0, e, pl.ds(j0, tk), :], v_ref[0, e, pl.ds(j0, tk), :],
                                    None, *carry[2 * e:2 * e + 2]))
        return tuple(out)

    state = lax.fori_loop(0, last, body, tuple(state))
    acc0, acc1 = state[1], state[3]
    low = lane < HEAD_DIM
    num = jnp.where(low, acc0, acc1)
    den = jnp.where(low, pltpu.roll(acc0, HEAD_DIM, 1), pltpu.roll(acc1, HEAD_DIM, 1))
    o_ref[0] = (num / den).astype(o_ref.dtype)


def _moba_attention(q, kx, vx, kmean):
    b, s, _ = q.shape
    nb = s // MOBA_BLOCK
    pairs = MOBA_W // LANES
    tq = ATT_TQ
    assert tq == MOBA_BLOCK and ATT_TK == 2 * MOBA_BLOCK
    return pl.pallas_call(
        _moba_body,
        grid=(b, pairs, s // tq),
        in_specs=[pl.BlockSpec((1, tq, LANES), lambda bi, hp, i: (bi, i, hp)),
                  pl.BlockSpec((1, 2, s, LANES), lambda bi, hp, i: (bi, hp, 0, 0)),
                  pl.BlockSpec((1, 2, s, LANES), lambda bi, hp, i: (bi, hp, 0, 0)),
                  pl.BlockSpec((1, nb, LANES), lambda bi, hp, i: (bi, 0, hp))],
        out_specs=pl.BlockSpec((1, tq, LANES), lambda bi, hp, i: (bi, i, hp)),
        out_shape=jax.ShapeDtypeStruct((b, s, MOBA_W), BF16),
        compiler_params=_params("parallel", "parallel", "arbitrary"),
        name="moba_attn",
    )(q, kx, vx, kmean)


def _diff_body(lam_ref, g_ref, q_ref, k_ref, v_ref, o_ref, *, lambda_init):
    i = pl.program_id(2)
    tq = q_ref.shape[1]
    tk = ATT_TK
    lp = lam_ref[...]
    lam = (jnp.exp(jnp.sum(lp[0:1] * lp[1:2], axis=1, keepdims=True))
           - jnp.exp(jnp.sum(lp[2:3] * lp[3:4], axis=1, keepdims=True)) + lambda_init)
    q = q_ref[0]
    lane = lax.broadcasted_iota(jnp.int32, (tq, LANES), 1)
    qs = [jnp.where(lane < HEAD_DIM, q, jnp.zeros_like(q)), jnp.where(lane >= HEAD_DIM, q, jnp.zeros_like(q))]
    ones = jnp.ones((tk, LANES), BF16)
    last = i // 2
    c0 = pl.multiple_of(last * tk, tk)
    qpos = i * tq + lax.broadcasted_iota(jnp.int32, (tq, 1), 0)
    causal = c0 + lax.broadcasted_iota(jnp.int32, (1, tk), 1) <= qpos

    def kv(k0):
        return k_ref[0, pl.ds(k0, tk), :], jnp.concatenate([v_ref[0, pl.ds(k0, tk), :], ones], axis=1)

    kd, vd = kv(c0)
    state = []
    for m in range(2):
        state.extend(_online_step(qs[m], kd, vd, causal, None, None))

    def body(j, carry):
        kj, vj = kv(pl.multiple_of(j * tk, tk))
        out = []
        for m in range(2):
            out.extend(_online_step(qs[m], kj, vj, None, *carry[2 * m:2 * m + 2]))
        return tuple(out)

    state = lax.fori_loop(0, last, body, tuple(state))
    a0, a1 = state[1], state[3]
    o = a0[:, :LANES] / a0[:, LANES:] - lam * (a1[:, :LANES] / a1[:, LANES:])
    y = o * lax.rsqrt(jnp.mean(o * o, axis=-1, keepdims=True) + NORM_EPS)
    o_ref[0] = ((y * g_ref[...]) * (1.0 - lambda_init)).astype(o_ref.dtype)


def _diff_attention(pb, lam_p, subln_g, lambda_init):
    b, s, _ = pb.shape
    tq = ATT_TQ
    koff = DIFF_QK_W // LANES
    voff = 2 * koff
    return pl.pallas_call(
        functools.partial(_diff_body, lambda_init=lambda_init),
        grid=(b, DIFF_HEADS, s // tq),
        in_specs=[pl.BlockSpec((4, HEAD_DIM), lambda bi, h, i: (0, 0)),
                  pl.BlockSpec((1, LANES), lambda bi, h, i: (0, 0)),
                  pl.BlockSpec((1, tq, LANES), lambda bi, h, i: (bi, i, h)),
                  pl.BlockSpec((1, s, LANES), lambda bi, h, i: (bi, 0, koff + h)),
                  pl.BlockSpec((1, s, LANES), lambda bi, h, i: (bi, 0, voff + h))],
        out_specs=pl.BlockSpec((1, tq, LANES), lambda bi, h, i: (bi, i, h)),
        out_shape=jax.ShapeDtypeStruct((b, s, DIFF_V_W), BF16),
        compiler_params=_params("parallel", "parallel", "arbitrary"),
        name="diff_attn",
    )(lam_p, subln_g.reshape(1, LANES), pb, pb, pb)


def _out_proj_body(*refs, widths):
    x_ref, g_ref = refs[0], refs[1]
    o_refs = refs[2:2 + len(widths)]
    w_ref, out_ref = refs[2 + len(widths)], refs[3 + len(widths)]
    y = None
    r0 = 0
    for o_ref, wd in zip(o_refs, widths):
        t = _dot(o_ref[0], w_ref[r0:r0 + wd, :])
        y = t if y is None else y + t
        r0 += wd
    out_ref[0] = x_ref[0] + g_ref[0] * y


def _out_proj(x, gate, mixes, w):
    b, s, d = x.shape
    tm = PROJ_TM
    widths = tuple(m.shape[2] for m in mixes)
    row = lambda bi, i: (bi, i, 0)
    return pl.pallas_call(
        functools.partial(_out_proj_body, widths=widths),
        grid=(b, s // tm),
        in_specs=[pl.BlockSpec((1, tm, d), row),
                  pl.BlockSpec((1, 1, d), lambda bi, i: (bi, 0, 0))]
                 + [pl.BlockSpec((1, tm, wd), row) for wd in widths]
                 + [pl.BlockSpec(w.shape, lambda bi, i: (0, 0))],
        out_specs=pl.BlockSpec((1, tm, d), row),
        out_shape=jax.ShapeDtypeStruct((b, s, d), F32),
        compiler_params=_params("parallel", "parallel"),
        name="out_proj",
    )(x, gate, *mixes, w)


def _first_argmax_onehot(v, iota):
    mx = jnp.max(v, axis=1, keepdims=True)
    idx = jnp.min(jnp.where(v == mx, iota, float(v.shape[1])), axis=1, keepdims=True)
    return iota == idx, mx


def _moe_body(x_ref, g_ref, sc_ref, sh_ref, gate_ref, wr_hi_ref, wr_lo_ref, br_ref,
              wg_ref, wu_ref, wd_ref, o_ref, a_ref):
    x = x_ref[0]
    h = _norm_mod(x, g_ref[...], sc_ref[0], sh_ref[0])
    h_hi, h_lo = _split_bf16(h)
    wr_hi = wr_hi_ref[...]
    r = (_dot(h_hi, wr_hi) + (_dot(h_lo, wr_hi) + _dot(h_hi, wr_lo_ref[...]))) + br_ref[...]
    tm = x.shape[0]
    gl = r[:, 0:MOE_GROUPS]
    iota = lax.broadcasted_iota(jnp.int32, (tm, MOE_GROUPS), 1).astype(F32)
    g_oh, g_mx = _first_argmax_onehot(gl, iota)
    gw = 1.0 / jnp.sum(jnp.exp(gl - g_mx), axis=1, keepdims=True)
    el_g = jnp.zeros((tm, MOE_PER_GROUP), F32)
    for g in range(MOE_GROUPS):
        lo = MOE_GROUPS + g * MOE_PER_GROUP
        el_g = el_g + jnp.where(g_oh[:, g:g + 1], r[:, lo:lo + MOE_PER_GROUP], 0.0)
    oh1, v1 = _first_argmax_onehot(el_g, iota)
    oh2, v2 = _first_argmax_onehot(jnp.where(oh1, -jnp.inf, el_g), iota)
    e2 = jnp.exp(v2 - v1)
    den = 1.0 + e2
    w_grp = jnp.where(oh1, 1.0 / den, 0.0) + jnp.where(oh2, e2 / den, 0.0)
    gsc = jnp.where(g_oh, gw, 0.0)

    hb = h_hi
    for e in range(MOE_EXPERTS):
        g, k = divmod(e, MOE_PER_GROUP)
        comb = gsc[:, g:g + 1] * w_grp[:, k:k + 1]
        gt = _dot(hb, wg_ref[e])
        up = _dot(hb, wu_ref[e])
        a = ((gt * (1.0 / (1.0 + jnp.exp(-gt)))) * up) * comb
        a_ref[:, e * MOE_FF:(e + 1) * MOE_FF] = a.astype(BF16)
    y = _dot(a_ref[...], wd_ref[...])
    o_ref[0] = x + gate_ref[0] * y


def _moe(x, g, sc, sh, gate, wg, bg, we, be, w_gate, w_up, w_down):
    b, s, d = x.shape
    tm = PROJ_TM
    nr = MOE_GROUPS + MOE_EXPERTS
    wr = jnp.pad(jnp.concatenate([wg, we], axis=1), ((0, 0), (0, LANES - nr)))
    br = jnp.pad(jnp.concatenate([bg, be], axis=0), (0, LANES - nr)).reshape(1, LANES)
    wr_hi, wr_lo = _split_bf16(wr)
    row = lambda bi, i: (bi, i, 0)
    vec = lambda bi, i: (bi, 0, 0)
    const2 = lambda bi, i: (0, 0)
    const3 = lambda bi, i: (0, 0, 0)
    once = pl.Buffered(1)
    return pl.pallas_call(
        _moe_body,
        grid=(b, s // tm),
        in_specs=[pl.BlockSpec((1, tm, d), row),
                  pl.BlockSpec((1, d), const2),
                  pl.BlockSpec((1, 1, d), vec),
                  pl.BlockSpec((1, 1, d), vec),
                  pl.BlockSpec((1, 1, d), vec),
                  pl.BlockSpec((d, LANES), const2),
                  pl.BlockSpec((d, LANES), const2),
                  pl.BlockSpec((1, LANES), const2),
                  pl.BlockSpec((MOE_EXPERTS, d, MOE_FF), const3, pipeline_mode=once),
                  pl.BlockSpec((MOE_EXPERTS, d, MOE_FF), const3, pipeline_mode=once),
                  pl.BlockSpec((MOE_EXPERTS * MOE_FF, d), const2, pipeline_mode=once)],
        out_specs=pl.BlockSpec((1, tm, d), row),
        out_shape=jax.ShapeDtypeStruct((b, s, d), F32),
        scratch_shapes=[pltpu.VMEM((tm, MOE_EXPERTS * MOE_FF), BF16)],
        compiler_params=_params("parallel", "parallel"),
        name="hier_moe",
    )(x, g, sc, sh, gate, wr_hi, wr_lo, br, w_gate.astype(BF16), w_up.astype(BF16),
      w_down.astype(BF16).reshape(MOE_EXPERTS * MOE_FF, d))


ODD_PAD = 2688


def _odd_proj_body(x_ref, g_ref, sc_ref, sh_ref, w_ref, c_ref, s1_ref, s2_ref,
                   q_ref, kvf_ref, kvx_ref, gates_ref):
    tm = x_ref.shape[1]
    h = _norm_mod(x_ref[0], g_ref[...], sc_ref[0], sh_ref[0]).astype(BF16)
    c, s1, s2 = c_ref[...], s1_ref[...], s2_ref[...]
    half = NSA_Q_W // 2
    for idx in range(2):
        acc = _dot(h, w_ref[:, idx * half:(idx + 1) * half])
        q_ref[0, :, idx * half:(idx + 1) * half] = (_rope(acc, c, s1, s2) * QK_SCALE).astype(BF16)
    lane = lax.broadcasted_iota(jnp.int32, (tm, LANES), 1)
    low = lane < HEAD_DIM
    blk = (pl.program_id(1) * tm + lax.broadcasted_iota(jnp.int32, (tm, LANES), 0)) // SLC_BLOCK
    ind = jnp.where(lane == blk + HEAD_DIM, 1.0, 0.0)
    for idx in range(6):
        c0 = NSA_Q_W + idx * NSA_KV_W
        acc = _dot(h, w_ref[:, c0:c0 + NSA_KV_W])
        if idx % 2 == 0:
            acc = _rope(acc, c, s1, s2)
        if idx < 2:
            for g in range(NSA_KV_HEADS):
                kvf_ref[0, idx * NSA_KV_HEADS + g] = acc[:, g * HEAD_DIM:(g + 1) * HEAD_DIM]
            continue
        fill = 1.0 if idx % 2 == 1 else (ind if idx == 2 else 0.0)
        for gp in range(NSA_KV_HEADS // 2):
            pair = acc[:, gp * LANES:(gp + 1) * LANES]
            swapped = pltpu.roll(pair, HEAD_DIM, 1)
            kvx_ref[0, (idx - 2) * NSA_KV_HEADS + 2 * gp] = jnp.where(low, pair, fill).astype(BF16)
            kvx_ref[0, (idx - 2) * NSA_KV_HEADS + 2 * gp + 1] = jnp.where(low, swapped, fill).astype(BF16)
    c0 = NSA_Q_W + 6 * NSA_KV_W
    gl = _dot(h, w_ref[:, c0:c0 + LANES])
    gates_ref[0] = 1.0 / (1.0 + jnp.exp(-gl))


def _odd_proj(x, g, sc, sh, w, tables):
    b, s, d = x.shape
    n = w.shape[1]
    tm = PROJ_TM
    row = lambda bi, i: (bi, i, 0)
    vec = lambda bi, i: (bi, 0, 0)
    tab = pl.BlockSpec((tm, LANES), lambda bi, i: (i, 0))
    hd = lambda bi, i: (bi, 0, i, 0)
    return pl.pallas_call(
        _odd_proj_body,
        grid=(b, s // tm),
        in_specs=[pl.BlockSpec((1, tm, d), row),
                  pl.BlockSpec((1, d), lambda bi, i: (0, 0)),
                  pl.BlockSpec((1, 1, d), vec),
                  pl.BlockSpec((1, 1, d), vec),
                  pl.BlockSpec((d, n), lambda bi, i: (0, 0)),
                  tab, tab, tab],
        out_specs=[pl.BlockSpec((1, tm, NSA_Q_W), row),
                   pl.BlockSpec((1, 2 * NSA_KV_HEADS, tm, HEAD_DIM), hd),
                   pl.BlockSpec((1, 4 * NSA_KV_HEADS, tm, LANES), hd),
                   pl.BlockSpec((1, tm, LANES), row)],
        out_shape=[jax.ShapeDtypeStruct((b, s, NSA_Q_W), BF16),
                   jax.ShapeDtypeStruct((b, 2 * NSA_KV_HEADS, s, HEAD_DIM), F32),
                   jax.ShapeDtypeStruct((b, 4 * NSA_KV_HEADS, s, LANES), BF16),
                   jax.ShapeDtypeStruct((b, s, LANES), F32)],
        compiler_params=_params("parallel", "parallel"),
        name="odd_proj",
    )(x, g, sc, sh, w, *tables)


def _compress_body(x_ref, pos_ref, w1_ref, b1_ref, w2_ref, b2_ref, o_ref):
    x = x_ref[0, 0]
    half = CMP_STRIDE * HEAD_DIM
    xa = (x + pos_ref[0, 0:1]).astype(BF16)
    xb = (x + pos_ref[0, 1:2]).astype(BF16)
    a = _dot(xa, w1_ref[0, 0:half])
    bm = _dot(xb, w1_ref[0, half:2 * half])
    nrow = x.shape[0]
    pre = (a + pltpu.roll(bm, nrow - 1, 0)) + b1_ref[0]
    hid = 0.5 * pre * (1.0 + jnp.tanh(math.sqrt(2.0 / math.pi) * (pre + 0.044715 * (pre * pre * pre))))
    o_ref[0, 0] = (_dot(hid.astype(BF16), w2_ref[0]) + b2_ref[0]).astype(o_ref.dtype)


def _compress(kvf, pos, w1, b1, w2, b2):
    b, n2, s, hd = kvf.shape
    g = n2 // 2
    nchunk = s // CMP_STRIDE
    half = CMP_STRIDE * hd
    x = kvf.reshape(b, n2, nchunk, half)
    kv = lambda bi, n: (n // g, 0, 0)
    return pl.pallas_call(
        _compress_body,
        grid=(b, n2),
        in_specs=[pl.BlockSpec((1, 1, nchunk, half), lambda bi, n: (bi, n, 0, 0)),
                  pl.BlockSpec((1, 2, half), kv),
                  pl.BlockSpec((1, 2 * half, CMP_HIDDEN), kv),
                  pl.BlockSpec((1, 1, CMP_HIDDEN), kv),
                  pl.BlockSpec((1, CMP_HIDDEN, hd), kv),
                  pl.BlockSpec((1, 1, hd), kv)],
        out_specs=pl.BlockSpec((1, 1, nchunk, hd), lambda bi, n: (bi, n, 0, 0)),
        out_shape=jax.ShapeDtypeStruct((b, n2, nchunk, hd), BF16),
        compiler_params=_params("parallel", "parallel"),
        name="nsa_compress",
    )(x, pos.reshape(2, 2, half), w1.astype(BF16), b1.reshape(2, 1, CMP_HIDDEN),
      w2.astype(BF16), b2.reshape(2, 1, hd))


def _nsa_body(q_ref, kc_ref, vc_ref, ks_ref, vs_ref, kw_ref, vw_ref, gt_ref, ovt_ref, o_ref):
    i = pl.program_id(2)
    tq = q_ref.shape[1]
    r = NSA_GROUP
    q0 = i * tq
    qf = q_ref[0]
    q4 = jnp.concatenate([qf[:, h * HEAD_DIM:(h + 1) * HEAD_DIM] for h in range(r)], axis=0)
    qpos_c = q0 + lax.broadcasted_iota(jnp.int32, (tq, 1), 0)
    qpos4 = jnp.concatenate([qpos_c] * r, axis=0)

    kc = kc_ref[0, 0]
    nc = kc.shape[0]
    s_c = _dot_nt(q4, kc)
    cmp_end = lax.broadcasted_iota(jnp.int32, (1, nc), 1) * CMP_STRIDE + (CMP_BLOCK - 1)
    s_c = jnp.where(cmp_end <= qpos4, s_c, NEG_INF)
    e_c = jnp.exp(s_c - jnp.max(s_c, axis=1, keepdims=True))
    p_c = e_c / jnp.sum(e_c, axis=1, keepdims=True)
    p_c = jnp.where(qpos4 >= CMP_BLOCK - 1, p_c, 0.0)
    o_c = _dot(p_c.astype(BF16), vc_ref[0, 0])

    p_sum = p_c[0:tq]
    for h in range(1, r):
        p_sum = p_sum + p_c[h * tq:(h + 1) * tq]
    ps_hi, ps_lo = _split_bf16(p_sum)
    ovt = ovt_ref[...]
    imp = _dot_nt(ovt, ps_hi) + _dot_nt(ovt, ps_lo)
    ns = imp.shape[0]
    blk = lax.broadcasted_iota(jnp.int32, (ns, tq), 0)
    qpos_r = q0 + lax.broadcasted_iota(jnp.int32, (ns, tq), 1)
    own = qpos_r // SLC_BLOCK
    started = blk * SLC_BLOCK <= qpos_r
    forced = (blk == 0) | (blk == own) | (blk == own - 1)
    imp = jnp.where(started, jnp.where(forced, FORCE_SCORE, imp), NEG_INF)
    bias_t = jnp.where(_rank_below(imp, blk, SLC_TOPN), 0.0, NEG_INF)
    parts = [jnp.zeros((HEAD_DIM, tq), F32), bias_t]
    if ns < LANES - HEAD_DIM:
        parts.append(jnp.zeros((LANES - HEAD_DIM - ns, tq), F32))
    pad = jnp.concatenate(parts, axis=0)
    bias = jnp.transpose(pad).astype(BF16)
    lane4 = lax.broadcasted_iota(jnp.int32, (r * tq, LANES), 1)
    qz = jnp.concatenate([q4, jnp.zeros_like(q4)], axis=1)
    qs = jnp.where(lane4 < HEAD_DIM, qz, jnp.concatenate([bias] * r, axis=0))

    tk = NSA_TK
    jl = q0 // tk
    d0 = pl.multiple_of(jl * tk, tk)
    causal = d0 + lax.broadcasted_iota(jnp.int32, (1, tk), 1) <= qpos4
    m_s, acc_s = _online_step(qs, ks_ref[0, 0, pl.ds(d0, tk), :], vs_ref[0, 0, pl.ds(d0, tk), :],
                              causal, None, None)

    def body(j, carry):
        k0 = pl.multiple_of(j * tk, tk)
        return _online_step(qs, ks_ref[0, 0, pl.ds(k0, tk), :], vs_ref[0, 0, pl.ds(k0, tk), :],
                            None, *carry)

    m_s, acc_s = lax.fori_loop(0, jl, body, (m_s, acc_s))
    o_s = acc_s[:, :HEAD_DIM] / acc_s[:, HEAD_DIM:]

    span = WINDOW + tq
    w0 = pl.multiple_of(jnp.maximum(q0 - WINDOW, 0), tq)
    kpos_w = w0 + lax.broadcasted_iota(jnp.int32, (1, span), 1)
    band = (kpos_w <= qpos4) & (kpos_w > qpos4 - WINDOW)
    _, acc_w = _online_step(qz, kw_ref[0, 0, pl.ds(w0, span), :], vw_ref[0, 0, pl.ds(w0, span), :],
                            band, None, None)
    o_w = acc_w[:, :HEAD_DIM] / acc_w[:, HEAD_DIM:]

    gt = gt_ref[0, 0]
    outs = []
    for h in range(r):
        sl = slice(h * tq, (h + 1) * tq)
        outs.append(gt[:, 3 * h:3 * h + 1] * o_c[sl] + gt[:, 3 * h + 1:3 * h + 2] * o_s[sl]
                    + gt[:, 3 * h + 2:3 * h + 3] * o_w[sl])
    o_ref[0] = jnp.concatenate(outs, axis=1).astype(o_ref.dtype)


def _nsa_attention(q, cmp, kvx, gates):
    b, s, _ = q.shape
    g = NSA_KV_HEADS
    tq = NSA_TQ
    nc = cmp.shape[2]
    ns = s // SLC_BLOCK
    assert ns <= LANES - HEAD_DIM
    cw = NSA_GROUP * HEAD_DIM
    cs = jnp.arange(nc)[None, :] * CMP_STRIDE
    ss = jnp.arange(ns)[:, None] * SLC_BLOCK
    ovt = ((cs <= ss + SLC_BLOCK - 1) & (cs + CMP_BLOCK - 1 >= ss)).astype(BF16)
    head = lambda off: (lambda bi, gi, i: (bi, off + gi, 0, 0))
    return pl.pallas_call(
        _nsa_body,
        grid=(b, g, s // tq),
        in_specs=[pl.BlockSpec((1, tq, cw), lambda bi, gi, i: (bi, i, gi)),
                  pl.BlockSpec((1, 1, nc, HEAD_DIM), head(0)),
                  pl.BlockSpec((1, 1, nc, HEAD_DIM), head(g)),
                  pl.BlockSpec((1, 1, s, LANES), head(0)),
                  pl.BlockSpec((1, 1, s, LANES), head(g)),
                  pl.BlockSpec((1, 1, s, LANES), head(2 * g)),
                  pl.BlockSpec((1, 1, s, LANES), head(3 * g)),
                  pl.BlockSpec((1, 1, tq, 3 * NSA_GROUP), lambda bi, gi, i: (bi, gi, i, 0)),
                  pl.BlockSpec((ns, nc), lambda bi, gi, i: (0, 0))],
        out_specs=pl.BlockSpec((1, tq, cw), lambda bi, gi, i: (bi, i, gi)),
        out_shape=jax.ShapeDtypeStruct((b, s, NSA_Q_W), BF16),
        compiler_params=_params("parallel", "parallel", "arbitrary"),
        name="nsa_attn",
    )(q, cmp, cmp, kvx, kvx, kvx, kvx, gates, ovt)


def _final_norm_body(x_ref, g_ref, o_ref):
    x = x_ref[0]
    o_ref[0] = (x * lax.rsqrt(jnp.mean(x * x, axis=-1, keepdims=True) + NORM_EPS)) * g_ref[...]


def _final_norm(x, g):
    b, s, d = x.shape
    tm = PROJ_TM
    row = lambda bi, i: (bi, i, 0)
    return pl.pallas_call(
        _final_norm_body,
        grid=(b, s // tm),
        in_specs=[pl.BlockSpec((1, tm, d), row), pl.BlockSpec((1, d), lambda bi, i: (0, 0))],
        out_specs=pl.BlockSpec((1, tm, d), row),
        out_shape=jax.ShapeDtypeStruct((b, s, d), F32),
        compiler_params=_params("parallel", "parallel"),
        name="final_norm",
    )(x, g.reshape(1, d))


def kernel(x, c, norm1_g, norm2_g, final_g, ada_w, ada_b, ev_w_in, ev_w_out, ev_lambda, ev_subln_g,
           od_w_in, od_w_out, od_cmp_pos, od_cmp_w1, od_cmp_b1, od_cmp_w2, od_cmp_b2,
           moe_wg, moe_bg, moe_we, moe_be, moe_w_gate, moe_w_up, moe_w_down):
    b, s, d = x.shape
    tables = _rope_tables(s)
    mod = _ada_mod(c, ada_w, ada_b)
    for l in range(DEPTH):
        sh1, sc1, g1, sh2, sc2, g2 = (mod[l, :, None, k * d:(k + 1) * d] for k in range(6))
        i = l // 2
        if l % 2 == 0:
            lambda_init = 0.8 - 0.6 * math.exp(-0.3 * l)
            qa, kx, vx, pb, kmean = _even_proj(x, norm1_g[l].reshape(1, d), sc1, sh1,
                                               ev_w_in[i].astype(BF16), tables)
            oa = _moba_attention(qa, kx, vx, kmean.reshape(b, s // MOBA_BLOCK, MOBA_W))
            ob = _diff_attention(pb, ev_lambda[i], ev_subln_g[i], lambda_init)
            x = _out_proj(x, g1, (oa, ob), ev_w_out[i].astype(BF16))
        else:
            w = jnp.pad(od_w_in[i], ((0, 0), (0, ODD_PAD - ODD_IN))).astype(BF16)
            q, kvf, kvx, gates = _odd_proj(x, norm1_g[l].reshape(1, d), sc1, sh1, w, tables)
            cmp = _compress(kvf, od_cmp_pos[i], od_cmp_w1[i], od_cmp_b1[i], od_cmp_w2[i], od_cmp_b2[i])
            gt = gates[:, :, :N_GATES].reshape(b, s, NSA_KV_HEADS, 3 * NSA_GROUP).transpose(0, 2, 1, 3)
            o = _nsa_attention(q, cmp, kvx, gt)
            x = _out_proj(x, g1, (o,), od_w_out[i].astype(BF16))
        x = _moe(x, norm2_g[l].reshape(1, d), sc2, sh2, g2, moe_wg[l], moe_bg[l], moe_we[l], moe_be[l],
                 moe_w_gate[l], moe_w_up[l], moe_w_down[l])
    return _final_norm(x, final_g)
```

```python
import functools
import math

import jax
import jax.numpy as jnp
from jax import lax
from jax.experimental import pallas as pl
from jax.experimental.pallas import tpu as pltpu

F32 = jnp.float32
BF16 = jnp.bfloat16

D_MODEL = 1024
DEPTH = 4
HEAD_DIM = 64
ROPE_DIM = HEAD_DIM // 4
ROPE_HALF = ROPE_DIM // 2
ROPE_THETA = 500000.0
NORM_EPS = 1e-6
NEG_INF = -1e30
FORCE_SCORE = 1e6
QK_SCALE = HEAD_DIM ** -0.5

MOBA_HEADS = 8
MOBA_BLOCK = 256
MOBA_TOPK = 3
DIFF_HEADS = 4
MOBA_W = MOBA_HEADS * HEAD_DIM
DIFF_QK_W = DIFF_HEADS * 2 * HEAD_DIM
DIFF_V_W = DIFF_HEADS * 2 * HEAD_DIM
DIFF_W = 2 * DIFF_QK_W + DIFF_V_W
EVEN_IN = 3 * MOBA_W + DIFF_W

NSA_HEADS = 16
NSA_GROUP = 4
NSA_KV_HEADS = 4
CMP_BLOCK = 32
CMP_STRIDE = 16
CMP_HIDDEN = 256
SLC_BLOCK = 64
SLC_TOPN = 16
WINDOW = 512
NSA_Q_W = NSA_HEADS * HEAD_DIM
NSA_KV_W = NSA_KV_HEADS * HEAD_DIM
ODD_IN = NSA_Q_W + 6 * NSA_KV_W + 3 * NSA_HEADS
N_GATES = 3 * NSA_HEADS

MOE_GROUPS = 4
MOE_PER_GROUP = 4
MOE_EXPERTS = 16
MOE_FF = 256

LANES = 128
VMEM_LIMIT = 56 * 1024 * 1024

PROJ_TM = 512
ATT_TQ = 256
ATT_TK = 512
NSA_TQ = 128
NSA_TK = 512


def _params(*sem):
    return pltpu.CompilerParams(dimension_semantics=sem, vmem_limit_bytes=VMEM_LIMIT)


def _dot(a, b):
    return jnp.dot(a, b, preferred_element_type=F32)


def _dot_nt(a, b):
    return lax.dot_general(a, b, (((1,), (1,)), ((), ())), preferred_element_type=F32)


def _split_bf16(x):
    hi = x.astype(BF16)
    lo = (x - hi.astype(F32)).astype(BF16)
    return hi, lo


def _norm_mod(x, g, sc, sh):
    y = x * lax.rsqrt(jnp.mean(x * x, axis=-1, keepdims=True) + NORM_EPS)
    return (y * g) * (1.0 + sc) + sh


def _rope(t, c, s1, s2):
    w = t.shape[1]
    k = w // LANES
    cw = jnp.concatenate([c] * k, axis=1) if k > 1 else c
    s1w = jnp.concatenate([s1] * k, axis=1) if k > 1 else s1
    s2w = jnp.concatenate([s2] * k, axis=1) if k > 1 else s2
    return t * cw + pltpu.roll(t, ROPE_HALF, 1) * s1w + pltpu.roll(t, w - ROPE_HALF, 1) * s2w


def _rope_tables(seq):
    pos = jnp.arange(seq, dtype=F32)
    inv = ROPE_THETA ** (-jnp.arange(0, ROPE_DIM, 2, dtype=F32) / ROPE_DIM)
    ang = pos[:, None] * inv[None, :]
    cos, sin = jnp.cos(ang), jnp.sin(ang)
    ones = jnp.ones((seq, HEAD_DIM - ROPE_DIM), F32)
    zeros8 = jnp.zeros((seq, ROPE_HALF), F32)
    zeros = jnp.zeros((seq, HEAD_DIM - ROPE_DIM), F32)
    c = jnp.concatenate([cos, cos, ones], axis=1)
    s1 = jnp.concatenate([zeros8, sin, zeros], axis=1)
    s2 = jnp.concatenate([-sin, zeros8, zeros], axis=1)
    rep = LANES // HEAD_DIM
    return tuple(jnp.tile(t, (1, rep)) for t in (c, s1, s2))


def _rank_below(v, idx, k, rows):
    cnt = jnp.zeros(v.shape, F32)
    for m in range(rows):
        rm = v[m:m + 1, :]
        cnt = cnt + jnp.where(rm > v, 1.0, 0.0) + jnp.where(rm == v, (idx > m).astype(F32), 0.0)
    return cnt < k


def _online_step(q, k, v, mask, m, acc):
    s = _dot_nt(q, k)
    if mask is not None:
        s = jnp.where(mask, s, NEG_INF)
    m_new = jnp.max(s, axis=1, keepdims=True)
    if m is not None:
        m_new = jnp.maximum(m, m_new)
    p = jnp.exp(s - m_new).astype(v.dtype)
    pv = _dot(p, v)
    if m is None:
        return m_new, pv
    return m_new, jnp.exp(m - m_new) * acc + pv


def _attn_pair(n_past, max_a, cnt_a, tk, qs, state, key, value, late_pv):
    ns = len(qs[0])
    q_cur = list(qs[0])
    m_cur = [state[0][e][0] for e in range(ns)]
    acc_cur = [state[0][e][1] for e in range(ns)]
    out_a = list(acc_cur)
    pend = None

    def flush():
        return [acc_cur[e] + _dot(pend[0][e], value(e, pend[1])) for e in range(ns)]

    for u in range(n_past):
        if pend is not None:
            acc_cur = flush()
        if u <= max_a:
            sw = u == cnt_a
            out_a = [jnp.where(sw, acc_cur[e], out_a[e]) for e in range(ns)]
            acc_cur = [jnp.where(sw, state[1][e][1], acc_cur[e]) for e in range(ns)]
            m_cur = [jnp.where(sw, state[1][e][0], m_cur[e]) for e in range(ns)]
            q_cur = [jnp.where(sw, qs[1][e], q_cur[e]) for e in range(ns)]
        chunk = jnp.where(u < cnt_a, u, u - cnt_a) if u < max_a else u - cnt_a
        k0 = pl.multiple_of(chunk * tk, tk)
        if not late_pv:
            for e in range(ns):
                m_cur[e], acc_cur[e] = _online_step(q_cur[e], key(e, k0), value(e, k0), None,
                                                    m_cur[e], acc_cur[e])
            continue
        ss = [_dot_nt(q_cur[e], key(e, k0)) for e in range(ns)]
        ps = []
        for e in range(ns):
            m_new = jnp.maximum(m_cur[e], jnp.max(ss[e], axis=1, keepdims=True))
            ps.append(jnp.exp(ss[e] - m_new).astype(BF16))
            acc_cur[e] = jnp.exp(m_cur[e] - m_new) * acc_cur[e]
            m_cur[e] = m_new
        pend = (ps, k0)
    if pend is not None:
        acc_cur = flush()
    return [out_a, acc_cur]


def _ada_body(c_ref, w_ref, b_ref, o_ref):
    c = c_ref[...]
    cs = c * (1.0 / (1.0 + jnp.exp(-c)))
    o_ref[0] = jnp.dot(cs, w_ref[0], preferred_element_type=F32,
                       precision=lax.Precision.HIGHEST) + b_ref[0]


def _ada_mod(c, ada_w, ada_b):
    b, d = c.shape
    depth, _, n = ada_w.shape
    rows = 8
    tn = 1536
    cp = jnp.pad(c, ((0, rows - b), (0, 0)))
    out = pl.pallas_call(
        _ada_body,
        grid=(depth, n // tn),
        in_specs=[pl.BlockSpec((rows, d), lambda l, j: (0, 0)),
                  pl.BlockSpec((1, d, tn), lambda l, j: (l, 0, j)),
                  pl.BlockSpec((1, 1, tn), lambda l, j: (l, 0, j))],
        out_specs=pl.BlockSpec((1, rows, tn), lambda l, j: (l, 0, j)),
        out_shape=jax.ShapeDtypeStruct((depth, rows, n), F32),
        compiler_params=_params("parallel", "parallel"),
        name="ada_mod",
    )(cp, ada_w, ada_b.reshape(depth, 1, n))
    return out[:, :b]


def _even_proj_body(x_ref, g_ref, sc_ref, sh_ref, w_ref, c_ref, s1_ref, s2_ref,
                    q_ref, k_ref, v_ref, pb_ref, km_ref):
    tm = x_ref.shape[1]
    h = _norm_mod(x_ref[0], g_ref[...], sc_ref[0], sh_ref[0]).astype(BF16)
    c, s1, s2 = c_ref[...], s1_ref[...], s2_ref[...]
    ch = MOBA_W
    lane = lax.broadcasted_iota(jnp.int32, (tm, LANES), 1)
    low = lane < HEAD_DIM
    blk = (pl.program_id(1) * tm + lax.broadcasted_iota(jnp.int32, (tm, LANES), 0)) // MOBA_BLOCK
    ind_hi = jnp.where(lane == blk + HEAD_DIM, 1.0, 0.0)
    ind_lo = jnp.where(lane == blk, 1.0, 0.0)
    for idx, kind in enumerate(("q", "k", "v", "q", "k", "v")):
        acc = _dot(h, w_ref[:, idx * ch:(idx + 1) * ch])
        if kind != "v":
            acc = _rope(acc, c, s1, s2)
        if kind == "q":
            acc = acc * QK_SCALE
        if idx == 0:
            q_ref[0] = acc.astype(BF16)
        elif idx == 1:
            nblk = tm // MOBA_BLOCK
            km_ref[0, 0] = jnp.concatenate(
                [jnp.mean(acc[n * MOBA_BLOCK:(n + 1) * MOBA_BLOCK], axis=0, keepdims=True)
                 for n in range(nblk)], axis=0)
            for hp in range(ch // LANES):
                kp = acc[:, hp * LANES:(hp + 1) * LANES]
                k_ref[0, 2 * hp] = jnp.where(low, kp, ind_hi).astype(BF16)
                k_ref[0, 2 * hp + 1] = jnp.where(low, ind_lo, kp).astype(BF16)
        elif idx == 2:
            for hp in range(ch // LANES):
                vp = acc[:, hp * LANES:(hp + 1) * LANES]
                v_ref[0, 2 * hp] = jnp.where(low, vp, 1.0).astype(BF16)
                v_ref[0, 2 * hp + 1] = jnp.where(low, 1.0, vp).astype(BF16)
        else:
            pb_ref[0, :, (idx - 3) * ch:(idx - 2) * ch] = acc.astype(BF16)


def _even_proj(x, g, sc, sh, w, tables):
    b, s, d = x.shape
    n = w.shape[1]
    tm = PROJ_TM
    nblk = tm // MOBA_BLOCK
    row = lambda bi, i: (bi, i, 0)
    vec = lambda bi, i: (bi, 0, 0)
    hd = lambda bi, i: (bi, 0, i, 0)
    tab = pl.BlockSpec((tm, LANES), lambda bi, i: (i, 0))
    return pl.pallas_call(
        _even_proj_body,
        grid=(b, s // tm),
        in_specs=[pl.BlockSpec((1, tm, d), row),
                  pl.BlockSpec((1, d), lambda bi, i: (0, 0)),
                  pl.BlockSpec((1, 1, d), vec),
                  pl.BlockSpec((1, 1, d), vec),
                  pl.BlockSpec((d, n), lambda bi, i: (0, 0)),
                  tab, tab, tab],
        out_specs=[pl.BlockSpec((1, tm, MOBA_W), row),
                   pl.BlockSpec((1, MOBA_HEADS, tm, LANES), hd),
                   pl.BlockSpec((1, MOBA_HEADS, tm, LANES), hd),
                   pl.BlockSpec((1, tm, DIFF_W), row),
                   pl.BlockSpec((1, 1, nblk, MOBA_W), lambda bi, i: (bi, i, 0, 0))],
        out_shape=[jax.ShapeDtypeStruct((b, s, MOBA_W), BF16),
                   jax.ShapeDtypeStruct((b, MOBA_HEADS, s, LANES), BF16),
                   jax.ShapeDtypeStruct((b, MOBA_HEADS, s, LANES), BF16),
                   jax.ShapeDtypeStruct((b, s, DIFF_W), BF16),
                   jax.ShapeDtypeStruct((b, s // tm, nblk, MOBA_W), F32)],
        compiler_params=_params("parallel", "parallel"),
        name="even_proj",
    )(x, g, sc, sh, w, *tables)


def _moba_body(qa_ref, qb_ref, k_ref, v_ref, km_ref, oa_ref, ob_ref):
    a = pl.program_id(2)
    tq = qa_ref.shape[1]
    nb = km_ref.shape[1]
    tk = ATT_TK
    nt = k_ref.shape[2] // tq
    km = km_ref[0]
    lane = lax.broadcasted_iota(jnp.int32, (tq, LANES), 1)
    lane_k = lax.broadcasted_iota(jnp.int32, (nb, LANES), 1)
    n_idx = lax.broadcasted_iota(jnp.int32, (nb, tq), 0)
    low = lane < HEAD_DIM

    qs, state = [], []
    for i, q_ref, rows in ((a, qa_ref, nb // 2), (nt - 1 - a, qb_ref, nb)):
        q = q_ref[0]
        c0 = pl.multiple_of((i // 2) * tk, tk)
        qpos = i * tq + lax.broadcasted_iota(jnp.int32, (tq, 1), 0)
        causal = c0 + lax.broadcasted_iota(jnp.int32, (1, tk), 1) <= qpos
        q_t, st_t = [], []
        for e in range(2):
            own = low if e == 0 else (lane >= HEAD_DIM)
            own_k = (lane_k < HEAD_DIM) if e == 0 else (lane_k >= HEAD_DIM)
            qm = jnp.where(own, q, jnp.zeros_like(q))
            km_hi, km_lo = _split_bf16(jnp.where(own_k, km, 0.0))
            gs = _dot_nt(km_hi, qm) + _dot_nt(km_lo, qm)
            gs = jnp.where(n_idx < i, gs, NEG_INF)
            keep = (_rank_below(gs, n_idx, MOBA_TOPK, rows) & (n_idx < i)) | (n_idx == i)
            bias_t = jnp.where(keep, 0.0, NEG_INF)
            off = HEAD_DIM * (1 - e)
            parts = [jnp.zeros((off, tq), F32)] if off else []
            pad = jnp.concatenate(parts + [bias_t, jnp.zeros((LANES - off - nb, tq), F32)], axis=0)
            qe = jnp.where(own, q, jnp.transpose(pad).astype(BF16))
            q_t.append(qe)
            st_t.append(_online_step(qe, k_ref[0, e, pl.ds(c0, tk), :], v_ref[0, e, pl.ds(c0, tk), :],
                                     causal, None, None))
        qs.append(q_t)
        state.append(st_t)

    n_chunks = k_ref.shape[2] // tk
    acc = _attn_pair(n_chunks - 1, (nt // 2 - 1) // 2, a // 2, tk, qs, state,
                     lambda e, k0: k_ref[0, e, pl.ds(k0, tk), :],
                     lambda e, k0: v_ref[0, e, pl.ds(k0, tk), :], late_pv=True)
    for (acc0, acc1), o_ref in zip(acc, (oa_ref, ob_ref)):
        num = jnp.where(low, acc0, acc1)
        den = jnp.where(low, pltpu.roll(acc0, HEAD_DIM, 1), pltpu.roll(acc1, HEAD_DIM, 1))
        o_ref[0] = (num / den).astype(o_ref.dtype)


def _moba_attention(q, kx, vx, kmean):
    b, s, _ = q.shape
    nb = s // MOBA_BLOCK
    pairs = MOBA_W // LANES
    tq = ATT_TQ
    nt = s // tq
    assert tq == MOBA_BLOCK and ATT_TK == 2 * MOBA_BLOCK and nt % 4 == 0
    half = jax.ShapeDtypeStruct((b, s // 2, MOBA_W), BF16)
    return pl.pallas_call(
        _moba_body,
        grid=(b, pairs, nt // 2),
        in_specs=[pl.BlockSpec((1, tq, LANES), lambda bi, hp, a: (bi, a, hp)),
                  pl.BlockSpec((1, tq, LANES), lambda bi, hp, a: (bi, nt - 1 - a, hp)),
                  pl.BlockSpec((1, 2, s, LANES), lambda bi, hp, a: (bi, hp, 0, 0)),
                  pl.BlockSpec((1, 2, s, LANES), lambda bi, hp, a: (bi, hp, 0, 0)),
                  pl.BlockSpec((1, nb, LANES), lambda bi, hp, a: (bi, 0, hp))],
        out_specs=[pl.BlockSpec((1, tq, LANES), lambda bi, hp, a: (bi, a, hp)),
                   pl.BlockSpec((1, tq, LANES), lambda bi, hp, a: (bi, nt // 2 - 1 - a, hp))],
        out_shape=[half, half],
        compiler_params=_params("parallel", "parallel", "arbitrary"),
        name="moba_attn",
    )(q, q, kx, vx, kmean)


def _diff_body(lam_ref, g_ref, qa_ref, qb_ref, k_ref, v_ref, oa_ref, ob_ref, *, lambda_init):
    a = pl.program_id(2)
    tq = qa_ref.shape[1]
    tk = ATT_TK
    nt = k_ref.shape[1] // tq
    lp = lam_ref[...]
    lam = (jnp.exp(jnp.sum(lp[0:1] * lp[1:2], axis=1, keepdims=True))
           - jnp.exp(jnp.sum(lp[2:3] * lp[3:4], axis=1, keepdims=True)) + lambda_init)
    lane = lax.broadcasted_iota(jnp.int32, (tq, LANES), 1)
    ones = jnp.ones((tk, LANES), BF16)

    def key(m, k0):
        return k_ref[0, pl.ds(k0, tk), :]

    def value(m, k0):
        return jnp.concatenate([v_ref[0, pl.ds(k0, tk), :], ones], axis=1)

    qs, state = [], []
    for i, q_ref in ((a, qa_ref), (nt - 1 - a, qb_ref)):
        q = q_ref[0]
        q_t = [jnp.where(lane < HEAD_DIM, q, jnp.zeros_like(q)), jnp.where(lane >= HEAD_DIM, q, jnp.zeros_like(q))]
        c0 = pl.multiple_of((i // 2) * tk, tk)
        qpos = i * tq + lax.broadcasted_iota(jnp.int32, (tq, 1), 0)
        causal = c0 + lax.broadcasted_iota(jnp.int32, (1, tk), 1) <= qpos
        qs.append(q_t)
        state.append([_online_step(q_t[m], key(m, c0), value(m, c0), causal, None, None) for m in range(2)])

    n_chunks = k_ref.shape[1] // tk
    acc = _attn_pair(n_chunks - 1, (nt // 2 - 1) // 2, a // 2, tk, qs, state, key, value, late_pv=False)
    for (a0, a1), o_ref in zip(acc, (oa_ref, ob_ref)):
        o = a0[:, :LANES] / a0[:, LANES:] - lam * (a1[:, :LANES] / a1[:, LANES:])
        y = o * lax.rsqrt(jnp.mean(o * o, axis=-1, keepdims=True) + NORM_EPS)
        o_ref[0] = ((y * g_ref[...]) * (1.0 - lambda_init)).astype(o_ref.dtype)


def _diff_attention(pb, lam_p, subln_g, lambda_init):
    b, s, _ = pb.shape
    tq = ATT_TQ
    nt = s // tq
    assert nt % 4 == 0
    koff = DIFF_QK_W // LANES
    voff = 2 * koff
    half = jax.ShapeDtypeStruct((b, s // 2, DIFF_V_W), BF16)
    return pl.pallas_call(
        functools.partial(_diff_body, lambda_init=lambda_init),
        grid=(b, DIFF_HEADS, nt // 2),
        in_specs=[pl.BlockSpec((4, HEAD_DIM), lambda bi, h, a: (0, 0)),
                  pl.BlockSpec((1, LANES), lambda bi, h, a: (0, 0)),
                  pl.BlockSpec((1, tq, LANES), lambda bi, h, a: (bi, a, h)),
                  pl.BlockSpec((1, tq, LANES), lambda bi, h, a: (bi, nt - 1 - a, h)),
                  pl.BlockSpec((1, s, LANES), lambda bi, h, a: (bi, 0, koff + h)),
                  pl.BlockSpec((1, s, LANES), lambda bi, h, a: (bi, 0, voff + h))],
        out_specs=[pl.BlockSpec((1, tq, LANES), lambda bi, h, a: (bi, a, h)),
                   pl.BlockSpec((1, tq, LANES), lambda bi, h, a: (bi, nt // 2 - 1 - a, h))],
        out_shape=[half, half],
        compiler_params=_params("parallel", "parallel", "arbitrary"),
        name="diff_attn",
    )(lam_p, subln_g.reshape(1, LANES), pb, pb, pb, pb)


def _out_proj_body(*refs, widths):
    x_ref, g_ref = refs[0], refs[1]
    o_refs = refs[2:2 + len(widths)]
    w_ref, out_ref = refs[2 + len(widths)], refs[3 + len(widths)]
    y = None
    r0 = 0
    for o_ref, wd in zip(o_refs, widths):
        t = _dot(o_ref[0], w_ref[r0:r0 + wd, :])
        y = t if y is None else y + t
        r0 += wd
    out_ref[0] = x_ref[0] + g_ref[0] * y


def _out_proj(x, gate, mixes, w):
    b, s, d = x.shape
    tm = PROJ_TM
    widths = tuple(m.shape[2] for m in mixes)
    row = lambda bi, i: (bi, i, 0)
    return pl.pallas_call(
        functools.partial(_out_proj_body, widths=widths),
        grid=(b, s // tm),
        in_specs=[pl.BlockSpec((1, tm, d), row),
                  pl.BlockSpec((1, 1, d), lambda bi, i: (bi, 0, 0))]
                 + [pl.BlockSpec((1, tm, wd), row) for wd in widths]
                 + [pl.BlockSpec(w.shape, lambda bi, i: (0, 0))],
        out_specs=pl.BlockSpec((1, tm, d), row),
        out_shape=jax.ShapeDtypeStruct((b, s, d), F32),
        compiler_params=_params("parallel", "parallel"),
        name="out_proj",
    )(x, gate, *mixes, w)


def _first_argmax_onehot(v, iota):
    mx = jnp.max(v, axis=1, keepdims=True)
    idx = jnp.min(jnp.where(v == mx, iota, float(v.shape[1])), axis=1, keepdims=True)
    return iota == idx, mx


def _moe_body(x_ref, g_ref, sc_ref, sh_ref, gate_ref, wr_hi_ref, wr_lo_ref, br_ref,
              wg_ref, wu_ref, wd_ref, o_ref, a_ref):
    x = x_ref[0]
    h = _norm_mod(x, g_ref[...], sc_ref[0], sh_ref[0])
    h_hi, h_lo = _split_bf16(h)
    wr_hi = wr_hi_ref[...]
    r = (_dot(h_hi, wr_hi) + (_dot(h_lo, wr_hi) + _dot(h_hi, wr_lo_ref[...]))) + br_ref[...]
    tm = x.shape[0]
    gl = r[:, 0:MOE_GROUPS]
    iota = lax.broadcasted_iota(jnp.int32, (tm, MOE_GROUPS), 1).astype(F32)
    g_oh, g_mx = _first_argmax_onehot(gl, iota)
    gw = 1.0 / jnp.sum(jnp.exp(gl - g_mx), axis=1, keepdims=True)
    el_g = jnp.zeros((tm, MOE_PER_GROUP), F32)
    for g in range(MOE_GROUPS):
        lo = MOE_GROUPS + g * MOE_PER_GROUP
        el_g = el_g + jnp.where(g_oh[:, g:g + 1], r[:, lo:lo + MOE_PER_GROUP], 0.0)
    oh1, v1 = _first_argmax_onehot(el_g, iota)
    oh2, v2 = _first_argmax_onehot(jnp.where(oh1, -jnp.inf, el_g), iota)
    e2 = jnp.exp(v2 - v1)
    den = 1.0 + e2
    w_grp = jnp.where(oh1, 1.0 / den, 0.0) + jnp.where(oh2, e2 / den, 0.0)
    gsc = jnp.where(g_oh, gw, 0.0)

    hb = h_hi
    for e in range(MOE_EXPERTS):
        g, k = divmod(e, MOE_PER_GROUP)
        comb = gsc[:, g:g + 1] * w_grp[:, k:k + 1]
        gt = _dot(hb, wg_ref[e])
        up = _dot(hb, wu_ref[e])
        a = ((gt * (1.0 / (1.0 + jnp.exp(-gt)))) * up) * comb
        a_ref[:, e * MOE_FF:(e + 1) * MOE_FF] = a.astype(BF16)
    y = _dot(a_ref[...], wd_ref[...])
    o_ref[0] = x + gate_ref[0] * y


def _moe(x, g, sc, sh, gate, wg, bg, we, be, w_gate, w_up, w_down):
    b, s, d = x.shape
    tm = PROJ_TM
    nr = MOE_GROUPS + MOE_EXPERTS
    wr = jnp.pad(jnp.concatenate([wg, we], axis=1), ((0, 0), (0, LANES - nr)))
    br = jnp.pad(jnp.concatenate([bg, be], axis=0), (0, LANES - nr)).reshape(1, LANES)
    wr_hi, wr_lo = _split_bf16(wr)
    row = lambda bi, i: (bi, i, 0)
    vec = lambda bi, i: (bi, 0, 0)
    const2 = lambda bi, i: (0, 0)
    const3 = lambda bi, i: (0, 0, 0)
    once = pl.Buffered(1)
    return pl.pallas_call(
        _moe_body,
        grid=(b, s // tm),
        in_specs=[pl.BlockSpec((1, tm, d), row),
                  pl.BlockSpec((1, d), const2),
                  pl.BlockSpec((1, 1, d), vec),
                  pl.BlockSpec((1, 1, d), vec),
                  pl.BlockSpec((1, 1, d), vec),
                  pl.BlockSpec((d, LANES), const2),
                  pl.BlockSpec((d, LANES), const2),
                  pl.BlockSpec((1, LANES), const2),
                  pl.BlockSpec((MOE_EXPERTS, d, MOE_FF), const3, pipeline_mode=once),
                  pl.BlockSpec((MOE_EXPERTS, d, MOE_FF), const3, pipeline_mode=once),
                  pl.BlockSpec((MOE_EXPERTS * MOE_FF, d), const2, pipeline_mode=once)],
        out_specs=pl.BlockSpec((1, tm, d), row),
        out_shape=jax.ShapeDtypeStruct((b, s, d), F32),
        scratch_shapes=[pltpu.VMEM((tm, MOE_EXPERTS * MOE_FF), BF16)],
        compiler_params=_params("parallel", "parallel"),
        name="hier_moe",
    )(x, g, sc, sh, gate, wr_hi, wr_lo, br, w_gate.astype(BF16), w_up.astype(BF16),
      w_down.astype(BF16).reshape(MOE_EXPERTS * MOE_FF, d))


ODD_PAD = 2688


def _odd_proj_body(x_ref, g_ref, sc_ref, sh_ref, w_ref, c_ref, s1_ref, s2_ref,
                   q_ref, kvf_ref, kvx_ref, gates_ref):
    tm = x_ref.shape[1]
    h = _norm_mod(x_ref[0], g_ref[...], sc_ref[0], sh_ref[0]).astype(BF16)
    c, s1, s2 = c_ref[...], s1_ref[...], s2_ref[...]
    half = NSA_Q_W // 2
    for idx in range(2):
        acc = _dot(h, w_ref[:, idx * half:(idx + 1) * half])
        q_ref[0, :, idx * half:(idx + 1) * half] = (_rope(acc, c, s1, s2) * QK_SCALE).astype(BF16)
    lane = lax.broadcasted_iota(jnp.int32, (tm, LANES), 1)
    low = lane < HEAD_DIM
    blk = (pl.program_id(1) * tm + lax.broadcasted_iota(jnp.int32, (tm, LANES), 0)) // SLC_BLOCK
    ind = jnp.where(lane == blk + HEAD_DIM, 1.0, 0.0)
    for idx in range(6):
        c0 = NSA_Q_W + idx * NSA_KV_W
        acc = _dot(h, w_ref[:, c0:c0 + NSA_KV_W])
        if idx % 2 == 0:
            acc = _rope(acc, c, s1, s2)
        if idx < 2:
            for g in range(NSA_KV_HEADS):
                kvf_ref[0, idx * NSA_KV_HEADS + g] = acc[:, g * HEAD_DIM:(g + 1) * HEAD_DIM]
            continue
        fill = 1.0 if idx % 2 == 1 else (ind if idx == 2 else 0.0)
        for gp in range(NSA_KV_HEADS // 2):
            pair = acc[:, gp * LANES:(gp + 1) * LANES]
            swapped = pltpu.roll(pair, HEAD_DIM, 1)
            kvx_ref[0, (idx - 2) * NSA_KV_HEADS + 2 * gp] = jnp.where(low, pair, fill).astype(BF16)
            kvx_ref[0, (idx - 2) * NSA_KV_HEADS + 2 * gp + 1] = jnp.where(low, swapped, fill).astype(BF16)
    c0 = NSA_Q_W + 6 * NSA_KV_W
    gl = _dot(h, w_ref[:, c0:c0 + LANES])
    gates_ref[0] = 1.0 / (1.0 + jnp.exp(-gl))


def _odd_proj(x, g, sc, sh, w, tables):
    b, s, d = x.shape
    n = w.shape[1]
    tm = PROJ_TM
    row = lambda bi, i: (bi, i, 0)
    vec = lambda bi, i: (bi, 0, 0)
    tab = pl.BlockSpec((tm, LANES), lambda bi, i: (i, 0))
    hd = lambda bi, i: (bi, 0, i, 0)
    return pl.pallas_call(
        _odd_proj_body,
        grid=(b, s // tm),
        in_specs=[pl.BlockSpec((1, tm, d), row),
                  pl.BlockSpec((1, d), lambda bi, i: (0, 0)),
                  pl.BlockSpec((1, 1, d), vec),
                  pl.BlockSpec((1, 1, d), vec),
                  pl.BlockSpec((d, n), lambda bi, i: (0, 0)),
                  tab, tab, tab],
        out_specs=[pl.BlockSpec((1, tm, NSA_Q_W), row),
                   pl.BlockSpec((1, 2 * NSA_KV_HEADS, tm, HEAD_DIM), hd),
                   pl.BlockSpec((1, 4 * NSA_KV_HEADS, tm, LANES), hd),
                   pl.BlockSpec((1, tm, LANES), row)],
        out_shape=[jax.ShapeDtypeStruct((b, s, NSA_Q_W), BF16),
                   jax.ShapeDtypeStruct((b, 2 * NSA_KV_HEADS, s, HEAD_DIM), F32),
                   jax.ShapeDtypeStruct((b, 4 * NSA_KV_HEADS, s, LANES), BF16),
                   jax.ShapeDtypeStruct((b, s, LANES), F32)],
        compiler_params=_params("parallel", "parallel"),
        name="odd_proj",
    )(x, g, sc, sh, w, *tables)


def _compress_body(x_ref, pos_ref, w1_ref, b1_ref, w2_ref, b2_ref, o_ref):
    x = x_ref[0, 0]
    half = CMP_STRIDE * HEAD_DIM
    xa = (x + pos_ref[0, 0:1]).astype(BF16)
    xb = (x + pos_ref[0, 1:2]).astype(BF16)
    a = _dot(xa, w1_ref[0, 0:half])
    bm = _dot(xb, w1_ref[0, half:2 * half])
    nrow = x.shape[0]
    pre = (a + pltpu.roll(bm, nrow - 1, 0)) + b1_ref[0]
    hid = 0.5 * pre * (1.0 + jnp.tanh(math.sqrt(2.0 / math.pi) * (pre + 0.044715 * (pre * pre * pre))))
    o_ref[0, 0] = (_dot(hid.astype(BF16), w2_ref[0]) + b2_ref[0]).astype(o_ref.dtype)


def _compress(kvf, pos, w1, b1, w2, b2):
    b, n2, s, hd = kvf.shape
    g = n2 // 2
    nchunk = s // CMP_STRIDE
    half = CMP_STRIDE * hd
    x = kvf.reshape(b, n2, nchunk, half)
    kv = lambda bi, n: (n // g, 0, 0)
    return pl.pallas_call(
        _compress_body,
        grid=(b, n2),
        in_specs=[pl.BlockSpec((1, 1, nchunk, half), lambda bi, n: (bi, n, 0, 0)),
                  pl.BlockSpec((1, 2, half), kv),
                  pl.BlockSpec((1, 2 * half, CMP_HIDDEN), kv),
                  pl.BlockSpec((1, 1, CMP_HIDDEN), kv),
                  pl.BlockSpec((1, CMP_HIDDEN, hd), kv),
                  pl.BlockSpec((1, 1, hd), kv)],
        out_specs=pl.BlockSpec((1, 1, nchunk, hd), lambda bi, n: (bi, n, 0, 0)),
        out_shape=jax.ShapeDtypeStruct((b, n2, nchunk, hd), BF16),
        compiler_params=_params("parallel", "parallel"),
        name="nsa_compress",
    )(x, pos.reshape(2, 2, half), w1.astype(BF16), b1.reshape(2, 1, CMP_HIDDEN),
      w2.astype(BF16), b2.reshape(2, 1, hd))


def _nsa_tile(i, first_half, q_ref, gt_ref, kc, vc, ks_ref, vs_ref, kw_ref, vw_ref, ovt):
    tq = q_ref.shape[1]
    r = NSA_GROUP
    q0 = i * tq
    qf = q_ref[0]
    q4 = jnp.concatenate([qf[:, h * HEAD_DIM:(h + 1) * HEAD_DIM] for h in range(r)], axis=0)
    qpos_c = q0 + lax.broadcasted_iota(jnp.int32, (tq, 1), 0)
    qpos4 = jnp.concatenate([qpos_c] * r, axis=0)

    nc = kc.shape[0]
    s_c = _dot_nt(q4, kc)
    cmp_end = lax.broadcasted_iota(jnp.int32, (1, nc), 1) * CMP_STRIDE + (CMP_BLOCK - 1)
    s_c = jnp.where(cmp_end <= qpos4, s_c, NEG_INF)
    e_c = jnp.exp(s_c - jnp.max(s_c, axis=1, keepdims=True))
    p_c = e_c / jnp.sum(e_c, axis=1, keepdims=True)
    p_c = jnp.where(qpos4 >= CMP_BLOCK - 1, p_c, 0.0)
    o_c = _dot(p_c.astype(BF16), vc)

    p_sum = p_c[0:tq]
    for h in range(1, r):
        p_sum = p_sum + p_c[h * tq:(h + 1) * tq]
    ps_hi, ps_lo = _split_bf16(p_sum)
    imp = _dot_nt(ovt, ps_hi) + _dot_nt(ovt, ps_lo)
    ns = imp.shape[0]
    blk = lax.broadcasted_iota(jnp.int32, (ns, tq), 0)
    qpos_r = q0 + lax.broadcasted_iota(jnp.int32, (ns, tq), 1)
    own = qpos_r // SLC_BLOCK
    started = blk * SLC_BLOCK <= qpos_r
    forced = (blk == 0) | (blk == own) | (blk == own - 1)
    imp = jnp.where(started, jnp.where(forced, FORCE_SCORE, imp), NEG_INF)
    bias_t = jnp.where(_rank_below(imp, blk, SLC_TOPN, ns // 2 if first_half else ns), 0.0, NEG_INF)
    parts = [jnp.zeros((HEAD_DIM, tq), F32), bias_t]
    if ns < LANES - HEAD_DIM:
        parts.append(jnp.zeros((LANES - HEAD_DIM - ns, tq), F32))
    bias = jnp.transpose(jnp.concatenate(parts, axis=0)).astype(BF16)
    lane4 = lax.broadcasted_iota(jnp.int32, (r * tq, LANES), 1)
    qz = jnp.concatenate([q4, jnp.zeros_like(q4)], axis=1)
    qs = jnp.where(lane4 < HEAD_DIM, qz, jnp.concatenate([bias] * r, axis=0))

    tk = NSA_TK
    d0 = pl.multiple_of((q0 // tk) * tk, tk)
    causal = d0 + lax.broadcasted_iota(jnp.int32, (1, tk), 1) <= qpos4
    st = _online_step(qs, ks_ref[0, 0, pl.ds(d0, tk), :], vs_ref[0, 0, pl.ds(d0, tk), :], causal, None, None)

    span = WINDOW + tq
    w0 = pl.multiple_of(jnp.maximum(q0 - WINDOW, 0), tq)
    kpos_w = w0 + lax.broadcasted_iota(jnp.int32, (1, tq), 1)
    s_w = _dot_nt(qz, kw_ref[0, 0, pl.ds(w0, span), :])
    cols = [jnp.where(kpos_w > qpos4 - WINDOW, s_w[:, :tq], NEG_INF)]
    if first_half:
        cols = [jnp.where(kpos_w <= qpos4, cols[0], NEG_INF)]
        cols += [jnp.where(kpos_w + c * tq <= qpos4, s_w[:, c * tq:(c + 1) * tq], NEG_INF)
                 for c in range(1, span // tq)]
    else:
        cols += [s_w[:, tq:span - tq],
                 jnp.where(kpos_w + (span - tq) <= qpos4, s_w[:, span - tq:], NEG_INF)]
    s_w = jnp.concatenate(cols, axis=1)
    p_w = jnp.exp(s_w - jnp.max(s_w, axis=1, keepdims=True)).astype(BF16)
    acc_w = _dot(p_w, vw_ref[0, 0, pl.ds(w0, span), :])
    o_w = acc_w[:, :HEAD_DIM] / acc_w[:, HEAD_DIM:]
    gt = gt_ref[0, 0]

    def finish(acc_s):
        o_s = acc_s[:, :HEAD_DIM] / acc_s[:, HEAD_DIM:]
        outs = []
        for h in range(r):
            sl = slice(h * tq, (h + 1) * tq)
            outs.append(gt[:, 3 * h:3 * h + 1] * o_c[sl] + gt[:, 3 * h + 1:3 * h + 2] * o_s[sl]
                        + gt[:, 3 * h + 2:3 * h + 3] * o_w[sl])
        return jnp.concatenate(outs, axis=1)

    return qs, st, finish


def _nsa_body(qa_ref, qb_ref, kc_ref, vc_ref, ks_ref, vs_ref, kw_ref, vw_ref, ga_ref, gb_ref, ovt_ref,
              oa_ref, ob_ref):
    a = pl.program_id(2)
    tq = qa_ref.shape[1]
    tk = NSA_TK
    nt = ks_ref.shape[2] // tq
    kc, vc, ovt = kc_ref[0, 0], vc_ref[0, 0], ovt_ref[...]
    tiles = [_nsa_tile(i, first, q_ref, g_ref, kc, vc, ks_ref, vs_ref, kw_ref, vw_ref, ovt)
             for i, first, q_ref, g_ref in ((a, True, qa_ref, ga_ref), (nt - 1 - a, False, qb_ref, gb_ref))]
    per = tk // tq
    n_chunks = ks_ref.shape[2] // tk
    acc = _attn_pair(n_chunks - 1, (nt // 2 - 1) // per, a // per, tk,
                     [[t[0]] for t in tiles], [[t[1]] for t in tiles],
                     lambda e, k0: ks_ref[0, 0, pl.ds(k0, tk), :],
                     lambda e, k0: vs_ref[0, 0, pl.ds(k0, tk), :], late_pv=False)
    oa_ref[0] = tiles[0][2](acc[0][0]).astype(oa_ref.dtype)
    ob_ref[0] = tiles[1][2](acc[1][0]).astype(ob_ref.dtype)


def _nsa_attention(q, cmp, kvx, gates):
    b, s, _ = q.shape
    g = NSA_KV_HEADS
    tq = NSA_TQ
    nt = s // tq
    nc = cmp.shape[2]
    ns = s // SLC_BLOCK
    assert ns <= LANES - HEAD_DIM
    assert nt % (2 * NSA_TK // tq) == 0 and s // 2 >= WINDOW
    cw = NSA_GROUP * HEAD_DIM
    cs = jnp.arange(nc)[None, :] * CMP_STRIDE
    ss = jnp.arange(ns)[:, None] * SLC_BLOCK
    ovt = ((cs <= ss + SLC_BLOCK - 1) & (cs + CMP_BLOCK - 1 >= ss)).astype(BF16)
    head = lambda off: (lambda bi, gi, a: (bi, off + gi, 0, 0))
    half = jax.ShapeDtypeStruct((b, s // 2, NSA_Q_W), BF16)
    return pl.pallas_call(
        _nsa_body,
        grid=(b, g, nt // 2),
        in_specs=[pl.BlockSpec((1, tq, cw), lambda bi, gi, a: (bi, a, gi)),
                  pl.BlockSpec((1, tq, cw), lambda bi, gi, a: (bi, nt - 1 - a, gi)),
                  pl.BlockSpec((1, 1, nc, HEAD_DIM), head(0)),
                  pl.BlockSpec((1, 1, nc, HEAD_DIM), head(g)),
                  pl.BlockSpec((1, 1, s, LANES), head(0)),
                  pl.BlockSpec((1, 1, s, LANES), head(g)),
                  pl.BlockSpec((1, 1, s, LANES), head(2 * g)),
                  pl.BlockSpec((1, 1, s, LANES), head(3 * g)),
                  pl.BlockSpec((1, 1, tq, 3 * NSA_GROUP), lambda bi, gi, a: (bi, gi, a, 0)),
                  pl.BlockSpec((1, 1, tq, 3 * NSA_GROUP), lambda bi, gi, a: (bi, gi, nt - 1 - a, 0)),
                  pl.BlockSpec((ns, nc), lambda bi, gi, a: (0, 0))],
        out_specs=[pl.BlockSpec((1, tq, cw), lambda bi, gi, a: (bi, a, gi)),
                   pl.BlockSpec((1, tq, cw), lambda bi, gi, a: (bi, nt // 2 - 1 - a, gi))],
        out_shape=[half, half],
        compiler_params=_params("parallel", "parallel", "arbitrary"),
        name="nsa_attn",
    )(q, q, cmp, cmp, kvx, kvx, kvx, kvx, gates, gates, ovt)


def _final_norm_body(x_ref, g_ref, o_ref):
    x = x_ref[0]
    o_ref[0] = (x * lax.rsqrt(jnp.mean(x * x, axis=-1, keepdims=True) + NORM_EPS)) * g_ref[...]


def _final_norm(x, g):
    b, s, d = x.shape
    tm = PROJ_TM
    row = lambda bi, i: (bi, i, 0)
    return pl.pallas_call(
        _final_norm_body,
        grid=(b, s // tm),
        in_specs=[pl.BlockSpec((1, tm, d), row), pl.BlockSpec((1, d), lambda bi, i: (0, 0))],
        out_specs=pl.BlockSpec((1, tm, d), row),
        out_shape=jax.ShapeDtypeStruct((b, s, d), F32),
        compiler_params=_params("parallel", "parallel"),
        name="final_norm",
    )(x, g.reshape(1, d))


def kernel(x, c, norm1_g, norm2_g, final_g, ada_w, ada_b, ev_w_in, ev_w_out, ev_lambda, ev_subln_g,
           od_w_in, od_w_out, od_cmp_pos, od_cmp_w1, od_cmp_b1, od_cmp_w2, od_cmp_b2,
           moe_wg, moe_bg, moe_we, moe_be, moe_w_gate, moe_w_up, moe_w_down):
    b, s, d = x.shape
    tables = _rope_tables(s)
    mod = _ada_mod(c, ada_w, ada_b)
    for l in range(DEPTH):
        sh1, sc1, g1, sh2, sc2, g2 = (mod[l, :, None, k * d:(k + 1) * d] for k in range(6))
        i = l // 2
        if l % 2 == 0:
            lambda_init = 0.8 - 0.6 * math.exp(-0.3 * l)
            qa, kx, vx, pb, kmean = _even_proj(x, norm1_g[l].reshape(1, d), sc1, sh1,
                                               ev_w_in[i].astype(BF16), tables)
            oa = _moba_attention(qa, kx, vx, kmean.reshape(b, s // MOBA_BLOCK, MOBA_W))
            ob = _diff_attention(pb, ev_lambda[i], ev_subln_g[i], lambda_init)
            mixes = (jnp.concatenate(oa, axis=1), jnp.concatenate(ob, axis=1))
            x = _out_proj(x, g1, mixes, ev_w_out[i].astype(BF16))
        else:
            w = jnp.pad(od_w_in[i], ((0, 0), (0, ODD_PAD - ODD_IN))).astype(BF16)
            q, kvf, kvx, gates = _odd_proj(x, norm1_g[l].reshape(1, d), sc1, sh1, w, tables)
            cmp = _compress(kvf, od_cmp_pos[i], od_cmp_w1[i], od_cmp_b1[i], od_cmp_w2[i], od_cmp_b2[i])
            gt = gates[:, :, :N_GATES].reshape(b, s, NSA_KV_HEADS, 3 * NSA_GROUP).transpose(0, 2, 1, 3)
            o = jnp.concatenate(_nsa_attention(q, cmp, kvx, gt), axis=1)
            x = _out_proj(x, g1, (o,), od_w_out[i].astype(BF16))
        x = _moe(x, norm2_g[l].reshape(1, d), sc2, sh2, g2, moe_wg[l], moe_bg[l], moe_we[l], moe_be[l],
                 moe_w_gate[l], moe_w_up[l], moe_w_down[l])
    return _final_norm(x, final_g)
```

```python
import functools
import math

import jax
import jax.numpy as jnp
from jax import lax
from jax.experimental import pallas as pl
from jax.experimental.pallas import tpu as pltpu

F32 = jnp.float32
BF16 = jnp.bfloat16

D_MODEL = 1024
DEPTH = 4
HEAD_DIM = 64
ROPE_DIM = HEAD_DIM // 4
ROPE_HALF = ROPE_DIM // 2
ROPE_THETA = 500000.0
NORM_EPS = 1e-6
NEG_INF = -1e30
FORCE_SCORE = 1e6
QK_SCALE = HEAD_DIM ** -0.5 * math.log2(math.e)

MOBA_HEADS = 8
MOBA_BLOCK = 256
MOBA_TOPK = 3
DIFF_HEADS = 4
MOBA_W = MOBA_HEADS * HEAD_DIM
DIFF_QK_W = DIFF_HEADS * 2 * HEAD_DIM
DIFF_V_W = DIFF_HEADS * 2 * HEAD_DIM
DIFF_W = 2 * DIFF_QK_W + DIFF_V_W
EVEN_IN = 3 * MOBA_W + DIFF_W

NSA_HEADS = 16
NSA_GROUP = 4
NSA_KV_HEADS = 4
CMP_BLOCK = 32
CMP_STRIDE = 16
CMP_HIDDEN = 256
SLC_BLOCK = 64
SLC_TOPN = 16
WINDOW = 512
NSA_Q_W = NSA_HEADS * HEAD_DIM
NSA_KV_W = NSA_KV_HEADS * HEAD_DIM
ODD_IN = NSA_Q_W + 6 * NSA_KV_W + 3 * NSA_HEADS
N_GATES = 3 * NSA_HEADS

MOE_GROUPS = 4
MOE_PER_GROUP = 4
MOE_EXPERTS = 16
MOE_FF = 256

LANES = 128
VMEM_LIMIT = 56 * 1024 * 1024

PROJ_TM = 512
ATT_TQ = 256
ATT_TK = 512
NSA_TQ = 128
NSA_TK = 512


def _params(*sem):
    return pltpu.CompilerParams(dimension_semantics=sem, vmem_limit_bytes=VMEM_LIMIT)


def _dot(a, b):
    return jnp.dot(a, b, preferred_element_type=F32)


def _dot_nt(a, b):
    return lax.dot_general(a, b, (((1,), (1,)), ((), ())), preferred_element_type=F32)


def _split_bf16(x):
    hi = x.astype(BF16)
    lo = (x - hi.astype(F32)).astype(BF16)
    return hi, lo


def _norm_mod(x, g, sc, sh):
    y = x * lax.rsqrt(jnp.mean(x * x, axis=-1, keepdims=True) + NORM_EPS)
    return (y * g) * (1.0 + sc) + sh


def _rope(t, c, s1, s2):
    w = t.shape[1]
    k = w // LANES
    cw = jnp.concatenate([c] * k, axis=1) if k > 1 else c
    s1w = jnp.concatenate([s1] * k, axis=1) if k > 1 else s1
    s2w = jnp.concatenate([s2] * k, axis=1) if k > 1 else s2
    return t * cw + pltpu.roll(t, ROPE_HALF, 1) * s1w + pltpu.roll(t, w - ROPE_HALF, 1) * s2w


def _rope_tables(seq):
    pos = jnp.arange(seq, dtype=F32)
    inv = ROPE_THETA ** (-jnp.arange(0, ROPE_DIM, 2, dtype=F32) / ROPE_DIM)
    ang = pos[:, None] * inv[None, :]
    cos, sin = jnp.cos(ang), jnp.sin(ang)
    ones = jnp.ones((seq, HEAD_DIM - ROPE_DIM), F32)
    zeros8 = jnp.zeros((seq, ROPE_HALF), F32)
    zeros = jnp.zeros((seq, HEAD_DIM - ROPE_DIM), F32)
    c = jnp.concatenate([cos, cos, ones], axis=1)
    s1 = jnp.concatenate([zeros8, sin, zeros], axis=1)
    s2 = jnp.concatenate([-sin, zeros8, zeros], axis=1)
    rep = LANES // HEAD_DIM
    return tuple(jnp.tile(t, (1, rep)) for t in (c, s1, s2))


def _rank_below(v, k, rows):
    n = v.shape[0]
    sub = 8
    groups = [v[g:g + sub] for g in range(0, n, sub)]
    cnts = [jnp.zeros(g.shape, F32) for g in groups]
    idx = lax.broadcasted_iota(jnp.int32, groups[0].shape, 0)
    for m in range(rows):
        rm = v[m:m + 1, :]
        for j, g in enumerate(groups):
            if j * sub > m:
                beat = rm >= g
            elif j * sub + sub - 1 < m:
                beat = rm > g
            else:
                beat = (rm > g) | ((rm == g) & (idx > m - j * sub))
            cnts[j] = cnts[j] + jnp.where(beat, 1.0, 0.0)
    return jnp.concatenate(cnts, axis=0) < k


def _online_step(q, k, v, mask, m, acc):
    s = _dot_nt(q, k)
    if mask is not None:
        s = jnp.where(mask, s, NEG_INF)
    m_new = jnp.max(s, axis=1, keepdims=True)
    if m is not None:
        m_new = jnp.maximum(m, m_new)
    p = jnp.exp2(s - m_new).astype(v.dtype)
    pv = _dot(p, v)
    if m is None:
        return m_new, pv
    return m_new, jnp.exp2(m - m_new) * acc + pv


class _Staged:
    def __init__(self, gen):
        self.gen, self.done, self.value = gen, False, None

    def advance(self):
        if not self.done:
            try:
                next(self.gen)
            except StopIteration as stop:
                self.done, self.value = True, stop.value

    def result(self):
        while not self.done:
            self.advance()
        return self.value


def _attn_pair(n_past, max_short, cnt_short, tk, long_tile, short_tile, key, value, late_pv):
    qs_l, st_l = long_tile[:2]
    ns = len(qs_l)
    n_static = n_past - max_short
    cnt_long = n_past - cnt_short
    q_cur = list(qs_l)
    m_cur = [st_l[e][0] for e in range(ns)]
    acc_cur = [st_l[e][1] for e in range(ns)]
    out_long = list(acc_cur)
    pend = None

    def flush():
        return [acc_cur[e] + _dot(pend[0][e], value(e, pend[1])) for e in range(ns)]

    for u in range(n_past):
        if pend is not None:
            acc_cur = flush()
        if u >= n_static:
            qs_s, st_s = short_tile.result()[:2]
            sw = u == cnt_long
            out_long = [jnp.where(sw, acc_cur[e], out_long[e]) for e in range(ns)]
            acc_cur = [jnp.where(sw, st_s[e][1], acc_cur[e]) for e in range(ns)]
            m_cur = [jnp.where(sw, st_s[e][0], m_cur[e]) for e in range(ns)]
            q_cur = [jnp.where(sw, qs_s[e], q_cur[e]) for e in range(ns)]
            chunk = jnp.where(u < cnt_long, u, u - cnt_long)
        else:
            chunk = u
        k0 = pl.multiple_of(chunk * tk, tk) if u >= n_static else chunk * tk
        if late_pv:
            ss = [_dot_nt(q_cur[e], key(e, k0)) for e in range(ns)]
            ps = []
            for e in range(ns):
                m_new = jnp.maximum(m_cur[e], jnp.max(ss[e], axis=1, keepdims=True))
                ps.append(jnp.exp2(ss[e] - m_new).astype(BF16))
                acc_cur[e] = jnp.exp2(m_cur[e] - m_new) * acc_cur[e]
                m_cur[e] = m_new
            pend = (ps, k0)
        else:
            for e in range(ns):
                m_cur[e], acc_cur[e] = _online_step(q_cur[e], key(e, k0), value(e, k0), None,
                                                    m_cur[e], acc_cur[e])
        if u < n_static:
            short_tile.advance()
    if pend is not None:
        acc_cur = flush()
    st_s = short_tile.result()[1]
    none_short = cnt_short == 0
    out_long = [jnp.where(none_short, acc_cur[e], out_long[e]) for e in range(ns)]
    out_short = [jnp.where(none_short, st_s[e][1], acc_cur[e]) for e in range(ns)]
    return out_long, out_short


def _ada_body(c_ref, w_ref, b_ref, o_ref):
    c = c_ref[...]
    cs = c * (1.0 / (1.0 + jnp.exp(-c)))
    o_ref[0] = jnp.dot(cs, w_ref[0], preferred_element_type=F32,
                       precision=lax.Precision.HIGHEST) + b_ref[0]


def _ada_mod(c, ada_w, ada_b):
    b, d = c.shape
    depth, _, n = ada_w.shape
    rows = 8
    tn = 1536
    cp = jnp.pad(c, ((0, rows - b), (0, 0)))
    out = pl.pallas_call(
        _ada_body,
        grid=(depth, n // tn),
        in_specs=[pl.BlockSpec((rows, d), lambda l, j: (0, 0)),
                  pl.BlockSpec((1, d, tn), lambda l, j: (l, 0, j)),
                  pl.BlockSpec((1, 1, tn), lambda l, j: (l, 0, j))],
        out_specs=pl.BlockSpec((1, rows, tn), lambda l, j: (l, 0, j)),
        out_shape=jax.ShapeDtypeStruct((depth, rows, n), F32),
        compiler_params=_params("parallel", "parallel"),
        name="ada_mod",
    )(cp, ada_w, ada_b.reshape(depth, 1, n))
    return out[:, :b]


def _even_proj_body(x_ref, g_ref, sc_ref, sh_ref, w_ref, c_ref, s1_ref, s2_ref,
                    q_ref, k_ref, v_ref, pb_ref, km_ref):
    tm = x_ref.shape[1]
    h = _norm_mod(x_ref[0], g_ref[...], sc_ref[0], sh_ref[0]).astype(BF16)
    c, s1, s2 = c_ref[...], s1_ref[...], s2_ref[...]
    ch = MOBA_W
    lane = lax.broadcasted_iota(jnp.int32, (tm, LANES), 1)
    low = lane < HEAD_DIM
    blk = (pl.program_id(1) * tm + lax.broadcasted_iota(jnp.int32, (tm, LANES), 0)) // MOBA_BLOCK
    ind_hi = jnp.where(lane == blk + HEAD_DIM, 1.0, 0.0)
    ind_lo = jnp.where(lane == blk, 1.0, 0.0)
    for idx, kind in enumerate(("q", "k", "v", "q", "k", "v")):
        acc = _dot(h, w_ref[:, idx * ch:(idx + 1) * ch])
        if kind != "v":
            acc = _rope(acc, c, s1, s2)
        if kind == "q":
            acc = acc * QK_SCALE
        if idx == 0:
            q_ref[0] = acc.astype(BF16)
        elif idx == 1:
            nblk = tm // MOBA_BLOCK
            km_ref[0, 0] = jnp.concatenate(
                [jnp.mean(acc[n * MOBA_BLOCK:(n + 1) * MOBA_BLOCK], axis=0, keepdims=True)
                 for n in range(nblk)], axis=0)
            for hp in range(ch // LANES):
                kp = acc[:, hp * LANES:(hp + 1) * LANES]
                k_ref[0, 2 * hp] = jnp.where(low, kp, ind_hi).astype(BF16)
                k_ref[0, 2 * hp + 1] = jnp.where(low, ind_lo, kp).astype(BF16)
        elif idx == 2:
            for hp in range(ch // LANES):
                vp = acc[:, hp * LANES:(hp + 1) * LANES]
                v_ref[0, 2 * hp] = jnp.where(low, vp, 1.0).astype(BF16)
                v_ref[0, 2 * hp + 1] = jnp.where(low, 1.0, vp).astype(BF16)
        else:
            pb_ref[0, :, (idx - 3) * ch:(idx - 2) * ch] = acc.astype(BF16)


def _even_proj(x, g, sc, sh, w, tables):
    b, s, d = x.shape
    n = w.shape[1]
    tm = PROJ_TM
    nblk = tm // MOBA_BLOCK
    row = lambda bi, i: (bi, i, 0)
    vec = lambda bi, i: (bi, 0, 0)
    hd = lambda bi, i: (bi, 0, i, 0)
    tab = pl.BlockSpec((tm, LANES), lambda bi, i: (i, 0))
    return pl.pallas_call(
        _even_proj_body,
        grid=(b, s // tm),
        in_specs=[pl.BlockSpec((1, tm, d), row),
                  pl.BlockSpec((1, d), lambda bi, i: (0, 0)),
                  pl.BlockSpec((1, 1, d), vec),
                  pl.BlockSpec((1, 1, d), vec),
                  pl.BlockSpec((d, n), lambda bi, i: (0, 0)),
                  tab, tab, tab],
        out_specs=[pl.BlockSpec((1, tm, MOBA_W), row),
                   pl.BlockSpec((1, MOBA_HEADS, tm, LANES), hd),
                   pl.BlockSpec((1, MOBA_HEADS, tm, LANES), hd),
                   pl.BlockSpec((1, tm, DIFF_W), row),
                   pl.BlockSpec((1, 1, nblk, MOBA_W), lambda bi, i: (bi, i, 0, 0))],
        out_shape=[jax.ShapeDtypeStruct((b, s, MOBA_W), BF16),
                   jax.ShapeDtypeStruct((b, MOBA_HEADS, s, LANES), BF16),
                   jax.ShapeDtypeStruct((b, MOBA_HEADS, s, LANES), BF16),
                   jax.ShapeDtypeStruct((b, s, DIFF_W), BF16),
                   jax.ShapeDtypeStruct((b, s // tm, nblk, MOBA_W), F32)],
        compiler_params=_params("parallel", "parallel"),
        name="even_proj",
    )(x, g, sc, sh, w, *tables)


def _moba_body(qa_ref, qb_ref, k_ref, v_ref, km_ref, oa_ref, ob_ref):
    a = pl.program_id(2)
    tq = qa_ref.shape[1]
    nb = km_ref.shape[1]
    tk = ATT_TK
    nt = k_ref.shape[2] // tq
    km = km_ref[0]
    lane = lax.broadcasted_iota(jnp.int32, (tq, LANES), 1)
    lane_k = lax.broadcasted_iota(jnp.int32, (nb, LANES), 1)
    n_idx = lax.broadcasted_iota(jnp.int32, (nb, tq), 0)
    low = lane < HEAD_DIM

    def tile(i, q_ref, rows):
        q = q_ref[0]
        c0 = pl.multiple_of((i // 2) * tk, tk)
        qpos = i * tq + lax.broadcasted_iota(jnp.int32, (tq, 1), 0)
        causal = c0 + lax.broadcasted_iota(jnp.int32, (1, tk), 1) <= qpos
        q_t, st_t = [], []
        for e in range(2):
            own = low if e == 0 else (lane >= HEAD_DIM)
            own_k = (lane_k < HEAD_DIM) if e == 0 else (lane_k >= HEAD_DIM)
            qm = jnp.where(own, q, jnp.zeros_like(q))
            km_hi, km_lo = _split_bf16(jnp.where(own_k, km, 0.0))
            gs = _dot_nt(km_hi, qm) + _dot_nt(km_lo, qm)
            gs = jnp.where(n_idx < i, gs, NEG_INF)
            keep = (_rank_below(gs, MOBA_TOPK, rows) & (n_idx < i)) | (n_idx == i)
            bias_t = jnp.where(keep, 0.0, NEG_INF)
            off = HEAD_DIM * (1 - e)
            parts = [jnp.zeros((off, tq), F32)] if off else []
            pad = jnp.concatenate(parts + [bias_t, jnp.zeros((LANES - off - nb, tq), F32)], axis=0)
            q_t.append(jnp.where(own, q, jnp.transpose(pad).astype(BF16)))
            yield
        for e in range(2):
            st_t.append(_online_step(q_t[e], k_ref[0, e, pl.ds(c0, tk), :], v_ref[0, e, pl.ds(c0, tk), :],
                                     causal, None, None))
            yield
        return q_t, st_t

    long_tile = _Staged(tile(nt - 1 - a, qb_ref, nb)).result()
    short_tile = _Staged(tile(a, qa_ref, nb // 2))
    n_chunks = k_ref.shape[2] // tk
    acc_b, acc_a = _attn_pair(n_chunks - 1, (nt // 2 - 1) // 2, a // 2, tk, long_tile, short_tile,
                              lambda e, k0: k_ref[0, e, pl.ds(k0, tk), :],
                              lambda e, k0: v_ref[0, e, pl.ds(k0, tk), :], late_pv=True)
    for (acc0, acc1), o_ref in ((acc_a, oa_ref), (acc_b, ob_ref)):
        num = jnp.where(low, acc0, acc1)
        den = jnp.where(low, pltpu.roll(acc0, HEAD_DIM, 1), pltpu.roll(acc1, HEAD_DIM, 1))
        o_ref[0] = (num / den).astype(o_ref.dtype)


def _moba_attention(q, kx, vx, kmean):
    b, s, _ = q.shape
    nb = s // MOBA_BLOCK
    pairs = MOBA_W // LANES
    tq = ATT_TQ
    nt = s // tq
    assert tq == MOBA_BLOCK and ATT_TK == 2 * MOBA_BLOCK and nt % 4 == 0
    half = jax.ShapeDtypeStruct((b, s // 2, MOBA_W), BF16)
    return pl.pallas_call(
        _moba_body,
        grid=(b, pairs, nt // 2),
        in_specs=[pl.BlockSpec((1, tq, LANES), lambda bi, hp, a: (bi, a, hp)),
                  pl.BlockSpec((1, tq, LANES), lambda bi, hp, a: (bi, nt - 1 - a, hp)),
                  pl.BlockSpec((1, 2, s, LANES), lambda bi, hp, a: (bi, hp, 0, 0)),
                  pl.BlockSpec((1, 2, s, LANES), lambda bi, hp, a: (bi, hp, 0, 0)),
                  pl.BlockSpec((1, nb, LANES), lambda bi, hp, a: (bi, 0, hp))],
        out_specs=[pl.BlockSpec((1, tq, LANES), lambda bi, hp, a: (bi, a, hp)),
                   pl.BlockSpec((1, tq, LANES), lambda bi, hp, a: (bi, nt // 2 - 1 - a, hp))],
        out_shape=[half, half],
        compiler_params=_params("parallel", "parallel", "arbitrary"),
        name="moba_attn",
    )(q, q, kx, vx, kmean)


def _diff_body(lam_ref, g_ref, qa_ref, qb_ref, k_ref, v_ref, oa_ref, ob_ref, *, lambda_init):
    a = pl.program_id(2)
    tq = qa_ref.shape[1]
    tk = ATT_TK
    nt = k_ref.shape[1] // tq
    lp = lam_ref[...]
    lam = (jnp.exp(jnp.sum(lp[0:1] * lp[1:2], axis=1, keepdims=True))
           - jnp.exp(jnp.sum(lp[2:3] * lp[3:4], axis=1, keepdims=True)) + lambda_init)
    lane = lax.broadcasted_iota(jnp.int32, (tq, LANES), 1)
    ones = jnp.ones((tk, LANES), BF16)

    def key(m, k0):
        return k_ref[0, pl.ds(k0, tk), :]

    def value(m, k0):
        return jnp.concatenate([v_ref[0, pl.ds(k0, tk), :], ones], axis=1)

    def tile(i, q_ref):
        q = q_ref[0]
        q_t = [jnp.where(lane < HEAD_DIM, q, jnp.zeros_like(q)), jnp.where(lane >= HEAD_DIM, q, jnp.zeros_like(q))]
        c0 = pl.multiple_of((i // 2) * tk, tk)
        qpos = i * tq + lax.broadcasted_iota(jnp.int32, (tq, 1), 0)
        causal = c0 + lax.broadcasted_iota(jnp.int32, (1, tk), 1) <= qpos
        st_t = []
        for m in range(2):
            st_t.append(_online_step(q_t[m], key(m, c0), value(m, c0), causal, None, None))
            yield
        return q_t, st_t

    long_tile = _Staged(tile(nt - 1 - a, qb_ref)).result()
    short_tile = _Staged(tile(a, qa_ref))
    n_chunks = k_ref.shape[1] // tk
    acc_b, acc_a = _attn_pair(n_chunks - 1, (nt // 2 - 1) // 2, a // 2, tk, long_tile, short_tile,
                              key, value, late_pv=False)
    for (a0, a1), o_ref in ((acc_a, oa_ref), (acc_b, ob_ref)):
        o = a0[:, :LANES] / a0[:, LANES:] - lam * (a1[:, :LANES] / a1[:, LANES:])
        y = o * lax.rsqrt(jnp.mean(o * o, axis=-1, keepdims=True) + NORM_EPS)
        o_ref[0] = ((y * g_ref[...]) * (1.0 - lambda_init)).astype(o_ref.dtype)


def _diff_attention(pb, lam_p, subln_g, lambda_init):
    b, s, _ = pb.shape
    tq = ATT_TQ
    nt = s // tq
    assert nt % 4 == 0
    koff = DIFF_QK_W // LANES
    voff = 2 * koff
    half = jax.ShapeDtypeStruct((b, s // 2, DIFF_V_W), BF16)
    return pl.pallas_call(
        functools.partial(_diff_body, lambda_init=lambda_init),
        grid=(b, DIFF_HEADS, nt // 2),
        in_specs=[pl.BlockSpec((4, HEAD_DIM), lambda bi, h, a: (0, 0)),
                  pl.BlockSpec((1, LANES), lambda bi, h, a: (0, 0)),
                  pl.BlockSpec((1, tq, LANES), lambda bi, h, a: (bi, a, h)),
                  pl.BlockSpec((1, tq, LANES), lambda bi, h, a: (bi, nt - 1 - a, h)),
                  pl.BlockSpec((1, s, LANES), lambda bi, h, a: (bi, 0, koff + h)),
                  pl.BlockSpec((1, s, LANES), lambda bi, h, a: (bi, 0, voff + h))],
        out_specs=[pl.BlockSpec((1, tq, LANES), lambda bi, h, a: (bi, a, h)),
                   pl.BlockSpec((1, tq, LANES), lambda bi, h, a: (bi, nt // 2 - 1 - a, h))],
        out_shape=[half, half],
        compiler_params=_params("parallel", "parallel", "arbitrary"),
        name="diff_attn",
    )(lam_p, subln_g.reshape(1, LANES), pb, pb, pb, pb)


def _first_argmax_onehot(v, iota):
    mx = jnp.max(v, axis=1, keepdims=True)
    idx = jnp.min(jnp.where(v == mx, iota, float(v.shape[1])), axis=1, keepdims=True)
    return iota == idx, mx


def _moe_body(*refs, widths, final):
    nm = len(widths)
    x_ref, g1_ref = refs[0], refs[1]
    mix_refs = refs[2:2 + nm]
    (wo_ref, g_ref, sc_ref, sh_ref, gate_ref, wr_hi_ref, wr_lo_ref, br_ref,
     wg_ref, wu_ref, wd_ref) = refs[2 + nm:13 + nm]
    fg_ref = refs[13 + nm] if final else None
    o_ref, a_ref = refs[-2], refs[-1]
    y = None
    r0 = 0
    for m_ref, wd in zip(mix_refs, widths):
        t = _dot(m_ref[0], wo_ref[r0:r0 + wd, :])
        y = t if y is None else y + t
        r0 += wd
    x = x_ref[0] + g1_ref[0] * y
    h = _norm_mod(x, g_ref[...], sc_ref[0], sh_ref[0])
    h_hi, h_lo = _split_bf16(h)
    wr_hi = wr_hi_ref[...]
    r = (_dot(h_hi, wr_hi) + (_dot(h_lo, wr_hi) + _dot(h_hi, wr_lo_ref[...]))) + br_ref[...]
    tm = x.shape[0]
    gl = r[:, 0:MOE_GROUPS]
    iota = lax.broadcasted_iota(jnp.int32, (tm, MOE_GROUPS), 1).astype(F32)
    g_oh, g_mx = _first_argmax_onehot(gl, iota)
    gw = 1.0 / jnp.sum(jnp.exp(gl - g_mx), axis=1, keepdims=True)
    el_g = jnp.zeros((tm, MOE_PER_GROUP), F32)
    for g in range(MOE_GROUPS):
        lo = MOE_GROUPS + g * MOE_PER_GROUP
        el_g = el_g + jnp.where(g_oh[:, g:g + 1], r[:, lo:lo + MOE_PER_GROUP], 0.0)
    oh1, v1 = _first_argmax_onehot(el_g, iota)
    oh2, v2 = _first_argmax_onehot(jnp.where(oh1, -jnp.inf, el_g), iota)
    e2 = jnp.exp(v2 - v1)
    den = 1.0 + e2
    w_grp = jnp.where(oh1, 1.0 / den, 0.0) + jnp.where(oh2, e2 / den, 0.0)
    gsc = jnp.where(g_oh, gw, 0.0)

    hb = h_hi
    for e in range(MOE_EXPERTS):
        g, k = divmod(e, MOE_PER_GROUP)
        comb = gsc[:, g:g + 1] * w_grp[:, k:k + 1]
        gt = _dot(hb, wg_ref[e])
        up = _dot(hb, wu_ref[e])
        a = ((gt * (1.0 / (1.0 + jnp.exp(-gt)))) * up) * comb
        a_ref[:, e * MOE_FF:(e + 1) * MOE_FF] = a.astype(BF16)
    out = x + gate_ref[0] * _dot(a_ref[...], wd_ref[...])
    if final:
        out = (out * lax.rsqrt(jnp.mean(out * out, axis=-1, keepdims=True) + NORM_EPS)) * fg_ref[...]
    o_ref[0] = out


def _mix_moe(x, g1, mixes, w_out, g, sc, sh, gate, wg, bg, we, be, w_gate, w_up, w_down, final_g):
    b, s, d = x.shape
    tm = PROJ_TM
    nr = MOE_GROUPS + MOE_EXPERTS
    wr = jnp.pad(jnp.concatenate([wg, we], axis=1), ((0, 0), (0, LANES - nr)))
    br = jnp.pad(jnp.concatenate([bg, be], axis=0), (0, LANES - nr)).reshape(1, LANES)
    wr_hi, wr_lo = _split_bf16(wr)
    widths = tuple(m.shape[2] for m in mixes)
    final = final_g is not None
    row = lambda bi, i: (bi, i, 0)
    vec = lambda bi, i: (bi, 0, 0)
    const2 = lambda bi, i: (0, 0)
    const3 = lambda bi, i: (0, 0, 0)
    once = pl.Buffered(1)
    in_specs = ([pl.BlockSpec((1, tm, d), row), pl.BlockSpec((1, 1, d), vec)]
                + [pl.BlockSpec((1, tm, wd), row) for wd in widths]
                + [pl.BlockSpec(w_out.shape, const2, pipeline_mode=once),
                   pl.BlockSpec((1, d), const2),
                   pl.BlockSpec((1, 1, d), vec),
                   pl.BlockSpec((1, 1, d), vec),
                   pl.BlockSpec((1, 1, d), vec),
                   pl.BlockSpec((d, LANES), const2),
                   pl.BlockSpec((d, LANES), const2),
                   pl.BlockSpec((1, LANES), const2),
                   pl.BlockSpec((MOE_EXPERTS, d, MOE_FF), const3, pipeline_mode=once),
                   pl.BlockSpec((MOE_EXPERTS, d, MOE_FF), const3, pipeline_mode=once),
                   pl.BlockSpec((MOE_EXPERTS * MOE_FF, d), const2, pipeline_mode=once)])
    args = [x, g1, *mixes, w_out.astype(BF16), g, sc, sh, gate, wr_hi, wr_lo, br,
            w_gate.astype(BF16), w_up.astype(BF16), w_down.astype(BF16).reshape(MOE_EXPERTS * MOE_FF, d)]
    if final:
        in_specs.append(pl.BlockSpec((1, d), const2))
        args.append(final_g.reshape(1, d))
    return pl.pallas_call(
        functools.partial(_moe_body, widths=widths, final=final),
        grid=(b, s // tm),
        in_specs=in_specs,
        out_specs=pl.BlockSpec((1, tm, d), row),
        out_shape=jax.ShapeDtypeStruct((b, s, d), F32),
        scratch_shapes=[pltpu.VMEM((tm, MOE_EXPERTS * MOE_FF), BF16)],
        compiler_params=_params("parallel", "parallel"),
        name="mix_moe",
    )(*args)


ODD_PAD = 2688


def _odd_proj_body(x_ref, g_ref, sc_ref, sh_ref, w_ref, c_ref, s1_ref, s2_ref,
                   q_ref, kvf_ref, kvx_ref, gates_ref):
    tm = x_ref.shape[1]
    h = _norm_mod(x_ref[0], g_ref[...], sc_ref[0], sh_ref[0]).astype(BF16)
    c, s1, s2 = c_ref[...], s1_ref[...], s2_ref[...]
    half = NSA_Q_W // 2
    for idx in range(2):
        acc = _dot(h, w_ref[:, idx * half:(idx + 1) * half])
        q_ref[0, :, idx * half:(idx + 1) * half] = (_rope(acc, c, s1, s2) * QK_SCALE).astype(BF16)
    lane = lax.broadcasted_iota(jnp.int32, (tm, LANES), 1)
    low = lane < HEAD_DIM
    blk = (pl.program_id(1) * tm + lax.broadcasted_iota(jnp.int32, (tm, LANES), 0)) // SLC_BLOCK
    ind = jnp.where(lane == blk + HEAD_DIM, 1.0, 0.0)
    for idx in range(6):
        c0 = NSA_Q_W + idx * NSA_KV_W
        acc = _dot(h, w_ref[:, c0:c0 + NSA_KV_W])
        if idx % 2 == 0:
            acc = _rope(acc, c, s1, s2)
        if idx < 2:
            for g in range(NSA_KV_HEADS):
                kvf_ref[0, idx * NSA_KV_HEADS + g] = acc[:, g * HEAD_DIM:(g + 1) * HEAD_DIM]
            continue
        fill = 1.0 if idx % 2 == 1 else (ind if idx == 2 else 0.0)
        for gp in range(NSA_KV_HEADS // 2):
            pair = acc[:, gp * LANES:(gp + 1) * LANES]
            swapped = pltpu.roll(pair, HEAD_DIM, 1)
            kvx_ref[0, (idx - 2) * NSA_KV_HEADS + 2 * gp] = jnp.where(low, pair, fill).astype(BF16)
            kvx_ref[0, (idx - 2) * NSA_KV_HEADS + 2 * gp + 1] = jnp.where(low, swapped, fill).astype(BF16)
    c0 = NSA_Q_W + 6 * NSA_KV_W
    gl = _dot(h, w_ref[:, c0:c0 + LANES])
    gates_ref[0] = 1.0 / (1.0 + jnp.exp(-gl))


def _odd_proj(x, g, sc, sh, w, tables):
    b, s, d = x.shape
    n = w.shape[1]
    tm = PROJ_TM
    row = lambda bi, i: (bi, i, 0)
    vec = lambda bi, i: (bi, 0, 0)
    tab = pl.BlockSpec((tm, LANES), lambda bi, i: (i, 0))
    hd = lambda bi, i: (bi, 0, i, 0)
    return pl.pallas_call(
        _odd_proj_body,
        grid=(b, s // tm),
        in_specs=[pl.BlockSpec((1, tm, d), row),
                  pl.BlockSpec((1, d), lambda bi, i: (0, 0)),
                  pl.BlockSpec((1, 1, d), vec),
                  pl.BlockSpec((1, 1, d), vec),
                  pl.BlockSpec((d, n), lambda bi, i: (0, 0)),
                  tab, tab, tab],
        out_specs=[pl.BlockSpec((1, tm, NSA_Q_W), row),
                   pl.BlockSpec((1, 2 * NSA_KV_HEADS, tm, HEAD_DIM), hd),
                   pl.BlockSpec((1, 4 * NSA_KV_HEADS, tm, LANES), hd),
                   pl.BlockSpec((1, tm, LANES), row)],
        out_shape=[jax.ShapeDtypeStruct((b, s, NSA_Q_W), BF16),
                   jax.ShapeDtypeStruct((b, 2 * NSA_KV_HEADS, s, HEAD_DIM), F32),
                   jax.ShapeDtypeStruct((b, 4 * NSA_KV_HEADS, s, LANES), BF16),
                   jax.ShapeDtypeStruct((b, s, LANES), F32)],
        compiler_params=_params("parallel", "parallel"),
        name="odd_proj",
    )(x, g, sc, sh, w, *tables)


def _compress_body(x_ref, pos_ref, w1_ref, b1_ref, w2_ref, b2_ref, o_ref):
    x = x_ref[0, 0]
    half = CMP_STRIDE * HEAD_DIM
    xa = (x + pos_ref[0, 0:1]).astype(BF16)
    xb = (x + pos_ref[0, 1:2]).astype(BF16)
    a = _dot(xa, w1_ref[0, 0:half])
    bm = _dot(xb, w1_ref[0, half:2 * half])
    nrow = x.shape[0]
    pre = (a + pltpu.roll(bm, nrow - 1, 0)) + b1_ref[0]
    hid = 0.5 * pre * (1.0 + jnp.tanh(math.sqrt(2.0 / math.pi) * (pre + 0.044715 * (pre * pre * pre))))
    o_ref[0, 0] = (_dot(hid.astype(BF16), w2_ref[0]) + b2_ref[0]).astype(o_ref.dtype)


def _compress(kvf, pos, w1, b1, w2, b2):
    b, n2, s, hd = kvf.shape
    g = n2 // 2
    nchunk = s // CMP_STRIDE
    half = CMP_STRIDE * hd
    x = kvf.reshape(b, n2, nchunk, half)
    kv = lambda bi, n: (n // g, 0, 0)
    return pl.pallas_call(
        _compress_body,
        grid=(b, n2),
        in_specs=[pl.BlockSpec((1, 1, nchunk, half), lambda bi, n: (bi, n, 0, 0)),
                  pl.BlockSpec((1, 2, half), kv),
                  pl.BlockSpec((1, 2 * half, CMP_HIDDEN), kv),
                  pl.BlockSpec((1, 1, CMP_HIDDEN), kv),
                  pl.BlockSpec((1, CMP_HIDDEN, hd), kv),
                  pl.BlockSpec((1, 1, hd), kv)],
        out_specs=pl.BlockSpec((1, 1, nchunk, hd), lambda bi, n: (bi, n, 0, 0)),
        out_shape=jax.ShapeDtypeStruct((b, n2, nchunk, hd), BF16),
        compiler_params=_params("parallel", "parallel"),
        name="nsa_compress",
    )(x, pos.reshape(2, 2, half), w1.astype(BF16), b1.reshape(2, 1, CMP_HIDDEN),
      w2.astype(BF16), b2.reshape(2, 1, hd))


def _nsa_tile(i, first_half, q_ref, gt_ref, kc, vc, ks_ref, vs_ref, kw_ref, vw_ref, ovt):
    tq = q_ref.shape[1]
    r = NSA_GROUP
    q0 = i * tq
    qf = q_ref[0]
    q4 = jnp.concatenate([qf[:, h * HEAD_DIM:(h + 1) * HEAD_DIM] for h in range(r)], axis=0)
    qpos_c = q0 + lax.broadcasted_iota(jnp.int32, (tq, 1), 0)
    qpos4 = jnp.concatenate([qpos_c] * r, axis=0)

    nc = kc.shape[0]
    s_c = _dot_nt(q4, kc)
    cmp_end = lax.broadcasted_iota(jnp.int32, (1, nc), 1) * CMP_STRIDE + (CMP_BLOCK - 1)
    s_c = jnp.where(cmp_end <= qpos4, s_c, NEG_INF)
    e_c = jnp.exp2(s_c - jnp.max(s_c, axis=1, keepdims=True))
    p_c = e_c / jnp.sum(e_c, axis=1, keepdims=True)
    p_c = jnp.where(qpos4 >= CMP_BLOCK - 1, p_c, 0.0)
    o_c = _dot(p_c.astype(BF16), vc)
    yield

    p_sum = p_c[0:tq]
    for h in range(1, r):
        p_sum = p_sum + p_c[h * tq:(h + 1) * tq]
    ps_hi, ps_lo = _split_bf16(p_sum)
    imp = _dot_nt(ovt, ps_hi) + _dot_nt(ovt, ps_lo)
    ns = imp.shape[0]
    blk = lax.broadcasted_iota(jnp.int32, (ns, tq), 0)
    qpos_r = q0 + lax.broadcasted_iota(jnp.int32, (ns, tq), 1)
    own = qpos_r // SLC_BLOCK
    started = blk * SLC_BLOCK <= qpos_r
    forced = (blk == 0) | (blk == own) | (blk == own - 1)
    imp = jnp.where(started, jnp.where(forced, FORCE_SCORE, imp), NEG_INF)
    bias_t = jnp.where(_rank_below(imp, SLC_TOPN, ns // 2 if first_half else ns), 0.0, NEG_INF)
    parts = [jnp.zeros((HEAD_DIM, tq), F32), bias_t]
    if ns < LANES - HEAD_DIM:
        parts.append(jnp.zeros((LANES - HEAD_DIM - ns, tq), F32))
    bias = jnp.transpose(jnp.concatenate(parts, axis=0)).astype(BF16)
    lane4 = lax.broadcasted_iota(jnp.int32, (r * tq, LANES), 1)
    qz = jnp.concatenate([q4, jnp.zeros_like(q4)], axis=1)
    qs = jnp.where(lane4 < HEAD_DIM, qz, jnp.concatenate([bias] * r, axis=0))
    yield

    tk = NSA_TK
    d0 = pl.multiple_of((q0 // tk) * tk, tk)
    causal = d0 + lax.broadcasted_iota(jnp.int32, (1, tk), 1) <= qpos4
    st = _online_step(qs, ks_ref[0, 0, pl.ds(d0, tk), :], vs_ref[0, 0, pl.ds(d0, tk), :], causal, None, None)
    yield

    span = WINDOW + tq
    w0 = pl.multiple_of(jnp.maximum(q0 - WINDOW, 0), tq)
    kpos_w = w0 + lax.broadcasted_iota(jnp.int32, (1, tq), 1)
    s_w = _dot_nt(qz, kw_ref[0, 0, pl.ds(w0, span), :])
    cols = [jnp.where(kpos_w > qpos4 - WINDOW, s_w[:, :tq], NEG_INF)]
    if first_half:
        cols = [jnp.where(kpos_w <= qpos4, cols[0], NEG_INF)]
        cols += [jnp.where(kpos_w + c * tq <= qpos4, s_w[:, c * tq:(c + 1) * tq], NEG_INF)
                 for c in range(1, span // tq)]
    else:
        cols += [s_w[:, tq:span - tq],
                 jnp.where(kpos_w + (span - tq) <= qpos4, s_w[:, span - tq:], NEG_INF)]
    s_w = jnp.concatenate(cols, axis=1)
    p_w = jnp.exp2(s_w - jnp.max(s_w, axis=1, keepdims=True)).astype(BF16)
    acc_w = _dot(p_w, vw_ref[0, 0, pl.ds(w0, span), :])
    o_w = acc_w[:, :HEAD_DIM] / acc_w[:, HEAD_DIM:]
    gt = gt_ref[0, 0]

    def finish(acc_s):
        o_s = acc_s[:, :HEAD_DIM] / acc_s[:, HEAD_DIM:]
        outs = []
        for h in range(r):
            sl = slice(h * tq, (h + 1) * tq)
            outs.append(gt[:, 3 * h:3 * h + 1] * o_c[sl] + gt[:, 3 * h + 1:3 * h + 2] * o_s[sl]
                        + gt[:, 3 * h + 2:3 * h + 3] * o_w[sl])
        return jnp.concatenate(outs, axis=1)

    return [qs], [st], finish


def _nsa_body(qa_ref, qb_ref, kc_ref, vc_ref, ks_ref, vs_ref, kw_ref, vw_ref, ga_ref, gb_ref, ovt_ref,
              oa_ref, ob_ref):
    a = pl.program_id(2)
    tq = qa_ref.shape[1]
    tk = NSA_TK
    nt = ks_ref.shape[2] // tq
    kc, vc, ovt = kc_ref[0, 0], vc_ref[0, 0], ovt_ref[...]
    per = tk // tq
    n_chunks = ks_ref.shape[2] // tk
    long_tile = _Staged(_nsa_tile(nt - 1 - a, False, qb_ref, gb_ref, kc, vc, ks_ref, vs_ref, kw_ref, vw_ref, ovt))
    short_tile = _Staged(_nsa_tile(a, True, qa_ref, ga_ref, kc, vc, ks_ref, vs_ref, kw_ref, vw_ref, ovt))
    acc_b, acc_a = _attn_pair(n_chunks - 1, (nt // 2 - 1) // per, a // per, tk, long_tile.result(), short_tile,
                              lambda e, k0: ks_ref[0, 0, pl.ds(k0, tk), :],
                              lambda e, k0: vs_ref[0, 0, pl.ds(k0, tk), :], late_pv=True)
    oa_ref[0] = short_tile.result()[2](acc_a[0]).astype(oa_ref.dtype)
    ob_ref[0] = long_tile.result()[2](acc_b[0]).astype(ob_ref.dtype)


def _nsa_attention(q, cmp, kvx, gates):
    b, s, _ = q.shape
    g = NSA_KV_HEADS
    tq = NSA_TQ
    nt = s // tq
    nc = cmp.shape[2]
    ns = s // SLC_BLOCK
    assert ns <= LANES - HEAD_DIM
    assert nt % (2 * NSA_TK // tq) == 0 and s // 2 >= WINDOW
    cw = NSA_GROUP * HEAD_DIM
    cs = jnp.arange(nc)[None, :] * CMP_STRIDE
    ss = jnp.arange(ns)[:, None] * SLC_BLOCK
    ovt = ((cs <= ss + SLC_BLOCK - 1) & (cs + CMP_BLOCK - 1 >= ss)).astype(BF16)
    head = lambda off: (lambda bi, gi, a: (bi, off + gi, 0, 0))
    half = jax.ShapeDtypeStruct((b, s // 2, NSA_Q_W), BF16)
    return pl.pallas_call(
        _nsa_body,
        grid=(b, g, nt // 2),
        in_specs=[pl.BlockSpec((1, tq, cw), lambda bi, gi, a: (bi, a, gi)),
                  pl.BlockSpec((1, tq, cw), lambda bi, gi, a: (bi, nt - 1 - a, gi)),
                  pl.BlockSpec((1, 1, nc, HEAD_DIM), head(0)),
                  pl.BlockSpec((1, 1, nc, HEAD_DIM), head(g)),
                  pl.BlockSpec((1, 1, s, LANES), head(0)),
                  pl.BlockSpec((1, 1, s, LANES), head(g)),
                  pl.BlockSpec((1, 1, s, LANES), head(2 * g)),
                  pl.BlockSpec((1, 1, s, LANES), head(3 * g)),
                  pl.BlockSpec((1, 1, tq, 3 * NSA_GROUP), lambda bi, gi, a: (bi, gi, a, 0)),
                  pl.BlockSpec((1, 1, tq, 3 * NSA_GROUP), lambda bi, gi, a: (bi, gi, nt - 1 - a, 0)),
                  pl.BlockSpec((ns, nc), lambda bi, gi, a: (0, 0))],
        out_specs=[pl.BlockSpec((1, tq, cw), lambda bi, gi, a: (bi, a, gi)),
                   pl.BlockSpec((1, tq, cw), lambda bi, gi, a: (bi, nt // 2 - 1 - a, gi))],
        out_shape=[half, half],
        compiler_params=_params("parallel", "parallel", "arbitrary"),
        name="nsa_attn",
    )(q, q, cmp, cmp, kvx, kvx, kvx, kvx, gates, gates, ovt)


def kernel(x, c, norm1_g, norm2_g, final_g, ada_w, ada_b, ev_w_in, ev_w_out, ev_lambda, ev_subln_g,
           od_w_in, od_w_out, od_cmp_pos, od_cmp_w1, od_cmp_b1, od_cmp_w2, od_cmp_b2,
           moe_wg, moe_bg, moe_we, moe_be, moe_w_gate, moe_w_up, moe_w_down):
    b, s, d = x.shape
    tables = _rope_tables(s)
    mod = _ada_mod(c, ada_w, ada_b)
    for l in range(DEPTH):
        sh1, sc1, g1, sh2, sc2, g2 = (mod[l, :, None, k * d:(k + 1) * d] for k in range(6))
        i = l // 2
        if l % 2 == 0:
            lambda_init = 0.8 - 0.6 * math.exp(-0.3 * l)
            qa, kx, vx, pb, kmean = _even_proj(x, norm1_g[l].reshape(1, d), sc1, sh1,
                                               ev_w_in[i].astype(BF16), tables)
            oa = _moba_attention(qa, kx, vx, kmean.reshape(b, s // MOBA_BLOCK, MOBA_W))
            ob = _diff_attention(pb, ev_lambda[i], ev_subln_g[i], lambda_init)
            mixes = (jnp.concatenate(oa, axis=1), jnp.concatenate(ob, axis=1))
            w_out = ev_w_out[i]
        else:
            w = jnp.pad(od_w_in[i], ((0, 0), (0, ODD_PAD - ODD_IN))).astype(BF16)
            q, kvf, kvx, gates = _odd_proj(x, norm1_g[l].reshape(1, d), sc1, sh1, w, tables)
            cmp = _compress(kvf, od_cmp_pos[i], od_cmp_w1[i], od_cmp_b1[i], od_cmp_w2[i], od_cmp_b2[i])
            gt = gates[:, :, :N_GATES].reshape(b, s, NSA_KV_HEADS, 3 * NSA_GROUP).transpose(0, 2, 1, 3)
            mixes = (jnp.concatenate(_nsa_attention(q, cmp, kvx, gt), axis=1),)
            w_out = od_w_out[i]
        x = _mix_moe(x, g1, mixes, w_out, norm2_g[l].reshape(1, d), sc2, sh2, g2, moe_wg[l], moe_bg[l],
                     moe_we[l], moe_be[l], moe_w_gate[l], moe_w_up[l], moe_w_down[l],
                     final_g if l == DEPTH - 1 else None)
    return x
```

```python
import functools
import math

import jax
import jax.numpy as jnp
from jax import lax
from jax.experimental import pallas as pl
from jax.experimental.pallas import tpu as pltpu

F32 = jnp.float32
BF16 = jnp.bfloat16

D_MODEL = 1024
DEPTH = 4
HEAD_DIM = 64
ROPE_DIM = HEAD_DIM // 4
ROPE_HALF = ROPE_DIM // 2
ROPE_THETA = 500000.0
NORM_EPS = 1e-6
NEG_INF = -1e30
FORCE_SCORE = 1e6
QK_SCALE = HEAD_DIM ** -0.5 * math.log2(math.e)

MOBA_HEADS = 8
MOBA_BLOCK = 256
MOBA_TOPK = 3
DIFF_HEADS = 4
MOBA_W = MOBA_HEADS * HEAD_DIM
DIFF_QK_W = DIFF_HEADS * 2 * HEAD_DIM
DIFF_V_W = DIFF_HEADS * 2 * HEAD_DIM
DIFF_W = 2 * DIFF_QK_W + DIFF_V_W
EVEN_IN = 3 * MOBA_W + DIFF_W

NSA_HEADS = 16
NSA_GROUP = 4
NSA_KV_HEADS = 4
CMP_BLOCK = 32
CMP_STRIDE = 16
CMP_HIDDEN = 256
SLC_BLOCK = 64
SLC_TOPN = 16
WINDOW = 512
NSA_Q_W = NSA_HEADS * HEAD_DIM
NSA_KV_W = NSA_KV_HEADS * HEAD_DIM
ODD_IN = NSA_Q_W + 6 * NSA_KV_W + 3 * NSA_HEADS
N_GATES = 3 * NSA_HEADS

MOE_GROUPS = 4
MOE_PER_GROUP = 4
MOE_EXPERTS = 16
MOE_FF = 256

LANES = 128
VMEM_LIMIT = 56 * 1024 * 1024

PROJ_TM = 512
ATT_TQ = 512
ATT_TK = 512
NSA_TQ = 256
NSA_TK = 512


def _params(*sem):
    return pltpu.CompilerParams(dimension_semantics=sem, vmem_limit_bytes=VMEM_LIMIT)


def _dot(a, b):
    return jnp.dot(a, b, preferred_element_type=F32)


def _dot_nt(a, b):
    return lax.dot_general(a, b, (((1,), (1,)), ((), ())), preferred_element_type=F32)


def _split_bf16(x):
    hi = x.astype(BF16)
    lo = (x - hi.astype(F32)).astype(BF16)
    return hi, lo


def _norm_mod(x, g, sc, sh):
    y = x * lax.rsqrt(jnp.mean(x * x, axis=-1, keepdims=True) + NORM_EPS)
    return (y * g) * (1.0 + sc) + sh


def _rope(t, c, s1, s2):
    w = t.shape[1]
    k = w // LANES
    cw = jnp.concatenate([c] * k, axis=1) if k > 1 else c
    s1w = jnp.concatenate([s1] * k, axis=1) if k > 1 else s1
    s2w = jnp.concatenate([s2] * k, axis=1) if k > 1 else s2
    return t * cw + pltpu.roll(t, ROPE_HALF, 1) * s1w + pltpu.roll(t, w - ROPE_HALF, 1) * s2w


def _rope_tables(seq):
    pos = jnp.arange(seq, dtype=F32)
    inv = ROPE_THETA ** (-jnp.arange(0, ROPE_DIM, 2, dtype=F32) / ROPE_DIM)
    ang = pos[:, None] * inv[None, :]
    cos, sin = jnp.cos(ang), jnp.sin(ang)
    ones = jnp.ones((seq, HEAD_DIM - ROPE_DIM), F32)
    zeros8 = jnp.zeros((seq, ROPE_HALF), F32)
    zeros = jnp.zeros((seq, HEAD_DIM - ROPE_DIM), F32)
    c = jnp.concatenate([cos, cos, ones], axis=1)
    s1 = jnp.concatenate([zeros8, sin, zeros], axis=1)
    s2 = jnp.concatenate([-sin, zeros8, zeros], axis=1)
    rep = LANES // HEAD_DIM
    return tuple(jnp.tile(t, (1, rep)) for t in (c, s1, s2))


def _rank_below(v, k, rows):
    n = v.shape[0]
    sub = 8
    groups = [v[g:g + sub] for g in range(0, n, sub)]
    cnts = [jnp.zeros(g.shape, F32) for g in groups]
    idx = lax.broadcasted_iota(jnp.int32, groups[0].shape, 0)
    for m in range(rows):
        rm = v[m:m + 1, :]
        for j, g in enumerate(groups):
            if j * sub > m:
                beat = rm >= g
            elif j * sub + sub - 1 < m:
                beat = rm > g
            else:
                beat = (rm > g) | ((rm == g) & (idx > m - j * sub))
            cnts[j] = cnts[j] + jnp.where(beat, 1.0, 0.0)
    return jnp.concatenate(cnts, axis=0) < k


def _online_step(q, k, v, mask, m, acc):
    s = _dot_nt(q, k)
    if mask is not None:
        s = jnp.where(mask, s, NEG_INF)
    m_new = jnp.max(s, axis=1, keepdims=True)
    if m is not None:
        m_new = jnp.maximum(m, m_new)
    p = jnp.exp2(s - m_new).astype(v.dtype)
    pv = _dot(p, v)
    if m is None:
        return m_new, pv
    return m_new, jnp.exp2(m - m_new) * acc + pv


class _Staged:
    def __init__(self, gen):
        self.gen, self.done, self.value = gen, False, None

    def advance(self):
        if not self.done:
            try:
                next(self.gen)
            except StopIteration as stop:
                self.done, self.value = True, stop.value

    def result(self):
        while not self.done:
            self.advance()
        return self.value


def _attn_pair(n_past, max_short, cnt_short, tk, long_tile, short_tile, key, value, late_pv):
    qs_l, st_l = long_tile[:2]
    ns = len(qs_l)
    n_static = n_past - max_short
    cnt_long = n_past - cnt_short
    q_cur = list(qs_l)
    m_cur = [st_l[e][0] for e in range(ns)]
    acc_cur = [st_l[e][1] for e in range(ns)]
    out_long = list(acc_cur)
    pend = None

    def flush():
        return [acc_cur[e] + _dot(pend[0][e], value(e, pend[1])) for e in range(ns)]

    for u in range(n_past):
        if pend is not None:
            acc_cur = flush()
        if u >= n_static:
            qs_s, st_s = short_tile.result()[:2]
            sw = u == cnt_long
            out_long = [jnp.where(sw, acc_cur[e], out_long[e]) for e in range(ns)]
            acc_cur = [jnp.where(sw, st_s[e][1], acc_cur[e]) for e in range(ns)]
            m_cur = [jnp.where(sw, st_s[e][0], m_cur[e]) for e in range(ns)]
            q_cur = [jnp.where(sw, qs_s[e], q_cur[e]) for e in range(ns)]
            chunk = jnp.where(u < cnt_long, u, u - cnt_long)
        else:
            chunk = u
        k0 = pl.multiple_of(chunk * tk, tk) if u >= n_static else chunk * tk
        if late_pv:
            ss = [_dot_nt(q_cur[e], key(e, k0)) for e in range(ns)]
            ps = []
            for e in range(ns):
                m_new = jnp.maximum(m_cur[e], jnp.max(ss[e], axis=1, keepdims=True))
                ps.append(jnp.exp2(ss[e] - m_new).astype(BF16))
                acc_cur[e] = jnp.exp2(m_cur[e] - m_new) * acc_cur[e]
                m_cur[e] = m_new
            pend = (ps, k0)
        else:
            for e in range(ns):
                m_cur[e], acc_cur[e] = _online_step(q_cur[e], key(e, k0), value(e, k0), None,
                                                    m_cur[e], acc_cur[e])
        if u < n_static:
            short_tile.advance()
    if pend is not None:
        acc_cur = flush()
    st_s = short_tile.result()[1]
    none_short = cnt_short == 0
    out_long = [jnp.where(none_short, acc_cur[e], out_long[e]) for e in range(ns)]
    out_short = [jnp.where(none_short, st_s[e][1], acc_cur[e]) for e in range(ns)]
    return out_long, out_short


def _ada_body(c_ref, w_ref, b_ref, o_ref):
    c = c_ref[...]
    cs = c * (1.0 / (1.0 + jnp.exp(-c)))
    o_ref[0] = jnp.dot(cs, w_ref[0], preferred_element_type=F32,
                       precision=lax.Precision.HIGHEST) + b_ref[0]


def _ada_mod(c, ada_w, ada_b):
    b, d = c.shape
    depth, _, n = ada_w.shape
    rows = 8
    tn = 1536
    cp = jnp.pad(c, ((0, rows - b), (0, 0)))
    out = pl.pallas_call(
        _ada_body,
        grid=(depth, n // tn),
        in_specs=[pl.BlockSpec((rows, d), lambda l, j: (0, 0)),
                  pl.BlockSpec((1, d, tn), lambda l, j: (l, 0, j)),
                  pl.BlockSpec((1, 1, tn), lambda l, j: (l, 0, j))],
        out_specs=pl.BlockSpec((1, rows, tn), lambda l, j: (l, 0, j)),
        out_shape=jax.ShapeDtypeStruct((depth, rows, n), F32),
        compiler_params=_params("parallel", "parallel"),
        name="ada_mod",
    )(cp, ada_w, ada_b.reshape(depth, 1, n))
    return out[:, :b]


def _even_proj_body(x_ref, g_ref, sc_ref, sh_ref, w_ref, c_ref, s1_ref, s2_ref,
                    q_ref, k_ref, v_ref, pb_ref, km_ref):
    tm = x_ref.shape[1]
    h = _norm_mod(x_ref[0], g_ref[...], sc_ref[0], sh_ref[0]).astype(BF16)
    c, s1, s2 = c_ref[...], s1_ref[...], s2_ref[...]
    ch = MOBA_W
    lane = lax.broadcasted_iota(jnp.int32, (tm, LANES), 1)
    low = lane < HEAD_DIM
    blk = (pl.program_id(1) * tm + lax.broadcasted_iota(jnp.int32, (tm, LANES), 0)) // MOBA_BLOCK
    ind_hi = jnp.where(lane == blk + HEAD_DIM, 1.0, 0.0)
    ind_lo = jnp.where(lane == blk, 1.0, 0.0)
    for idx, kind in enumerate(("q", "k", "v", "q", "k", "v")):
        acc = _dot(h, w_ref[:, idx * ch:(idx + 1) * ch])
        if kind != "v":
            acc = _rope(acc, c, s1, s2)
        if kind == "q":
            acc = acc * QK_SCALE
        if idx == 0:
            q_ref[0] = acc.astype(BF16)
        elif idx == 1:
            nblk = tm // MOBA_BLOCK
            km_ref[0, 0] = jnp.concatenate(
                [jnp.mean(acc[n * MOBA_BLOCK:(n + 1) * MOBA_BLOCK], axis=0, keepdims=True)
                 for n in range(nblk)], axis=0)
            for hp in range(ch // LANES):
                kp = acc[:, hp * LANES:(hp + 1) * LANES]
                k_ref[0, 2 * hp] = jnp.where(low, kp, ind_hi).astype(BF16)
                k_ref[0, 2 * hp + 1] = jnp.where(low, ind_lo, kp).astype(BF16)
        elif idx == 2:
            for hp in range(ch // LANES):
                vp = acc[:, hp * LANES:(hp + 1) * LANES]
                v_ref[0, 2 * hp] = jnp.where(low, vp, 1.0).astype(BF16)
                v_ref[0, 2 * hp + 1] = jnp.where(low, 1.0, vp).astype(BF16)
        else:
            pb_ref[0, :, (idx - 3) * ch:(idx - 2) * ch] = acc.astype(BF16)


def _even_proj(x, g, sc, sh, w, tables):
    b, s, d = x.shape
    n = w.shape[1]
    tm = PROJ_TM
    nblk = tm // MOBA_BLOCK
    row = lambda bi, i: (bi, i, 0)
    vec = lambda bi, i: (bi, 0, 0)
    hd = lambda bi, i: (bi, 0, i, 0)
    tab = pl.BlockSpec((tm, LANES), lambda bi, i: (i, 0))
    return pl.pallas_call(
        _even_proj_body,
        grid=(b, s // tm),
        in_specs=[pl.BlockSpec((1, tm, d), row),
                  pl.BlockSpec((1, d), lambda bi, i: (0, 0)),
                  pl.BlockSpec((1, 1, d), vec),
                  pl.BlockSpec((1, 1, d), vec),
                  pl.BlockSpec((d, n), lambda bi, i: (0, 0)),
                  tab, tab, tab],
        out_specs=[pl.BlockSpec((1, tm, MOBA_W), row),
                   pl.BlockSpec((1, MOBA_HEADS, tm, LANES), hd),
                   pl.BlockSpec((1, MOBA_HEADS, tm, LANES), hd),
                   pl.BlockSpec((1, tm, DIFF_W), row),
                   pl.BlockSpec((1, 1, nblk, MOBA_W), lambda bi, i: (bi, i, 0, 0))],
        out_shape=[jax.ShapeDtypeStruct((b, s, MOBA_W), BF16),
                   jax.ShapeDtypeStruct((b, MOBA_HEADS, s, LANES), BF16),
                   jax.ShapeDtypeStruct((b, MOBA_HEADS, s, LANES), BF16),
                   jax.ShapeDtypeStruct((b, s, DIFF_W), BF16),
                   jax.ShapeDtypeStruct((b, s // tm, nblk, MOBA_W), F32)],
        compiler_params=_params("parallel", "parallel"),
        name="even_proj",
    )(x, g, sc, sh, w, *tables)


def _moba_body(qa_ref, qb_ref, k_ref, v_ref, km_ref, oa_ref, ob_ref):
    a = pl.program_id(2)
    tq = qa_ref.shape[1]
    nb = km_ref.shape[1]
    tk = ATT_TK
    nt = k_ref.shape[2] // tq
    per = tk // tq
    km = km_ref[0]
    lane = lax.broadcasted_iota(jnp.int32, (tq, LANES), 1)
    lane_k = lax.broadcasted_iota(jnp.int32, (nb, LANES), 1)
    n_idx = lax.broadcasted_iota(jnp.int32, (nb, tq), 0)
    low = lane < HEAD_DIM

    def tile(i, q_ref, rows):
        q = q_ref[0]
        c0 = pl.multiple_of((i // per) * tk, tk)
        qpos = i * tq + lax.broadcasted_iota(jnp.int32, (tq, 1), 0)
        causal = c0 + lax.broadcasted_iota(jnp.int32, (1, tk), 1) <= qpos
        own_blk = (i * tq + lax.broadcasted_iota(jnp.int32, (nb, tq), 1)) // MOBA_BLOCK
        q_t, st_t = [], []
        for e in range(2):
            own = low if e == 0 else (lane >= HEAD_DIM)
            own_k = (lane_k < HEAD_DIM) if e == 0 else (lane_k >= HEAD_DIM)
            qm = jnp.where(own, q, jnp.zeros_like(q))
            km_hi, km_lo = _split_bf16(jnp.where(own_k, km, 0.0))
            gs = _dot_nt(km_hi, qm) + _dot_nt(km_lo, qm)
            gs = jnp.where(n_idx < own_blk, gs, NEG_INF)
            keep = (_rank_below(gs, MOBA_TOPK, rows) & (n_idx < own_blk)) | (n_idx == own_blk)
            bias_t = jnp.where(keep, 0.0, NEG_INF)
            off = HEAD_DIM * (1 - e)
            parts = [jnp.zeros((off, tq), F32)] if off else []
            pad = jnp.concatenate(parts + [bias_t, jnp.zeros((LANES - off - nb, tq), F32)], axis=0)
            q_t.append(jnp.where(own, q, jnp.transpose(pad).astype(BF16)))
            yield
        for e in range(2):
            st_t.append(_online_step(q_t[e], k_ref[0, e, pl.ds(c0, tk), :], v_ref[0, e, pl.ds(c0, tk), :],
                                     causal, None, None))
            yield
        return q_t, st_t

    long_tile = _Staged(tile(nt - 1 - a, qb_ref, nb)).result()
    short_tile = _Staged(tile(a, qa_ref, nb // 2))
    n_chunks = k_ref.shape[2] // tk
    acc_b, acc_a = _attn_pair(n_chunks - 1, (nt // 2 - 1) // per, a // per, tk, long_tile, short_tile,
                              lambda e, k0: k_ref[0, e, pl.ds(k0, tk), :],
                              lambda e, k0: v_ref[0, e, pl.ds(k0, tk), :], late_pv=True)
    for (acc0, acc1), o_ref in ((acc_a, oa_ref), (acc_b, ob_ref)):
        num = jnp.where(low, acc0, acc1)
        den = jnp.where(low, pltpu.roll(acc0, HEAD_DIM, 1), pltpu.roll(acc1, HEAD_DIM, 1))
        o_ref[0] = (num / den).astype(o_ref.dtype)


def _moba_attention(q, kx, vx, kmean):
    b, s, _ = q.shape
    nb = s // MOBA_BLOCK
    pairs = MOBA_W // LANES
    tq = ATT_TQ
    nt = s // tq
    assert tq % MOBA_BLOCK == 0 and ATT_TK % tq == 0 and nt % (2 * ATT_TK // tq) == 0
    half = jax.ShapeDtypeStruct((b, s // 2, MOBA_W), BF16)
    return pl.pallas_call(
        _moba_body,
        grid=(b, pairs, nt // 2),
        in_specs=[pl.BlockSpec((1, tq, LANES), lambda bi, hp, a: (bi, a, hp)),
                  pl.BlockSpec((1, tq, LANES), lambda bi, hp, a: (bi, nt - 1 - a, hp)),
                  pl.BlockSpec((1, 2, s, LANES), lambda bi, hp, a: (bi, hp, 0, 0)),
                  pl.BlockSpec((1, 2, s, LANES), lambda bi, hp, a: (bi, hp, 0, 0)),
                  pl.BlockSpec((1, nb, LANES), lambda bi, hp, a: (bi, 0, hp))],
        out_specs=[pl.BlockSpec((1, tq, LANES), lambda bi, hp, a: (bi, a, hp)),
                   pl.BlockSpec((1, tq, LANES), lambda bi, hp, a: (bi, nt // 2 - 1 - a, hp))],
        out_shape=[half, half],
        compiler_params=_params("parallel", "parallel", "arbitrary"),
        name="moba_attn",
    )(q, q, kx, vx, kmean)


def _diff_body(lam_ref, g_ref, qa_ref, qb_ref, k_ref, v_ref, oa_ref, ob_ref, *, lambda_init):
    a = pl.program_id(2)
    tq = qa_ref.shape[1]
    tk = ATT_TK
    nt = k_ref.shape[1] // tq
    per = tk // tq
    lp = lam_ref[...]
    lam = (jnp.exp(jnp.sum(lp[0:1] * lp[1:2], axis=1, keepdims=True))
           - jnp.exp(jnp.sum(lp[2:3] * lp[3:4], axis=1, keepdims=True)) + lambda_init)
    lane = lax.broadcasted_iota(jnp.int32, (tq, LANES), 1)
    ones = jnp.ones((tk, LANES), BF16)

    def key(m, k0):
        return k_ref[0, pl.ds(k0, tk), :]

    def value(m, k0):
        return jnp.concatenate([v_ref[0, pl.ds(k0, tk), :], ones], axis=1)

    def tile(i, q_ref):
        q = q_ref[0]
        q_t = [jnp.where(lane < HEAD_DIM, q, jnp.zeros_like(q)), jnp.where(lane >= HEAD_DIM, q, jnp.zeros_like(q))]
        c0 = pl.multiple_of((i // per) * tk, tk)
        qpos = i * tq + lax.broadcasted_iota(jnp.int32, (tq, 1), 0)
        causal = c0 + lax.broadcasted_iota(jnp.int32, (1, tk), 1) <= qpos
        st_t = []
        for m in range(2):
            st_t.append(_online_step(q_t[m], key(m, c0), value(m, c0), causal, None, None))
            yield
        return q_t, st_t

    long_tile = _Staged(tile(nt - 1 - a, qb_ref)).result()
    short_tile = _Staged(tile(a, qa_ref))
    n_chunks = k_ref.shape[1] // tk
    acc_b, acc_a = _attn_pair(n_chunks - 1, (nt // 2 - 1) // per, a // per, tk, long_tile, short_tile,
                              key, value, late_pv=False)
    for (a0, a1), o_ref in ((acc_a, oa_ref), (acc_b, ob_ref)):
        o = a0[:, :LANES] / a0[:, LANES:] - lam * (a1[:, :LANES] / a1[:, LANES:])
        y = o * lax.rsqrt(jnp.mean(o * o, axis=-1, keepdims=True) + NORM_EPS)
        o_ref[0] = ((y * g_ref[...]) * (1.0 - lambda_init)).astype(o_ref.dtype)


def _diff_attention(pb, lam_p, subln_g, lambda_init):
    b, s, _ = pb.shape
    tq = ATT_TQ
    nt = s // tq
    assert ATT_TK % tq == 0 and nt % (2 * ATT_TK // tq) == 0
    koff = DIFF_QK_W // LANES
    voff = 2 * koff
    half = jax.ShapeDtypeStruct((b, s // 2, DIFF_V_W), BF16)
    return pl.pallas_call(
        functools.partial(_diff_body, lambda_init=lambda_init),
        grid=(b, DIFF_HEADS, nt // 2),
        in_specs=[pl.BlockSpec((4, HEAD_DIM), lambda bi, h, a: (0, 0)),
                  pl.BlockSpec((1, LANES), lambda bi, h, a: (0, 0)),
                  pl.BlockSpec((1, tq, LANES), lambda bi, h, a: (bi, a, h)),
                  pl.BlockSpec((1, tq, LANES), lambda bi, h, a: (bi, nt - 1 - a, h)),
                  pl.BlockSpec((1, s, LANES), lambda bi, h, a: (bi, 0, koff + h)),
                  pl.BlockSpec((1, s, LANES), lambda bi, h, a: (bi, 0, voff + h))],
        out_specs=[pl.BlockSpec((1, tq, LANES), lambda bi, h, a: (bi, a, h)),
                   pl.BlockSpec((1, tq, LANES), lambda bi, h, a: (bi, nt // 2 - 1 - a, h))],
        out_shape=[half, half],
        compiler_params=_params("parallel", "parallel", "arbitrary"),
        name="diff_attn",
    )(lam_p, subln_g.reshape(1, LANES), pb, pb, pb, pb)


def _first_argmax_onehot(v, iota):
    mx = jnp.max(v, axis=1, keepdims=True)
    idx = jnp.min(jnp.where(v == mx, iota, float(v.shape[1])), axis=1, keepdims=True)
    return iota == idx, mx


def _moe_body(*refs, widths, final):
    nm = len(widths)
    x_ref, g1_ref = refs[0], refs[1]
    mix_refs = refs[2:2 + nm]
    (wo_ref, g_ref, sc_ref, sh_ref, gate_ref, wr_hi_ref, wr_lo_ref, br_ref,
     wg_ref, wu_ref, wd_ref) = refs[2 + nm:13 + nm]
    fg_ref = refs[13 + nm] if final else None
    o_ref, a_ref = refs[-2], refs[-1]
    y = None
    r0 = 0
    for m_ref, wd in zip(mix_refs, widths):
        t = _dot(m_ref[0], wo_ref[r0:r0 + wd, :])
        y = t if y is None else y + t
        r0 += wd
    x = x_ref[0] + g1_ref[0] * y
    h = _norm_mod(x, g_ref[...], sc_ref[0], sh_ref[0])
    h_hi, h_lo = _split_bf16(h)
    wr_hi = wr_hi_ref[...]
    r = (_dot(h_hi, wr_hi) + (_dot(h_lo, wr_hi) + _dot(h_hi, wr_lo_ref[...]))) + br_ref[...]
    tm = x.shape[0]
    gl = r[:, 0:MOE_GROUPS]
    iota = lax.broadcasted_iota(jnp.int32, (tm, MOE_GROUPS), 1).astype(F32)
    g_oh, g_mx = _first_argmax_onehot(gl, iota)
    gw = 1.0 / jnp.sum(jnp.exp(gl - g_mx), axis=1, keepdims=True)
    el_g = jnp.zeros((tm, MOE_PER_GROUP), F32)
    for g in range(MOE_GROUPS):
        lo = MOE_GROUPS + g * MOE_PER_GROUP
        el_g = el_g + jnp.where(g_oh[:, g:g + 1], r[:, lo:lo + MOE_PER_GROUP], 0.0)
    oh1, v1 = _first_argmax_onehot(el_g, iota)
    oh2, v2 = _first_argmax_onehot(jnp.where(oh1, -jnp.inf, el_g), iota)
    e2 = jnp.exp(v2 - v1)
    den = 1.0 + e2
    w_grp = jnp.where(oh1, 1.0 / den, 0.0) + jnp.where(oh2, e2 / den, 0.0)
    gsc = jnp.where(g_oh, gw, 0.0)

    hb = h_hi
    for e in range(MOE_EXPERTS):
        g, k = divmod(e, MOE_PER_GROUP)
        comb = gsc[:, g:g + 1] * w_grp[:, k:k + 1]
        gt = _dot(hb, wg_ref[e])
        up = _dot(hb, wu_ref[e])
        a = ((gt * (1.0 / (1.0 + jnp.exp(-gt)))) * up) * comb
        a_ref[:, e * MOE_FF:(e + 1) * MOE_FF] = a.astype(BF16)
    out = x + gate_ref[0] * _dot(a_ref[...], wd_ref[...])
    if final:
        out = (out * lax.rsqrt(jnp.mean(out * out, axis=-1, keepdims=True) + NORM_EPS)) * fg_ref[...]
    o_ref[0] = out


def _mix_moe(x, g1, mixes, w_out, g, sc, sh, gate, wg, bg, we, be, w_gate, w_up, w_down, final_g):
    b, s, d = x.shape
    tm = PROJ_TM
    nr = MOE_GROUPS + MOE_EXPERTS
    wr = jnp.pad(jnp.concatenate([wg, we], axis=1), ((0, 0), (0, LANES - nr)))
    br = jnp.pad(jnp.concatenate([bg, be], axis=0), (0, LANES - nr)).reshape(1, LANES)
    wr_hi, wr_lo = _split_bf16(wr)
    widths = tuple(m.shape[2] for m in mixes)
    final = final_g is not None
    row = lambda bi, i: (bi, i, 0)
    vec = lambda bi, i: (bi, 0, 0)
    const2 = lambda bi, i: (0, 0)
    const3 = lambda bi, i: (0, 0, 0)
    once = pl.Buffered(1)
    in_specs = ([pl.BlockSpec((1, tm, d), row), pl.BlockSpec((1, 1, d), vec)]
                + [pl.BlockSpec((1, tm, wd), row) for wd in widths]
                + [pl.BlockSpec(w_out.shape, const2, pipeline_mode=once),
                   pl.BlockSpec((1, d), const2),
                   pl.BlockSpec((1, 1, d), vec),
                   pl.BlockSpec((1, 1, d), vec),
                   pl.BlockSpec((1, 1, d), vec),
                   pl.BlockSpec((d, LANES), const2),
                   pl.BlockSpec((d, LANES), const2),
                   pl.BlockSpec((1, LANES), const2),
                   pl.BlockSpec((MOE_EXPERTS, d, MOE_FF), const3, pipeline_mode=once),
                   pl.BlockSpec((MOE_EXPERTS, d, MOE_FF), const3, pipeline_mode=once),
                   pl.BlockSpec((MOE_EXPERTS * MOE_FF, d), const2, pipeline_mode=once)])
    args = [x, g1, *mixes, w_out.astype(BF16), g, sc, sh, gate, wr_hi, wr_lo, br,
            w_gate.astype(BF16), w_up.astype(BF16), w_down.astype(BF16).reshape(MOE_EXPERTS * MOE_FF, d)]
    if final:
        in_specs.append(pl.BlockSpec((1, d), const2))
        args.append(final_g.reshape(1, d))
    return pl.pallas_call(
        functools.partial(_moe_body, widths=widths, final=final),
        grid=(b, s // tm),
        in_specs=in_specs,
        out_specs=pl.BlockSpec((1, tm, d), row),
        out_shape=jax.ShapeDtypeStruct((b, s, d), F32),
        scratch_shapes=[pltpu.VMEM((tm, MOE_EXPERTS * MOE_FF), BF16)],
        compiler_params=_params("parallel", "parallel"),
        name="mix_moe",
    )(*args)


ODD_PAD = 2688


def _odd_proj_body(x_ref, g_ref, sc_ref, sh_ref, w_ref, c_ref, s1_ref, s2_ref,
                   q_ref, kvf_ref, kvx_ref, gates_ref):
    tm = x_ref.shape[1]
    h = _norm_mod(x_ref[0], g_ref[...], sc_ref[0], sh_ref[0]).astype(BF16)
    c, s1, s2 = c_ref[...], s1_ref[...], s2_ref[...]
    half = NSA_Q_W // 2
    for idx in range(2):
        acc = _dot(h, w_ref[:, idx * half:(idx + 1) * half])
        q_ref[0, :, idx * half:(idx + 1) * half] = (_rope(acc, c, s1, s2) * QK_SCALE).astype(BF16)
    lane = lax.broadcasted_iota(jnp.int32, (tm, LANES), 1)
    low = lane < HEAD_DIM
    blk = (pl.program_id(1) * tm + lax.broadcasted_iota(jnp.int32, (tm, LANES), 0)) // SLC_BLOCK
    ind = jnp.where(lane == blk + HEAD_DIM, 1.0, 0.0)
    for idx in range(6):
        c0 = NSA_Q_W + idx * NSA_KV_W
        acc = _dot(h, w_ref[:, c0:c0 + NSA_KV_W])
        if idx % 2 == 0:
            acc = _rope(acc, c, s1, s2)
        if idx < 2:
            for g in range(NSA_KV_HEADS):
                kvf_ref[0, idx * NSA_KV_HEADS + g] = acc[:, g * HEAD_DIM:(g + 1) * HEAD_DIM]
            continue
        fill = 1.0 if idx % 2 == 1 else (ind if idx == 2 else 0.0)
        for gp in range(NSA_KV_HEADS // 2):
            pair = acc[:, gp * LANES:(gp + 1) * LANES]
            swapped = pltpu.roll(pair, HEAD_DIM, 1)
            kvx_ref[0, (idx - 2) * NSA_KV_HEADS + 2 * gp] = jnp.where(low, pair, fill).astype(BF16)
            kvx_ref[0, (idx - 2) * NSA_KV_HEADS + 2 * gp + 1] = jnp.where(low, swapped, fill).astype(BF16)
    c0 = NSA_Q_W + 6 * NSA_KV_W
    gl = _dot(h, w_ref[:, c0:c0 + LANES])
    gates_ref[0] = 1.0 / (1.0 + jnp.exp(-gl))


def _odd_proj(x, g, sc, sh, w, tables):
    b, s, d = x.shape
    n = w.shape[1]
    tm = PROJ_TM
    row = lambda bi, i: (bi, i, 0)
    vec = lambda bi, i: (bi, 0, 0)
    tab = pl.BlockSpec((tm, LANES), lambda bi, i: (i, 0))
    hd = lambda bi, i: (bi, 0, i, 0)
    return pl.pallas_call(
        _odd_proj_body,
        grid=(b, s // tm),
        in_specs=[pl.BlockSpec((1, tm, d), row),
                  pl.BlockSpec((1, d), lambda bi, i: (0, 0)),
                  pl.BlockSpec((1, 1, d), vec),
                  pl.BlockSpec((1, 1, d), vec),
                  pl.BlockSpec((d, n), lambda bi, i: (0, 0)),
                  tab, tab, tab],
        out_specs=[pl.BlockSpec((1, tm, NSA_Q_W), row),
                   pl.BlockSpec((1, 2 * NSA_KV_HEADS, tm, HEAD_DIM), hd),
                   pl.BlockSpec((1, 4 * NSA_KV_HEADS, tm, LANES), hd),
                   pl.BlockSpec((1, tm, LANES), row)],
        out_shape=[jax.ShapeDtypeStruct((b, s, NSA_Q_W), BF16),
                   jax.ShapeDtypeStruct((b, 2 * NSA_KV_HEADS, s, HEAD_DIM), F32),
                   jax.ShapeDtypeStruct((b, 4 * NSA_KV_HEADS, s, LANES), BF16),
                   jax.ShapeDtypeStruct((b, s, LANES), F32)],
        compiler_params=_params("parallel", "parallel"),
        name="odd_proj",
    )(x, g, sc, sh, w, *tables)


def _compress_body(x_ref, pos_ref, w1_ref, b1_ref, w2_ref, b2_ref, o_ref):
    x = x_ref[0, 0]
    half = CMP_STRIDE * HEAD_DIM
    xa = (x + pos_ref[0, 0:1]).astype(BF16)
    xb = (x + pos_ref[0, 1:2]).astype(BF16)
    a = _dot(xa, w1_ref[0, 0:half])
    bm = _dot(xb, w1_ref[0, half:2 * half])
    nrow = x.shape[0]
    pre = (a + pltpu.roll(bm, nrow - 1, 0)) + b1_ref[0]
    hid = 0.5 * pre * (1.0 + jnp.tanh(math.sqrt(2.0 / math.pi) * (pre + 0.044715 * (pre * pre * pre))))
    o_ref[0, 0] = (_dot(hid.astype(BF16), w2_ref[0]) + b2_ref[0]).astype(o_ref.dtype)


def _compress(kvf, pos, w1, b1, w2, b2):
    b, n2, s, hd = kvf.shape
    g = n2 // 2
    nchunk = s // CMP_STRIDE
    half = CMP_STRIDE * hd
    x = kvf.reshape(b, n2, nchunk, half)
    kv = lambda bi, n: (n // g, 0, 0)
    return pl.pallas_call(
        _compress_body,
        grid=(b, n2),
        in_specs=[pl.BlockSpec((1, 1, nchunk, half), lambda bi, n: (bi, n, 0, 0)),
                  pl.BlockSpec((1, 2, half), kv),
                  pl.BlockSpec((1, 2 * half, CMP_HIDDEN), kv),
                  pl.BlockSpec((1, 1, CMP_HIDDEN), kv),
                  pl.BlockSpec((1, CMP_HIDDEN, hd), kv),
                  pl.BlockSpec((1, 1, hd), kv)],
        out_specs=pl.BlockSpec((1, 1, nchunk, hd), lambda bi, n: (bi, n, 0, 0)),
        out_shape=jax.ShapeDtypeStruct((b, n2, nchunk, hd), BF16),
        compiler_params=_params("parallel", "parallel"),
        name="nsa_compress",
    )(x, pos.reshape(2, 2, half), w1.astype(BF16), b1.reshape(2, 1, CMP_HIDDEN),
      w2.astype(BF16), b2.reshape(2, 1, hd))


def _nsa_tile(i, first_half, q_ref, gt_ref, kc, vc, ks_ref, vs_ref, kw_ref, vw_ref, ovt):
    tq = q_ref.shape[1]
    r = NSA_GROUP
    q0 = i * tq
    qf = q_ref[0]
    q4 = jnp.concatenate([qf[:, h * HEAD_DIM:(h + 1) * HEAD_DIM] for h in range(r)], axis=0)
    qpos_c = q0 + lax.broadcasted_iota(jnp.int32, (tq, 1), 0)
    qpos4 = jnp.concatenate([qpos_c] * r, axis=0)

    nc = kc.shape[0]
    s_c = _dot_nt(q4, kc)
    cmp_end = lax.broadcasted_iota(jnp.int32, (1, nc), 1) * CMP_STRIDE + (CMP_BLOCK - 1)
    s_c = jnp.where(cmp_end <= qpos4, s_c, NEG_INF)
    e_c = jnp.exp2(s_c - jnp.max(s_c, axis=1, keepdims=True))
    p_c = e_c / jnp.sum(e_c, axis=1, keepdims=True)
    p_c = jnp.where(qpos4 >= CMP_BLOCK - 1, p_c, 0.0)
    o_c = _dot(p_c.astype(BF16), vc)
    yield

    p_sum = p_c[0:tq]
    for h in range(1, r):
        p_sum = p_sum + p_c[h * tq:(h + 1) * tq]
    ps_hi, ps_lo = _split_bf16(p_sum)
    imp = _dot_nt(ovt, ps_hi) + _dot_nt(ovt, ps_lo)
    ns = imp.shape[0]
    blk = lax.broadcasted_iota(jnp.int32, (ns, tq), 0)
    qpos_r = q0 + lax.broadcasted_iota(jnp.int32, (ns, tq), 1)
    own = qpos_r // SLC_BLOCK
    started = blk * SLC_BLOCK <= qpos_r
    forced = (blk == 0) | (blk == own) | (blk == own - 1)
    imp = jnp.where(started, jnp.where(forced, FORCE_SCORE, imp), NEG_INF)
    bias_t = jnp.where(_rank_below(imp, SLC_TOPN, ns // 2 if first_half else ns), 0.0, NEG_INF)
    parts = [jnp.zeros((HEAD_DIM, tq), F32), bias_t]
    if ns < LANES - HEAD_DIM:
        parts.append(jnp.zeros((LANES - HEAD_DIM - ns, tq), F32))
    bias = jnp.transpose(jnp.concatenate(parts, axis=0)).astype(BF16)
    lane4 = lax.broadcasted_iota(jnp.int32, (r * tq, LANES), 1)
    qz = jnp.concatenate([q4, jnp.zeros_like(q4)], axis=1)
    qs = jnp.where(lane4 < HEAD_DIM, qz, jnp.concatenate([bias] * r, axis=0))
    yield

    tk = NSA_TK
    d0 = pl.multiple_of((q0 // tk) * tk, tk)
    causal = d0 + lax.broadcasted_iota(jnp.int32, (1, tk), 1) <= qpos4
    st = _online_step(qs, ks_ref[0, 0, pl.ds(d0, tk), :], vs_ref[0, 0, pl.ds(d0, tk), :], causal, None, None)
    yield

    span = WINDOW + tq
    w0 = pl.multiple_of(jnp.maximum(q0 - WINDOW, 0), tq)
    kpos_w = w0 + lax.broadcasted_iota(jnp.int32, (1, tq), 1)
    s_w = _dot_nt(qz, kw_ref[0, 0, pl.ds(w0, span), :])
    cols = [jnp.where(kpos_w > qpos4 - WINDOW, s_w[:, :tq], NEG_INF)]
    if first_half:
        cols = [jnp.where(kpos_w <= qpos4, cols[0], NEG_INF)]
        cols += [jnp.where(kpos_w + c * tq <= qpos4, s_w[:, c * tq:(c + 1) * tq], NEG_INF)
                 for c in range(1, span // tq)]
    else:
        cols += [s_w[:, tq:span - tq],
                 jnp.where(kpos_w + (span - tq) <= qpos4, s_w[:, span - tq:], NEG_INF)]
    s_w = jnp.concatenate(cols, axis=1)
    p_w = jnp.exp2(s_w - jnp.max(s_w, axis=1, keepdims=True)).astype(BF16)
    acc_w = _dot(p_w, vw_ref[0, 0, pl.ds(w0, span), :])
    o_w = acc_w[:, :HEAD_DIM] / acc_w[:, HEAD_DIM:]
    gt = gt_ref[0, 0]

    def finish(acc_s):
        o_s = acc_s[:, :HEAD_DIM] / acc_s[:, HEAD_DIM:]
        outs = []
        for h in range(r):
            sl = slice(h * tq, (h + 1) * tq)
            outs.append(gt[:, 3 * h:3 * h + 1] * o_c[sl] + gt[:, 3 * h + 1:3 * h + 2] * o_s[sl]
                        + gt[:, 3 * h + 2:3 * h + 3] * o_w[sl])
        return jnp.concatenate(outs, axis=1)

    return [qs], [st], finish


def _nsa_body(qa_ref, qb_ref, kc_ref, vc_ref, ks_ref, vs_ref, kw_ref, vw_ref, ga_ref, gb_ref, ovt_ref,
              oa_ref, ob_ref):
    a = pl.program_id(2)
    tq = qa_ref.shape[1]
    tk = NSA_TK
    nt = ks_ref.shape[2] // tq
    kc, vc, ovt = kc_ref[0, 0], vc_ref[0, 0], ovt_ref[...]
    per = tk // tq
    n_chunks = ks_ref.shape[2] // tk
    long_tile = _Staged(_nsa_tile(nt - 1 - a, False, qb_ref, gb_ref, kc, vc, ks_ref, vs_ref, kw_ref, vw_ref, ovt))
    short_tile = _Staged(_nsa_tile(a, True, qa_ref, ga_ref, kc, vc, ks_ref, vs_ref, kw_ref, vw_ref, ovt))
    acc_b, acc_a = _attn_pair(n_chunks - 1, (nt // 2 - 1) // per, a // per, tk, long_tile.result(), short_tile,
                              lambda e, k0: ks_ref[0, 0, pl.ds(k0, tk), :],
                              lambda e, k0: vs_ref[0, 0, pl.ds(k0, tk), :], late_pv=False)
    oa_ref[0] = short_tile.result()[2](acc_a[0]).astype(oa_ref.dtype)
    ob_ref[0] = long_tile.result()[2](acc_b[0]).astype(ob_ref.dtype)


def _nsa_attention(q, cmp, kvx, gates):
    b, s, _ = q.shape
    g = NSA_KV_HEADS
    tq = NSA_TQ
    nt = s // tq
    nc = cmp.shape[2]
    ns = s // SLC_BLOCK
    assert ns <= LANES - HEAD_DIM
    assert nt % (2 * NSA_TK // tq) == 0 and s // 2 >= WINDOW
    cw = NSA_GROUP * HEAD_DIM
    cs = jnp.arange(nc)[None, :] * CMP_STRIDE
    ss = jnp.arange(ns)[:, None] * SLC_BLOCK
    ovt = ((cs <= ss + SLC_BLOCK - 1) & (cs + CMP_BLOCK - 1 >= ss)).astype(BF16)
    head = lambda off: (lambda bi, gi, a: (bi, off + gi, 0, 0))
    half = jax.ShapeDtypeStruct((b, s // 2, NSA_Q_W), BF16)
    return pl.pallas_call(
        _nsa_body,
        grid=(b, g, nt // 2),
        in_specs=[pl.BlockSpec((1, tq, cw), lambda bi, gi, a: (bi, a, gi)),
                  pl.BlockSpec((1, tq, cw), lambda bi, gi, a: (bi, nt - 1 - a, gi)),
                  pl.BlockSpec((1, 1, nc, HEAD_DIM), head(0)),
                  pl.BlockSpec((1, 1, nc, HEAD_DIM), head(g)),
                  pl.BlockSpec((1, 1, s, LANES), head(0)),
                  pl.BlockSpec((1, 1, s, LANES), head(g)),
                  pl.BlockSpec((1, 1, s, LANES), head(2 * g)),
                  pl.BlockSpec((1, 1, s, LANES), head(3 * g)),
                  pl.BlockSpec((1, 1, tq, 3 * NSA_GROUP), lambda bi, gi, a: (bi, gi, a, 0)),
                  pl.BlockSpec((1, 1, tq, 3 * NSA_GROUP), lambda bi, gi, a: (bi, gi, nt - 1 - a, 0)),
                  pl.BlockSpec((ns, nc), lambda bi, gi, a: (0, 0))],
        out_specs=[pl.BlockSpec((1, tq, cw), lambda bi, gi, a: (bi, a, gi)),
                   pl.BlockSpec((1, tq, cw), lambda bi, gi, a: (bi, nt // 2 - 1 - a, gi))],
        out_shape=[half, half],
        compiler_params=_params("parallel", "parallel", "arbitrary"),
        name="nsa_attn",
    )(q, q, cmp, cmp, kvx, kvx, kvx, kvx, gates, gates, ovt)


def kernel(x, c, norm1_g, norm2_g, final_g, ada_w, ada_b, ev_w_in, ev_w_out, ev_lambda, ev_subln_g,
           od_w_in, od_w_out, od_cmp_pos, od_cmp_w1, od_cmp_b1, od_cmp_w2, od_cmp_b2,
           moe_wg, moe_bg, moe_we, moe_be, moe_w_gate, moe_w_up, moe_w_down):
    b, s, d = x.shape
    tables = _rope_tables(s)
    mod = _ada_mod(c, ada_w, ada_b)
    for l in range(DEPTH):
        sh1, sc1, g1, sh2, sc2, g2 = (mod[l, :, None, k * d:(k + 1) * d] for k in range(6))
        i = l // 2
        if l % 2 == 0:
            lambda_init = 0.8 - 0.6 * math.exp(-0.3 * l)
            qa, kx, vx, pb, kmean = _even_proj(x, norm1_g[l].reshape(1, d), sc1, sh1,
                                               ev_w_in[i].astype(BF16), tables)
            oa = _moba_attention(qa, kx, vx, kmean.reshape(b, s // MOBA_BLOCK, MOBA_W))
            ob = _diff_attention(pb, ev_lambda[i], ev_subln_g[i], lambda_init)
            mixes = (jnp.concatenate(oa, axis=1), jnp.concatenate(ob, axis=1))
            w_out = ev_w_out[i]
        else:
            w = jnp.pad(od_w_in[i], ((0, 0), (0, ODD_PAD - ODD_IN))).astype(BF16)
            q, kvf, kvx, gates = _odd_proj(x, norm1_g[l].reshape(1, d), sc1, sh1, w, tables)
            cmp = _compress(kvf, od_cmp_pos[i], od_cmp_w1[i], od_cmp_b1[i], od_cmp_w2[i], od_cmp_b2[i])
            gt = gates[:, :, :N_GATES].reshape(b, s, NSA_KV_HEADS, 3 * NSA_GROUP).transpose(0, 2, 1, 3)
            mixes = (jnp.concatenate(_nsa_attention(q, cmp, kvx, gt), axis=1),)
            w_out = od_w_out[i]
        x = _mix_moe(x, g1, mixes, w_out, norm2_g[l].reshape(1, d), sc2, sh2, g2, moe_wg[l], moe_bg[l],
                     moe_we[l], moe_be[l], moe_w_gate[l], moe_w_up[l], moe_w_down[l],
                     final_g if l == DEPTH - 1 else None)
    return x
```

```python
import functools
import math

import jax
import jax.numpy as jnp
from jax import lax
from jax.experimental import pallas as pl
from jax.experimental.pallas import tpu as pltpu

F32 = jnp.float32
BF16 = jnp.bfloat16

D_MODEL = 1024
DEPTH = 4
HEAD_DIM = 64
ROPE_DIM = HEAD_DIM // 4
ROPE_HALF = ROPE_DIM // 2
ROPE_THETA = 500000.0
NORM_EPS = 1e-6
NEG_INF = -1e30
FORCE_SCORE = 1e6
QK_SCALE = HEAD_DIM ** -0.5 * math.log2(math.e)

MOBA_HEADS = 8
MOBA_BLOCK = 256
MOBA_TOPK = 3
DIFF_HEADS = 4
MOBA_W = MOBA_HEADS * HEAD_DIM
DIFF_QK_W = DIFF_HEADS * 2 * HEAD_DIM
DIFF_V_W = DIFF_HEADS * 2 * HEAD_DIM
DIFF_W = 2 * DIFF_QK_W + DIFF_V_W
EVEN_IN = 3 * MOBA_W + DIFF_W

NSA_HEADS = 16
NSA_GROUP = 4
NSA_KV_HEADS = 4
CMP_BLOCK = 32
CMP_STRIDE = 16
CMP_HIDDEN = 256
SLC_BLOCK = 64
SLC_TOPN = 16
WINDOW = 512
NSA_Q_W = NSA_HEADS * HEAD_DIM
NSA_KV_W = NSA_KV_HEADS * HEAD_DIM
ODD_IN = NSA_Q_W + 6 * NSA_KV_W + 3 * NSA_HEADS
N_GATES = 3 * NSA_HEADS

MOE_GROUPS = 4
MOE_PER_GROUP = 4
MOE_EXPERTS = 16
MOE_FF = 256

LANES = 128
VMEM_LIMIT = 56 * 1024 * 1024

PROJ_TM = 512
ATT_TQ = 512
ATT_TK = 512
NSA_TQ = 256
NSA_TK = 512


def _params(*sem):
    return pltpu.CompilerParams(dimension_semantics=sem, vmem_limit_bytes=VMEM_LIMIT)


def _dot(a, b):
    return jnp.dot(a, b, preferred_element_type=F32)


def _dot_nt(a, b):
    return lax.dot_general(a, b, (((1,), (1,)), ((), ())), preferred_element_type=F32)


def _split_bf16(x):
    hi = x.astype(BF16)
    lo = (x - hi.astype(F32)).astype(BF16)
    return hi, lo


def _norm_mod(x, g, sc, sh):
    y = x * lax.rsqrt(jnp.mean(x * x, axis=-1, keepdims=True) + NORM_EPS)
    return (y * g) * (1.0 + sc) + sh


def _rope(t, c, s1, s2):
    w = t.shape[1]
    k = w // LANES
    cw = jnp.concatenate([c] * k, axis=1) if k > 1 else c
    s1w = jnp.concatenate([s1] * k, axis=1) if k > 1 else s1
    s2w = jnp.concatenate([s2] * k, axis=1) if k > 1 else s2
    return t * cw + pltpu.roll(t, ROPE_HALF, 1) * s1w + pltpu.roll(t, w - ROPE_HALF, 1) * s2w


def _rope_tables(seq):
    pos = jnp.arange(seq, dtype=F32)
    inv = ROPE_THETA ** (-jnp.arange(0, ROPE_DIM, 2, dtype=F32) / ROPE_DIM)
    ang = pos[:, None] * inv[None, :]
    cos, sin = jnp.cos(ang), jnp.sin(ang)
    ones = jnp.ones((seq, HEAD_DIM - ROPE_DIM), F32)
    zeros8 = jnp.zeros((seq, ROPE_HALF), F32)
    zeros = jnp.zeros((seq, HEAD_DIM - ROPE_DIM), F32)
    c = jnp.concatenate([cos, cos, ones], axis=1)
    s1 = jnp.concatenate([zeros8, sin, zeros], axis=1)
    s2 = jnp.concatenate([-sin, zeros8, zeros], axis=1)
    rep = LANES // HEAD_DIM
    return tuple(jnp.tile(t, (1, rep)) for t in (c, s1, s2))


def _rank_below(v, k, rows):
    n = v.shape[0]
    sub = 8
    groups = [v[g:g + sub] for g in range(0, n, sub)]
    cnts = [jnp.zeros(g.shape, F32) for g in groups]
    idx = lax.broadcasted_iota(jnp.int32, groups[0].shape, 0)
    for m in range(rows):
        rm = v[m:m + 1, :]
        for j, g in enumerate(groups):
            if j * sub > m:
                beat = rm >= g
            elif j * sub + sub - 1 < m:
                beat = rm > g
            else:
                beat = (rm > g) | ((rm == g) & (idx > m - j * sub))
            cnts[j] = cnts[j] + jnp.where(beat, 1.0, 0.0)
    return jnp.concatenate(cnts, axis=0) < k


def _online_step(q, k, v, mask, m, acc):
    s = _dot_nt(q, k)
    if mask is not None:
        s = jnp.where(mask, s, NEG_INF)
    m_new = jnp.max(s, axis=1, keepdims=True)
    if m is not None:
        m_new = jnp.maximum(m, m_new)
    p = jnp.exp2(s - m_new).astype(v.dtype)
    pv = _dot(p, v)
    if m is None:
        return m_new, pv
    return m_new, jnp.exp2(m - m_new) * acc + pv


class _Staged:
    def __init__(self, gen):
        self.gen, self.done, self.value = gen, False, None

    def advance(self):
        if not self.done:
            try:
                next(self.gen)
            except StopIteration as stop:
                self.done, self.value = True, stop.value

    def result(self):
        while not self.done:
            self.advance()
        return self.value


def _attn_pair(n_past, max_short, cnt_short, tk, long_tile, short_tile, key, value, late_pv):
    qs_l, st_l = long_tile[:2]
    ns = len(qs_l)
    n_static = n_past - max_short
    cnt_long = n_past - cnt_short
    q_cur = list(qs_l)
    m_cur = [st_l[e][0] for e in range(ns)]
    acc_cur = [st_l[e][1] for e in range(ns)]
    out_long = list(acc_cur)
    pend = None

    def flush():
        return [acc_cur[e] + _dot(pend[0][e], value(e, pend[1])) for e in range(ns)]

    for u in range(n_past):
        if pend is not None:
            acc_cur = flush()
        if u >= n_static:
            qs_s, st_s = short_tile.result()[:2]
            sw = u == cnt_long
            out_long = [jnp.where(sw, acc_cur[e], out_long[e]) for e in range(ns)]
            acc_cur = [jnp.where(sw, st_s[e][1], acc_cur[e]) for e in range(ns)]
            m_cur = [jnp.where(sw, st_s[e][0], m_cur[e]) for e in range(ns)]
            q_cur = [jnp.where(sw, qs_s[e], q_cur[e]) for e in range(ns)]
            chunk = jnp.where(u < cnt_long, u, u - cnt_long)
        else:
            chunk = u
        k0 = pl.multiple_of(chunk * tk, tk) if u >= n_static else chunk * tk
        if late_pv:
            ss = [_dot_nt(q_cur[e], key(e, k0)) for e in range(ns)]
            ps = []
            for e in range(ns):
                m_new = jnp.maximum(m_cur[e], jnp.max(ss[e], axis=1, keepdims=True))
                ps.append(jnp.exp2(ss[e] - m_new).astype(BF16))
                acc_cur[e] = jnp.exp2(m_cur[e] - m_new) * acc_cur[e]
                m_cur[e] = m_new
            pend = (ps, k0)
        else:
            for e in range(ns):
                m_cur[e], acc_cur[e] = _online_step(q_cur[e], key(e, k0), value(e, k0), None,
                                                    m_cur[e], acc_cur[e])
        if u < n_static:
            short_tile.advance()
    if pend is not None:
        acc_cur = flush()
    st_s = short_tile.result()[1]
    none_short = cnt_short == 0
    out_long = [jnp.where(none_short, acc_cur[e], out_long[e]) for e in range(ns)]
    out_short = [jnp.where(none_short, st_s[e][1], acc_cur[e]) for e in range(ns)]
    return out_long, out_short


def _ada_body(c_ref, w_ref, b_ref, o_ref):
    c = c_ref[...]
    cs = c * (1.0 / (1.0 + jnp.exp(-c)))
    o_ref[0] = jnp.dot(cs, w_ref[0], preferred_element_type=F32,
                       precision=lax.Precision.HIGHEST) + b_ref[0]


def _ada_mod(c, ada_w, ada_b):
    b, d = c.shape
    depth, _, n = ada_w.shape
    rows = 8
    tn = 1536
    cp = jnp.pad(c, ((0, rows - b), (0, 0)))
    out = pl.pallas_call(
        _ada_body,
        grid=(depth, n // tn),
        in_specs=[pl.BlockSpec((rows, d), lambda l, j: (0, 0)),
                  pl.BlockSpec((1, d, tn), lambda l, j: (l, 0, j)),
                  pl.BlockSpec((1, 1, tn), lambda l, j: (l, 0, j))],
        out_specs=pl.BlockSpec((1, rows, tn), lambda l, j: (l, 0, j)),
        out_shape=jax.ShapeDtypeStruct((depth, rows, n), F32),
        compiler_params=_params("parallel", "parallel"),
        name="ada_mod",
    )(cp, ada_w, ada_b.reshape(depth, 1, n))
    return out[:, :b]


def _even_proj_body(x_ref, g_ref, sc_ref, sh_ref, w_ref, c_ref, s1_ref, s2_ref,
                    q_ref, k_ref, v_ref, pb_ref, km_ref):
    tm = x_ref.shape[1]
    h = _norm_mod(x_ref[0], g_ref[...], sc_ref[0], sh_ref[0]).astype(BF16)
    c, s1, s2 = c_ref[...], s1_ref[...], s2_ref[...]
    ch = MOBA_W
    lane = lax.broadcasted_iota(jnp.int32, (tm, LANES), 1)
    low = lane < HEAD_DIM
    blk = (pl.program_id(1) * tm + lax.broadcasted_iota(jnp.int32, (tm, LANES), 0)) // MOBA_BLOCK
    ind_hi = jnp.where(lane == blk + HEAD_DIM, 1.0, 0.0)
    ind_lo = jnp.where(lane == blk, 1.0, 0.0)
    for idx, kind in enumerate(("q", "k", "v", "q", "k", "v")):
        acc = _dot(h, w_ref[:, idx * ch:(idx + 1) * ch])
        if kind != "v":
            acc = _rope(acc, c, s1, s2)
        if kind == "q":
            acc = acc * QK_SCALE
        if idx == 0:
            q_ref[0] = acc.astype(BF16)
        elif idx == 1:
            nblk = tm // MOBA_BLOCK
            km_ref[0, 0] = jnp.concatenate(
                [jnp.mean(acc[n * MOBA_BLOCK:(n + 1) * MOBA_BLOCK], axis=0, keepdims=True)
                 for n in range(nblk)], axis=0)
            for hp in range(ch // LANES):
                kp = acc[:, hp * LANES:(hp + 1) * LANES]
                k_ref[0, 2 * hp] = jnp.where(low, kp, ind_hi).astype(BF16)
                k_ref[0, 2 * hp + 1] = jnp.where(low, ind_lo, kp).astype(BF16)
        elif idx == 2:
            for hp in range(ch // LANES):
                vp = acc[:, hp * LANES:(hp + 1) * LANES]
                v_ref[0, 2 * hp] = jnp.where(low, vp, 1.0).astype(BF16)
                v_ref[0, 2 * hp + 1] = jnp.where(low, 1.0, vp).astype(BF16)
        else:
            pb_ref[0, :, (idx - 3) * ch:(idx - 2) * ch] = acc.astype(BF16)


def _even_proj(x, g, sc, sh, w, tables):
    b, s, d = x.shape
    n = w.shape[1]
    tm = PROJ_TM
    nblk = tm // MOBA_BLOCK
    row = lambda bi, i: (bi, i, 0)
    vec = lambda bi, i: (bi, 0, 0)
    hd = lambda bi, i: (bi, 0, i, 0)
    tab = pl.BlockSpec((tm, LANES), lambda bi, i: (i, 0))
    return pl.pallas_call(
        _even_proj_body,
        grid=(b, s // tm),
        in_specs=[pl.BlockSpec((1, tm, d), row),
                  pl.BlockSpec((1, d), lambda bi, i: (0, 0)),
                  pl.BlockSpec((1, 1, d), vec),
                  pl.BlockSpec((1, 1, d), vec),
                  pl.BlockSpec((d, n), lambda bi, i: (0, 0)),
                  tab, tab, tab],
        out_specs=[pl.BlockSpec((1, tm, MOBA_W), row),
                   pl.BlockSpec((1, MOBA_HEADS, tm, LANES), hd),
                   pl.BlockSpec((1, MOBA_HEADS, tm, LANES), hd),
                   pl.BlockSpec((1, tm, DIFF_W), row),
                   pl.BlockSpec((1, 1, nblk, MOBA_W), lambda bi, i: (bi, i, 0, 0))],
        out_shape=[jax.ShapeDtypeStruct((b, s, MOBA_W), BF16),
                   jax.ShapeDtypeStruct((b, MOBA_HEADS, s, LANES), BF16),
                   jax.ShapeDtypeStruct((b, MOBA_HEADS, s, LANES), BF16),
                   jax.ShapeDtypeStruct((b, s, DIFF_W), BF16),
                   jax.ShapeDtypeStruct((b, s // tm, nblk, MOBA_W), F32)],
        compiler_params=_params("parallel", "parallel"),
        name="even_proj",
    )(x, g, sc, sh, w, *tables)


def _moba_body(qa_ref, qb_ref, k_ref, v_ref, km_ref, oa_ref, ob_ref):
    a = pl.program_id(2)
    tq = qa_ref.shape[1]
    nb = km_ref.shape[1]
    tk = ATT_TK
    nt = k_ref.shape[2] // tq
    per = tk // tq
    km = km_ref[0]
    lane = lax.broadcasted_iota(jnp.int32, (tq, LANES), 1)
    lane_k = lax.broadcasted_iota(jnp.int32, (nb, LANES), 1)
    n_idx = lax.broadcasted_iota(jnp.int32, (nb, tq), 0)
    low = lane < HEAD_DIM

    def tile(i, q_ref, rows):
        q = q_ref[0]
        c0 = pl.multiple_of((i // per) * tk, tk)
        qpos = i * tq + lax.broadcasted_iota(jnp.int32, (tq, 1), 0)
        causal = c0 + lax.broadcasted_iota(jnp.int32, (1, tk), 1) <= qpos
        own_blk = (i * tq + lax.broadcasted_iota(jnp.int32, (nb, tq), 1)) // MOBA_BLOCK
        q_t, st_t = [], []
        for e in range(2):
            own = low if e == 0 else (lane >= HEAD_DIM)
            own_k = (lane_k < HEAD_DIM) if e == 0 else (lane_k >= HEAD_DIM)
            qm = jnp.where(own, q, jnp.zeros_like(q))
            km_hi, km_lo = _split_bf16(jnp.where(own_k, km, 0.0))
            gs = _dot_nt(km_hi, qm) + _dot_nt(km_lo, qm)
            gs = jnp.where(n_idx < own_blk, gs, NEG_INF)
            keep = (_rank_below(gs, MOBA_TOPK, rows) & (n_idx < own_blk)) | (n_idx == own_blk)
            bias_t = jnp.where(keep, 0.0, NEG_INF)
            off = HEAD_DIM * (1 - e)
            parts = [jnp.zeros((off, tq), F32)] if off else []
            pad = jnp.concatenate(parts + [bias_t, jnp.zeros((LANES - off - nb, tq), F32)], axis=0)
            q_t.append(jnp.where(own, q, jnp.transpose(pad).astype(BF16)))
            yield
        for e in range(2):
            st_t.append(_online_step(q_t[e], k_ref[0, e, pl.ds(c0, tk), :], v_ref[0, e, pl.ds(c0, tk), :],
                                     causal, None, None))
            yield
        return q_t, st_t

    long_tile = _Staged(tile(nt - 1 - a, qb_ref, nb)).result()
    short_tile = _Staged(tile(a, qa_ref, nb // 2))
    n_chunks = k_ref.shape[2] // tk
    acc_b, acc_a = _attn_pair(n_chunks - 1, (nt // 2 - 1) // per, a // per, tk, long_tile, short_tile,
                              lambda e, k0: k_ref[0, e, pl.ds(k0, tk), :],
                              lambda e, k0: v_ref[0, e, pl.ds(k0, tk), :], late_pv=True)
    for (acc0, acc1), o_ref in ((acc_a, oa_ref), (acc_b, ob_ref)):
        num = jnp.where(low, acc0, acc1)
        den = jnp.where(low, pltpu.roll(acc0, HEAD_DIM, 1), pltpu.roll(acc1, HEAD_DIM, 1))
        o_ref[0] = (num / den).astype(o_ref.dtype)


def _moba_attention(q, kx, vx, kmean):
    b, s, _ = q.shape
    nb = s // MOBA_BLOCK
    pairs = MOBA_W // LANES
    tq = ATT_TQ
    nt = s // tq
    assert tq % MOBA_BLOCK == 0 and ATT_TK % tq == 0 and nt % (2 * ATT_TK // tq) == 0
    half = jax.ShapeDtypeStruct((b, s // 2, MOBA_W), BF16)
    return pl.pallas_call(
        _moba_body,
        grid=(b, pairs, nt // 2),
        in_specs=[pl.BlockSpec((1, tq, LANES), lambda bi, hp, a: (bi, a, hp)),
                  pl.BlockSpec((1, tq, LANES), lambda bi, hp, a: (bi, nt - 1 - a, hp)),
                  pl.BlockSpec((1, 2, s, LANES), lambda bi, hp, a: (bi, hp, 0, 0)),
                  pl.BlockSpec((1, 2, s, LANES), lambda bi, hp, a: (bi, hp, 0, 0)),
                  pl.BlockSpec((1, nb, LANES), lambda bi, hp, a: (bi, 0, hp))],
        out_specs=[pl.BlockSpec((1, tq, LANES), lambda bi, hp, a: (bi, a, hp)),
                   pl.BlockSpec((1, tq, LANES), lambda bi, hp, a: (bi, nt // 2 - 1 - a, hp))],
        out_shape=[half, half],
        compiler_params=_params("parallel", "parallel", "arbitrary"),
        name="moba_attn",
    )(q, q, kx, vx, kmean)


def _diff_body(lam_ref, g_ref, qa_ref, qb_ref, k_ref, v_ref, oa_ref, ob_ref, *, lambda_init):
    a = pl.program_id(2)
    tq = qa_ref.shape[1]
    tk = ATT_TK
    nt = k_ref.shape[1] // tq
    per = tk // tq
    lp = lam_ref[...]
    lam = (jnp.exp(jnp.sum(lp[0:1] * lp[1:2], axis=1, keepdims=True))
           - jnp.exp(jnp.sum(lp[2:3] * lp[3:4], axis=1, keepdims=True)) + lambda_init)
    lane = lax.broadcasted_iota(jnp.int32, (tq, LANES), 1)
    ones = jnp.ones((tk, LANES), BF16)

    def key(m, k0):
        return k_ref[0, pl.ds(k0, tk), :]

    def value(m, k0):
        return jnp.concatenate([v_ref[0, pl.ds(k0, tk), :], ones], axis=1)

    def tile(i, q_ref):
        q = q_ref[0]
        q_t = [jnp.where(lane < HEAD_DIM, q, jnp.zeros_like(q)), jnp.where(lane >= HEAD_DIM, q, jnp.zeros_like(q))]
        c0 = pl.multiple_of((i // per) * tk, tk)
        qpos = i * tq + lax.broadcasted_iota(jnp.int32, (tq, 1), 0)
        causal = c0 + lax.broadcasted_iota(jnp.int32, (1, tk), 1) <= qpos
        st_t = []
        for m in range(2):
            st_t.append(_online_step(q_t[m], key(m, c0), value(m, c0), causal, None, None))
            yield
        return q_t, st_t

    long_tile = _Staged(tile(nt - 1 - a, qb_ref)).result()
    short_tile = _Staged(tile(a, qa_ref))
    n_chunks = k_ref.shape[1] // tk
    acc_b, acc_a = _attn_pair(n_chunks - 1, (nt // 2 - 1) // per, a // per, tk, long_tile, short_tile,
                              key, value, late_pv=False)
    for (a0, a1), o_ref in ((acc_a, oa_ref), (acc_b, ob_ref)):
        o = a0[:, :LANES] / a0[:, LANES:] - lam * (a1[:, :LANES] / a1[:, LANES:])
        y = o * lax.rsqrt(jnp.mean(o * o, axis=-1, keepdims=True) + NORM_EPS)
        o_ref[0] = ((y * g_ref[...]) * (1.0 - lambda_init)).astype(o_ref.dtype)


def _diff_attention(pb, lam_p, subln_g, lambda_init):
    b, s, _ = pb.shape
    tq = ATT_TQ
    nt = s // tq
    assert ATT_TK % tq == 0 and nt % (2 * ATT_TK // tq) == 0
    koff = DIFF_QK_W // LANES
    voff = 2 * koff
    half = jax.ShapeDtypeStruct((b, s // 2, DIFF_V_W), BF16)
    return pl.pallas_call(
        functools.partial(_diff_body, lambda_init=lambda_init),
        grid=(b, DIFF_HEADS, nt // 2),
        in_specs=[pl.BlockSpec((4, HEAD_DIM), lambda bi, h, a: (0, 0)),
                  pl.BlockSpec((1, LANES), lambda bi, h, a: (0, 0)),
                  pl.BlockSpec((1, tq, LANES), lambda bi, h, a: (bi, a, h)),
                  pl.BlockSpec((1, tq, LANES), lambda bi, h, a: (bi, nt - 1 - a, h)),
                  pl.BlockSpec((1, s, LANES), lambda bi, h, a: (bi, 0, koff + h)),
                  pl.BlockSpec((1, s, LANES), lambda bi, h, a: (bi, 0, voff + h))],
        out_specs=[pl.BlockSpec((1, tq, LANES), lambda bi, h, a: (bi, a, h)),
                   pl.BlockSpec((1, tq, LANES), lambda bi, h, a: (bi, nt // 2 - 1 - a, h))],
        out_shape=[half, half],
        compiler_params=_params("parallel", "parallel", "arbitrary"),
        name="diff_attn",
    )(lam_p, subln_g.reshape(1, LANES), pb, pb, pb, pb)


def _first_argmax_onehot(v, iota):
    mx = jnp.max(v, axis=1, keepdims=True)
    idx = jnp.min(jnp.where(v == mx, iota, float(v.shape[1])), axis=1, keepdims=True)
    return iota == idx, mx


def _moe_body(*refs, widths, final):
    nm = 2 * len(widths)
    x_ref, g1_ref = refs[0], refs[1]
    mix_refs = refs[2:2 + nm]
    (wo_ref, g_ref, sc_ref, sh_ref, gate_ref, wr_hi_ref, wr_lo_ref, br_ref,
     wg_ref, wu_ref, wd_ref) = refs[2 + nm:13 + nm]
    fg_ref = refs[13 + nm] if final else None
    o_ref, a_ref = refs[-2], refs[-1]
    first = pl.program_id(1) < pl.num_programs(1) // 2
    y = None
    r0 = 0
    for k, wd in enumerate(widths):
        mix = jnp.where(first, mix_refs[2 * k][0], mix_refs[2 * k + 1][0])
        t = _dot(mix, wo_ref[r0:r0 + wd, :])
        y = t if y is None else y + t
        r0 += wd
    x = x_ref[0] + g1_ref[0] * y
    h = _norm_mod(x, g_ref[...], sc_ref[0], sh_ref[0])
    h_hi, h_lo = _split_bf16(h)
    wr_hi = wr_hi_ref[...]
    r = (_dot(h_hi, wr_hi) + (_dot(h_lo, wr_hi) + _dot(h_hi, wr_lo_ref[...]))) + br_ref[...]
    tm = x.shape[0]
    gl = r[:, 0:MOE_GROUPS]
    iota = lax.broadcasted_iota(jnp.int32, (tm, MOE_GROUPS), 1).astype(F32)
    g_oh, g_mx = _first_argmax_onehot(gl, iota)
    gw = 1.0 / jnp.sum(jnp.exp(gl - g_mx), axis=1, keepdims=True)
    el_g = jnp.zeros((tm, MOE_PER_GROUP), F32)
    for g in range(MOE_GROUPS):
        lo = MOE_GROUPS + g * MOE_PER_GROUP
        el_g = el_g + jnp.where(g_oh[:, g:g + 1], r[:, lo:lo + MOE_PER_GROUP], 0.0)
    oh1, v1 = _first_argmax_onehot(el_g, iota)
    oh2, v2 = _first_argmax_onehot(jnp.where(oh1, -jnp.inf, el_g), iota)
    e2 = jnp.exp(v2 - v1)
    den = 1.0 + e2
    w_grp = jnp.where(oh1, 1.0 / den, 0.0) + jnp.where(oh2, e2 / den, 0.0)
    gsc = jnp.where(g_oh, gw, 0.0)

    hb = h_hi
    for e in range(MOE_EXPERTS):
        g, k = divmod(e, MOE_PER_GROUP)
        comb = gsc[:, g:g + 1] * w_grp[:, k:k + 1]
        gt = _dot(hb, wg_ref[e])
        up = _dot(hb, wu_ref[e])
        a = ((gt * (1.0 / (1.0 + jnp.exp(-gt)))) * up) * comb
        a_ref[:, e * MOE_FF:(e + 1) * MOE_FF] = a.astype(BF16)
    out = x + gate_ref[0] * _dot(a_ref[...], wd_ref[...])
    if final:
        out = (out * lax.rsqrt(jnp.mean(out * out, axis=-1, keepdims=True) + NORM_EPS)) * fg_ref[...]
    o_ref[0] = out


def _mix_moe(x, g1, mixes, w_out, g, sc, sh, gate, wg, bg, we, be, w_gate, w_up, w_down, final_g):
    b, s, d = x.shape
    tm = PROJ_TM
    nr = MOE_GROUPS + MOE_EXPERTS
    wr = jnp.pad(jnp.concatenate([wg, we], axis=1), ((0, 0), (0, LANES - nr)))
    br = jnp.pad(jnp.concatenate([bg, be], axis=0), (0, LANES - nr)).reshape(1, LANES)
    wr_hi, wr_lo = _split_bf16(wr)
    widths = tuple(m[0].shape[2] for m in mixes)
    nh = s // tm // 2
    final = final_g is not None
    row = lambda bi, i: (bi, i, 0)
    vec = lambda bi, i: (bi, 0, 0)
    const2 = lambda bi, i: (0, 0)
    const3 = lambda bi, i: (0, 0, 0)
    once = pl.Buffered(1)
    in_specs = ([pl.BlockSpec((1, tm, d), row), pl.BlockSpec((1, 1, d), vec)]
                + [spec for wd in widths for spec in
                   (pl.BlockSpec((1, tm, wd), lambda bi, i: (bi, jnp.minimum(i, nh - 1), 0)),
                    pl.BlockSpec((1, tm, wd), lambda bi, i: (bi, jnp.maximum(i - nh, 0), 0)))]
                + [pl.BlockSpec(w_out.shape, const2, pipeline_mode=once),
                   pl.BlockSpec((1, d), const2),
                   pl.BlockSpec((1, 1, d), vec),
                   pl.BlockSpec((1, 1, d), vec),
                   pl.BlockSpec((1, 1, d), vec),
                   pl.BlockSpec((d, LANES), const2),
                   pl.BlockSpec((d, LANES), const2),
                   pl.BlockSpec((1, LANES), const2),
                   pl.BlockSpec((MOE_EXPERTS, d, MOE_FF), const3, pipeline_mode=once),
                   pl.BlockSpec((MOE_EXPERTS, d, MOE_FF), const3, pipeline_mode=once),
                   pl.BlockSpec((MOE_EXPERTS * MOE_FF, d), const2, pipeline_mode=once)])
    args = [x, g1, *[half for m in mixes for half in m], w_out.astype(BF16), g, sc, sh, gate, wr_hi, wr_lo, br,
            w_gate.astype(BF16), w_up.astype(BF16), w_down.astype(BF16).reshape(MOE_EXPERTS * MOE_FF, d)]
    if final:
        in_specs.append(pl.BlockSpec((1, d), const2))
        args.append(final_g.reshape(1, d))
    return pl.pallas_call(
        functools.partial(_moe_body, widths=widths, final=final),
        grid=(b, s // tm),
        in_specs=in_specs,
        out_specs=pl.BlockSpec((1, tm, d), row),
        out_shape=jax.ShapeDtypeStruct((b, s, d), F32),
        scratch_shapes=[pltpu.VMEM((tm, MOE_EXPERTS * MOE_FF), BF16)],
        compiler_params=_params("parallel", "parallel"),
        name="mix_moe",
    )(*args)


ODD_PAD = 2688


def _odd_proj_body(x_ref, g_ref, sc_ref, sh_ref, w_ref, c_ref, s1_ref, s2_ref,
                   q_ref, kvf_ref, kvx_ref, gates_ref):
    tm = x_ref.shape[1]
    h = _norm_mod(x_ref[0], g_ref[...], sc_ref[0], sh_ref[0]).astype(BF16)
    c, s1, s2 = c_ref[...], s1_ref[...], s2_ref[...]
    half = NSA_Q_W // 2
    for idx in range(2):
        acc = _dot(h, w_ref[:, idx * half:(idx + 1) * half])
        q_ref[0, :, idx * half:(idx + 1) * half] = (_rope(acc, c, s1, s2) * QK_SCALE).astype(BF16)
    lane = lax.broadcasted_iota(jnp.int32, (tm, LANES), 1)
    low = lane < HEAD_DIM
    blk = (pl.program_id(1) * tm + lax.broadcasted_iota(jnp.int32, (tm, LANES), 0)) // SLC_BLOCK
    ind = jnp.where(lane == blk + HEAD_DIM, 1.0, 0.0)
    for idx in range(6):
        c0 = NSA_Q_W + idx * NSA_KV_W
        acc = _dot(h, w_ref[:, c0:c0 + NSA_KV_W])
        if idx % 2 == 0:
            acc = _rope(acc, c, s1, s2)
        if idx < 2:
            for g in range(NSA_KV_HEADS):
                kvf_ref[0, idx * NSA_KV_HEADS + g] = acc[:, g * HEAD_DIM:(g + 1) * HEAD_DIM]
            continue
        fill = 1.0 if idx % 2 == 1 else (ind if idx == 2 else 0.0)
        for gp in range(NSA_KV_HEADS // 2):
            pair = acc[:, gp * LANES:(gp + 1) * LANES]
            swapped = pltpu.roll(pair, HEAD_DIM, 1)
            kvx_ref[0, (idx - 2) * NSA_KV_HEADS + 2 * gp] = jnp.where(low, pair, fill).astype(BF16)
            kvx_ref[0, (idx - 2) * NSA_KV_HEADS + 2 * gp + 1] = jnp.where(low, swapped, fill).astype(BF16)
    c0 = NSA_Q_W + 6 * NSA_KV_W
    gl = _dot(h, w_ref[:, c0:c0 + LANES])
    gates = 1.0 / (1.0 + jnp.exp(-gl))
    per_group = 3 * NSA_GROUP
    for g in range(NSA_KV_HEADS):
        gates_ref[0, g] = pltpu.roll(gates, (LANES - g * per_group) % LANES, 1)


def _odd_proj(x, g, sc, sh, w, tables):
    b, s, d = x.shape
    n = w.shape[1]
    tm = PROJ_TM
    row = lambda bi, i: (bi, i, 0)
    vec = lambda bi, i: (bi, 0, 0)
    tab = pl.BlockSpec((tm, LANES), lambda bi, i: (i, 0))
    hd = lambda bi, i: (bi, 0, i, 0)
    return pl.pallas_call(
        _odd_proj_body,
        grid=(b, s // tm),
        in_specs=[pl.BlockSpec((1, tm, d), row),
                  pl.BlockSpec((1, d), lambda bi, i: (0, 0)),
                  pl.BlockSpec((1, 1, d), vec),
                  pl.BlockSpec((1, 1, d), vec),
                  pl.BlockSpec((d, n), lambda bi, i: (0, 0)),
                  tab, tab, tab],
        out_specs=[pl.BlockSpec((1, tm, NSA_Q_W), row),
                   pl.BlockSpec((1, 2 * NSA_KV_HEADS, tm, HEAD_DIM), hd),
                   pl.BlockSpec((1, 4 * NSA_KV_HEADS, tm, LANES), hd),
                   pl.BlockSpec((1, NSA_KV_HEADS, tm, LANES), hd)],
        out_shape=[jax.ShapeDtypeStruct((b, s, NSA_Q_W), BF16),
                   jax.ShapeDtypeStruct((b, 2 * NSA_KV_HEADS, s, HEAD_DIM), F32),
                   jax.ShapeDtypeStruct((b, 4 * NSA_KV_HEADS, s, LANES), BF16),
                   jax.ShapeDtypeStruct((b, NSA_KV_HEADS, s, LANES), F32)],
        compiler_params=_params("parallel", "parallel"),
        name="odd_proj",
    )(x, g, sc, sh, w, *tables)


def _compress_body(x_ref, pos_ref, w1_ref, b1_ref, w2_ref, b2_ref, o_ref):
    nrow = x_ref.shape[2] // CMP_STRIDE
    x = jnp.concatenate([x_ref[0, 0, pl.ds(l, nrow, stride=CMP_STRIDE), :] for l in range(CMP_STRIDE)], axis=1)
    half = CMP_STRIDE * HEAD_DIM
    xa = (x + pos_ref[0, 0:1]).astype(BF16)
    xb = (x + pos_ref[0, 1:2]).astype(BF16)
    a = _dot(xa, w1_ref[0, 0:half])
    bm = _dot(xb, w1_ref[0, half:2 * half])
    pre = (a + pltpu.roll(bm, nrow - 1, 0)) + b1_ref[0]
    hid = 0.5 * pre * (1.0 + jnp.tanh(math.sqrt(2.0 / math.pi) * (pre + 0.044715 * (pre * pre * pre))))
    o_ref[0, 0] = (_dot(hid.astype(BF16), w2_ref[0]) + b2_ref[0]).astype(o_ref.dtype)


def _compress(kvf, pos, w1, b1, w2, b2):
    b, n2, s, hd = kvf.shape
    g = n2 // 2
    nchunk = s // CMP_STRIDE
    half = CMP_STRIDE * hd
    kv = lambda bi, n: (n // g, 0, 0)
    return pl.pallas_call(
        _compress_body,
        grid=(b, n2),
        in_specs=[pl.BlockSpec((1, 1, s, hd), lambda bi, n: (bi, n, 0, 0)),
                  pl.BlockSpec((1, 2, half), kv),
                  pl.BlockSpec((1, 2 * half, CMP_HIDDEN), kv),
                  pl.BlockSpec((1, 1, CMP_HIDDEN), kv),
                  pl.BlockSpec((1, CMP_HIDDEN, hd), kv),
                  pl.BlockSpec((1, 1, hd), kv)],
        out_specs=pl.BlockSpec((1, 1, nchunk, hd), lambda bi, n: (bi, n, 0, 0)),
        out_shape=jax.ShapeDtypeStruct((b, n2, nchunk, hd), BF16),
        compiler_params=_params("parallel", "parallel"),
        name="nsa_compress",
    )(kvf, pos.reshape(2, 2, half), w1.astype(BF16), b1.reshape(2, 1, CMP_HIDDEN),
      w2.astype(BF16), b2.reshape(2, 1, hd))


def _nsa_tile(i, first_half, q_ref, gt_ref, kc, vc, ks_ref, vs_ref, kw_ref, vw_ref, ovt):
    tq = q_ref.shape[1]
    r = NSA_GROUP
    q0 = i * tq
    qf = q_ref[0]
    q4 = jnp.concatenate([qf[:, h * HEAD_DIM:(h + 1) * HEAD_DIM] for h in range(r)], axis=0)
    qpos_c = q0 + lax.broadcasted_iota(jnp.int32, (tq, 1), 0)
    qpos4 = jnp.concatenate([qpos_c] * r, axis=0)

    if first_half:
        half_nc = kc.shape[0] // 2
        kc, vc, ovt = kc[:half_nc], vc[:half_nc], ovt[:, :half_nc]
    nc = kc.shape[0]
    s_c = _dot_nt(q4, kc)
    cmp_end = lax.broadcasted_iota(jnp.int32, (1, nc), 1) * CMP_STRIDE + (CMP_BLOCK - 1)
    s_c = jnp.where(cmp_end <= qpos4, s_c, NEG_INF)
    e_c = jnp.exp2(s_c - jnp.max(s_c, axis=1, keepdims=True))
    p_c = e_c / jnp.sum(e_c, axis=1, keepdims=True)
    p_c = jnp.where(qpos4 >= CMP_BLOCK - 1, p_c, 0.0)
    o_c = _dot(p_c.astype(BF16), vc)
    yield

    p_sum = p_c[0:tq]
    for h in range(1, r):
        p_sum = p_sum + p_c[h * tq:(h + 1) * tq]
    ps_hi, ps_lo = _split_bf16(p_sum)
    imp = _dot_nt(ovt, ps_hi) + _dot_nt(ovt, ps_lo)
    ns = imp.shape[0]
    blk = lax.broadcasted_iota(jnp.int32, (ns, tq), 0)
    qpos_r = q0 + lax.broadcasted_iota(jnp.int32, (ns, tq), 1)
    own = qpos_r // SLC_BLOCK
    started = blk * SLC_BLOCK <= qpos_r
    forced = (blk == 0) | (blk == own) | (blk == own - 1)
    imp = jnp.where(started, jnp.where(forced, FORCE_SCORE, imp), NEG_INF)
    bias_t = jnp.where(_rank_below(imp, SLC_TOPN, ns // 2 if first_half else ns), 0.0, NEG_INF)
    parts = [jnp.zeros((HEAD_DIM, tq), F32), bias_t]
    if ns < LANES - HEAD_DIM:
        parts.append(jnp.zeros((LANES - HEAD_DIM - ns, tq), F32))
    bias = jnp.transpose(jnp.concatenate(parts, axis=0)).astype(BF16)
    lane4 = lax.broadcasted_iota(jnp.int32, (r * tq, LANES), 1)
    qz = jnp.concatenate([q4, jnp.zeros_like(q4)], axis=1)
    qs = jnp.where(lane4 < HEAD_DIM, qz, jnp.concatenate([bias] * r, axis=0))
    yield

    tk = NSA_TK
    d0 = pl.multiple_of((q0 // tk) * tk, tk)
    causal = d0 + lax.broadcasted_iota(jnp.int32, (1, tk), 1) <= qpos4
    st = _online_step(qs, ks_ref[0, 0, pl.ds(d0, tk), :], vs_ref[0, 0, pl.ds(d0, tk), :], causal, None, None)
    yield

    span = WINDOW + tq
    w0 = pl.multiple_of(jnp.maximum(q0 - WINDOW, 0), tq)
    kpos_w = w0 + lax.broadcasted_iota(jnp.int32, (1, tq), 1)
    s_w = _dot_nt(qz, kw_ref[0, 0, pl.ds(w0, span), :])
    cols = [jnp.where(kpos_w > qpos4 - WINDOW, s_w[:, :tq], NEG_INF)]
    if first_half:
        cols = [jnp.where(kpos_w <= qpos4, cols[0], NEG_INF)]
        cols += [jnp.where(kpos_w + c * tq <= qpos4, s_w[:, c * tq:(c + 1) * tq], NEG_INF)
                 for c in range(1, span // tq)]
    else:
        cols += [s_w[:, tq:span - tq],
                 jnp.where(kpos_w + (span - tq) <= qpos4, s_w[:, span - tq:], NEG_INF)]
    s_w = jnp.concatenate(cols, axis=1)
    p_w = jnp.exp2(s_w - jnp.max(s_w, axis=1, keepdims=True)).astype(BF16)
    acc_w = _dot(p_w, vw_ref[0, 0, pl.ds(w0, span), :])
    o_w = acc_w[:, :HEAD_DIM] / acc_w[:, HEAD_DIM:]
    gt = gt_ref[0, 0]

    def finish(acc_s):
        o_s = acc_s[:, :HEAD_DIM] / acc_s[:, HEAD_DIM:]
        outs = []
        for h in range(r):
            sl = slice(h * tq, (h + 1) * tq)
            outs.append(gt[:, 3 * h:3 * h + 1] * o_c[sl] + gt[:, 3 * h + 1:3 * h + 2] * o_s[sl]
                        + gt[:, 3 * h + 2:3 * h + 3] * o_w[sl])
        return jnp.concatenate(outs, axis=1)

    return [qs], [st], finish


def _nsa_body(qa_ref, qb_ref, kc_ref, vc_ref, ks_ref, vs_ref, kw_ref, vw_ref, ga_ref, gb_ref, ovt_ref,
              oa_ref, ob_ref):
    a = pl.program_id(2)
    tq = qa_ref.shape[1]
    tk = NSA_TK
    nt = ks_ref.shape[2] // tq
    kc, vc, ovt = kc_ref[0, 0], vc_ref[0, 0], ovt_ref[...]
    per = tk // tq
    n_chunks = ks_ref.shape[2] // tk
    long_tile = _Staged(_nsa_tile(nt - 1 - a, False, qb_ref, gb_ref, kc, vc, ks_ref, vs_ref, kw_ref, vw_ref, ovt))
    short_tile = _Staged(_nsa_tile(a, True, qa_ref, ga_ref, kc, vc, ks_ref, vs_ref, kw_ref, vw_ref, ovt))
    acc_b, acc_a = _attn_pair(n_chunks - 1, (nt // 2 - 1) // per, a // per, tk, long_tile.result(), short_tile,
                              lambda e, k0: ks_ref[0, 0, pl.ds(k0, tk), :],
                              lambda e, k0: vs_ref[0, 0, pl.ds(k0, tk), :], late_pv=False)
    oa_ref[0] = short_tile.result()[2](acc_a[0]).astype(oa_ref.dtype)
    ob_ref[0] = long_tile.result()[2](acc_b[0]).astype(ob_ref.dtype)


def _nsa_attention(q, cmp, kvx, gates):
    b, s, _ = q.shape
    g = NSA_KV_HEADS
    tq = NSA_TQ
    nt = s // tq
    nc = cmp.shape[2]
    ns = s // SLC_BLOCK
    assert ns <= LANES - HEAD_DIM
    assert nt % (2 * NSA_TK // tq) == 0 and s // 2 >= WINDOW
    cw = NSA_GROUP * HEAD_DIM
    cs = jnp.arange(nc)[None, :] * CMP_STRIDE
    ss = jnp.arange(ns)[:, None] * SLC_BLOCK
    ovt = ((cs <= ss + SLC_BLOCK - 1) & (cs + CMP_BLOCK - 1 >= ss)).astype(BF16)
    head = lambda off: (lambda bi, gi, a: (bi, off + gi, 0, 0))
    half = jax.ShapeDtypeStruct((b, s // 2, NSA_Q_W), BF16)
    return pl.pallas_call(
        _nsa_body,
        grid=(b, g, nt // 2),
        in_specs=[pl.BlockSpec((1, tq, cw), lambda bi, gi, a: (bi, a, gi)),
                  pl.BlockSpec((1, tq, cw), lambda bi, gi, a: (bi, nt - 1 - a, gi)),
                  pl.BlockSpec((1, 1, nc, HEAD_DIM), head(0)),
                  pl.BlockSpec((1, 1, nc, HEAD_DIM), head(g)),
                  pl.BlockSpec((1, 1, s, LANES), head(0)),
                  pl.BlockSpec((1, 1, s, LANES), head(g)),
                  pl.BlockSpec((1, 1, s, LANES), head(2 * g)),
                  pl.BlockSpec((1, 1, s, LANES), head(3 * g)),
                  pl.BlockSpec((1, 1, tq, LANES), lambda bi, gi, a: (bi, gi, a, 0)),
                  pl.BlockSpec((1, 1, tq, LANES), lambda bi, gi, a: (bi, gi, nt - 1 - a, 0)),
                  pl.BlockSpec((ns, nc), lambda bi, gi, a: (0, 0))],
        out_specs=[pl.BlockSpec((1, tq, cw), lambda bi, gi, a: (bi, a, gi)),
                   pl.BlockSpec((1, tq, cw), lambda bi, gi, a: (bi, nt // 2 - 1 - a, gi))],
        out_shape=[half, half],
        compiler_params=_params("parallel", "parallel", "arbitrary"),
        name="nsa_attn",
    )(q, q, cmp, cmp, kvx, kvx, kvx, kvx, gates, gates, ovt)


def kernel(x, c, norm1_g, norm2_g, final_g, ada_w, ada_b, ev_w_in, ev_w_out, ev_lambda, ev_subln_g,
           od_w_in, od_w_out, od_cmp_pos, od_cmp_w1, od_cmp_b1, od_cmp_w2, od_cmp_b2,
           moe_wg, moe_bg, moe_we, moe_be, moe_w_gate, moe_w_up, moe_w_down):
    b, s, d = x.shape
    tables = _rope_tables(s)
    mod = _ada_mod(c, ada_w, ada_b)
    for l in range(DEPTH):
        sh1, sc1, g1, sh2, sc2, g2 = (mod[l, :, None, k * d:(k + 1) * d] for k in range(6))
        i = l // 2
        if l % 2 == 0:
            lambda_init = 0.8 - 0.6 * math.exp(-0.3 * l)
            qa, kx, vx, pb, kmean = _even_proj(x, norm1_g[l].reshape(1, d), sc1, sh1,
                                               ev_w_in[i].astype(BF16), tables)
            oa = _moba_attention(qa, kx, vx, kmean.reshape(b, s // MOBA_BLOCK, MOBA_W))
            ob = _diff_attention(pb, ev_lambda[i], ev_subln_g[i], lambda_init)
            mixes = (oa, ob)
            w_out = ev_w_out[i]
        else:
            w = jnp.pad(od_w_in[i], ((0, 0), (0, ODD_PAD - ODD_IN))).astype(BF16)
            q, kvf, kvx, gates = _odd_proj(x, norm1_g[l].reshape(1, d), sc1, sh1, w, tables)
            cmp = _compress(kvf, od_cmp_pos[i], od_cmp_w1[i], od_cmp_b1[i], od_cmp_w2[i], od_cmp_b2[i])
            mixes = (_nsa_attention(q, cmp, kvx, gates),)
            w_out = od_w_out[i]
        x = _mix_moe(x, g1, mixes, w_out, norm2_g[l].reshape(1, d), sc2, sh2, g2, moe_wg[l], moe_bg[l],
                     moe_we[l], moe_be[l], moe_w_gate[l], moe_w_up[l], moe_w_down[l],
                     final_g if l == DEPTH - 1 else None)
    return x
```

```python
import functools
import math

import jax
import jax.numpy as jnp
from jax import lax
from jax.experimental import pallas as pl
from jax.experimental.pallas import tpu as pltpu

F32 = jnp.float32
BF16 = jnp.bfloat16

DEPTH = 4
HEAD_DIM = 64
ROPE_DIM = HEAD_DIM // 4
ROPE_HALF = ROPE_DIM // 2
ROPE_THETA = 500000.0
NORM_EPS = 1e-6
NEG_INF = -1e30
FORCE_SCORE = 1e6
QK_SCALE = HEAD_DIM ** -0.5 * math.log2(math.e)

MOBA_HEADS = 8
MOBA_BLOCK = 256
MOBA_TOPK = 3
DIFF_HEADS = 4
MOBA_W = MOBA_HEADS * HEAD_DIM
DIFF_QK_W = DIFF_HEADS * 2 * HEAD_DIM
DIFF_V_W = DIFF_HEADS * 2 * HEAD_DIM
DIFF_W = 2 * DIFF_QK_W + DIFF_V_W

NSA_HEADS = 16
NSA_GROUP = 4
NSA_KV_HEADS = 4
CMP_BLOCK = 32
CMP_STRIDE = 16
CMP_HIDDEN = 256
SLC_BLOCK = 64
SLC_TOPN = 16
WINDOW = 512
NSA_Q_W = NSA_HEADS * HEAD_DIM
NSA_KV_W = NSA_KV_HEADS * HEAD_DIM
ODD_IN = NSA_Q_W + 6 * NSA_KV_W + 3 * NSA_HEADS

MOE_GROUPS = 4
MOE_PER_GROUP = 4
MOE_EXPERTS = 16
MOE_FF = 256

LANES = 128
VMEM_LIMIT = 56 * 1024 * 1024

PROJ_TM = 512
ATT_TQ = 512
ATT_TK = 512
NSA_TQ = 256
NSA_TK = 512


def _params(*sem):
    return pltpu.CompilerParams(dimension_semantics=sem, vmem_limit_bytes=VMEM_LIMIT)


def _dot(a, b):
    return jnp.dot(a, b, preferred_element_type=F32)


def _dot_nt(a, b):
    return lax.dot_general(a, b, (((1,), (1,)), ((), ())), preferred_element_type=F32)


def _split_bf16(x):
    hi = x.astype(BF16)
    lo = (x - hi.astype(F32)).astype(BF16)
    return hi, lo


def _norm_mod(x, g, sc, sh):
    y = x * lax.rsqrt(jnp.mean(x * x, axis=-1, keepdims=True) + NORM_EPS)
    return (y * g) * (1.0 + sc) + sh


def _rope(t, c, s1, s2):
    w = t.shape[1]
    k = w // LANES
    cw = jnp.concatenate([c] * k, axis=1) if k > 1 else c
    s1w = jnp.concatenate([s1] * k, axis=1) if k > 1 else s1
    s2w = jnp.concatenate([s2] * k, axis=1) if k > 1 else s2
    return t * cw + pltpu.roll(t, ROPE_HALF, 1) * s1w + pltpu.roll(t, w - ROPE_HALF, 1) * s2w


def _rope_tables(seq):
    pos = jnp.arange(seq, dtype=F32)
    inv = ROPE_THETA ** (-jnp.arange(0, ROPE_DIM, 2, dtype=F32) / ROPE_DIM)
    ang = pos[:, None] * inv[None, :]
    cos, sin = jnp.cos(ang), jnp.sin(ang)
    ones = jnp.ones((seq, HEAD_DIM - ROPE_DIM), F32)
    zeros8 = jnp.zeros((seq, ROPE_HALF), F32)
    zeros = jnp.zeros((seq, HEAD_DIM - ROPE_DIM), F32)
    c = jnp.concatenate([cos, cos, ones], axis=1)
    s1 = jnp.concatenate([zeros8, sin, zeros], axis=1)
    s2 = jnp.concatenate([-sin, zeros8, zeros], axis=1)
    rep = LANES // HEAD_DIM
    return tuple(jnp.tile(t, (1, rep)) for t in (c, s1, s2))


def _rank_below(v, k, rows):
    n = v.shape[0]
    sub = 8
    groups = [v[g:g + sub] for g in range(0, n, sub)]
    cnts = [jnp.zeros(g.shape, F32) for g in groups]
    idx = lax.broadcasted_iota(jnp.int32, groups[0].shape, 0)
    for m in range(rows):
        rm = v[m:m + 1, :]
        for j, g in enumerate(groups):
            if j * sub > m:
                beat = rm >= g
            elif j * sub + sub - 1 < m:
                beat = rm > g
            else:
                beat = (rm > g) | ((rm == g) & (idx > m - j * sub))
            cnts[j] = cnts[j] + jnp.where(beat, 1.0, 0.0)
    return jnp.concatenate(cnts, axis=0) < k


def _online_step(q, k, v, mask, m, acc):
    s = _dot_nt(q, k)
    if mask is not None:
        s = jnp.where(mask, s, NEG_INF)
    m_new = jnp.max(s, axis=1, keepdims=True)
    if m is not None:
        m_new = jnp.maximum(m, m_new)
    p = jnp.exp2(s - m_new).astype(v.dtype)
    pv = _dot(p, v)
    if m is None:
        return m_new, pv
    return m_new, jnp.exp2(m - m_new) * acc + pv


class _Staged:
    def __init__(self, gen):
        self.gen, self.done, self.value = gen, False, None

    def advance(self):
        if not self.done:
            try:
                next(self.gen)
            except StopIteration as stop:
                self.done, self.value = True, stop.value

    def result(self):
        while not self.done:
            self.advance()
        return self.value


def _attn_pair(n_past, max_short, cnt_short, tk, long_tile, short_tile, key, value, late_pv):
    qs_l, st_l = long_tile[:2]
    ns = len(qs_l)
    n_static = n_past - max_short
    cnt_long = n_past - cnt_short
    q_cur = list(qs_l)
    m_cur = [st_l[e][0] for e in range(ns)]
    acc_cur = [st_l[e][1] for e in range(ns)]
    out_long = list(acc_cur)
    pend = None

    def flush():
        return [acc_cur[e] + _dot(pend[0][e], value(e, pend[1])) for e in range(ns)]

    for u in range(n_past):
        if pend is not None:
            acc_cur = flush()
        if u >= n_static:
            qs_s, st_s = short_tile.result()[:2]
            sw = u == cnt_long
            out_long = [jnp.where(sw, acc_cur[e], out_long[e]) for e in range(ns)]
            acc_cur = [jnp.where(sw, st_s[e][1], acc_cur[e]) for e in range(ns)]
            m_cur = [jnp.where(sw, st_s[e][0], m_cur[e]) for e in range(ns)]
            q_cur = [jnp.where(sw, qs_s[e], q_cur[e]) for e in range(ns)]
            chunk = jnp.where(u < cnt_long, u, u - cnt_long)
        else:
            chunk = u
        k0 = pl.multiple_of(chunk * tk, tk) if u >= n_static else chunk * tk
        if late_pv:
            ss = [_dot_nt(q_cur[e], key(e, k0)) for e in range(ns)]
            ps = []
            for e in range(ns):
                m_new = jnp.maximum(m_cur[e], jnp.max(ss[e], axis=1, keepdims=True))
                ps.append(jnp.exp2(ss[e] - m_new).astype(BF16))
                acc_cur[e] = jnp.exp2(m_cur[e] - m_new) * acc_cur[e]
                m_cur[e] = m_new
            pend = (ps, k0)
        else:
            for e in range(ns):
                m_cur[e], acc_cur[e] = _online_step(q_cur[e], key(e, k0), value(e, k0), None,
                                                    m_cur[e], acc_cur[e])
        if u < n_static:
            short_tile.advance()
    if pend is not None:
        acc_cur = flush()
    st_s = short_tile.result()[1]
    none_short = cnt_short == 0
    out_long = [jnp.where(none_short, acc_cur[e], out_long[e]) for e in range(ns)]
    out_short = [jnp.where(none_short, st_s[e][1], acc_cur[e]) for e in range(ns)]
    return out_long, out_short


def _ada_body(c_ref, w_ref, b_ref, o_ref):
    c = c_ref[...]
    cs = c * (1.0 / (1.0 + jnp.exp(-c)))
    o_ref[0] = jnp.dot(cs, w_ref[0], preferred_element_type=F32,
                       precision=lax.Precision.HIGHEST) + b_ref[0]


def _ada_mod(c, ada_w, ada_b):
    b, d = c.shape
    depth, _, n = ada_w.shape
    rows = 8
    tn = 1536
    cp = jnp.pad(c, ((0, rows - b), (0, 0)))
    out = pl.pallas_call(
        _ada_body,
        grid=(depth, n // tn),
        in_specs=[pl.BlockSpec((rows, d), lambda l, j: (0, 0)),
                  pl.BlockSpec((1, d, tn), lambda l, j: (l, 0, j)),
                  pl.BlockSpec((1, 1, tn), lambda l, j: (l, 0, j))],
        out_specs=pl.BlockSpec((1, rows, tn), lambda l, j: (l, 0, j)),
        out_shape=jax.ShapeDtypeStruct((depth, rows, n), F32),
        compiler_params=_params("parallel", "parallel"),
        name="ada_mod",
    )(cp, ada_w, ada_b.reshape(depth, 1, n))
    return out[:, :b]


def _even_proj_body(x_ref, g_ref, sc_ref, sh_ref, w_ref, c_ref, s1_ref, s2_ref,
                    q_ref, k_ref, v_ref, pb_ref, km_ref):
    tm = x_ref.shape[1]
    h = _norm_mod(x_ref[0], g_ref[...], sc_ref[0], sh_ref[0]).astype(BF16)
    c, s1, s2 = c_ref[...], s1_ref[...], s2_ref[...]
    ch = MOBA_W
    lane = lax.broadcasted_iota(jnp.int32, (tm, LANES), 1)
    low = lane < HEAD_DIM
    blk = (pl.program_id(1) * tm + lax.broadcasted_iota(jnp.int32, (tm, LANES), 0)) // MOBA_BLOCK
    ind_hi = jnp.where(lane == blk + HEAD_DIM, 1.0, 0.0)
    ind_lo = jnp.where(lane == blk, 1.0, 0.0)
    for idx, kind in enumerate(("q", "k", "v", "q", "k", "v")):
        acc = _dot(h, w_ref[:, idx * ch:(idx + 1) * ch])
        if kind != "v":
            acc = _rope(acc, c, s1, s2)
        if kind == "q":
            acc = acc * QK_SCALE
        if idx == 0:
            q_ref[0] = acc.astype(BF16)
        elif idx == 1:
            nblk = tm // MOBA_BLOCK
            km_ref[0, 0] = jnp.concatenate(
                [jnp.mean(acc[n * MOBA_BLOCK:(n + 1) * MOBA_BLOCK], axis=0, keepdims=True)
                 for n in range(nblk)], axis=0)
            for hp in range(ch // LANES):
                kp = acc[:, hp * LANES:(hp + 1) * LANES]
                k_ref[0, 2 * hp] = jnp.where(low, kp, ind_hi).astype(BF16)
                k_ref[0, 2 * hp + 1] = jnp.where(low, ind_lo, kp).astype(BF16)
        elif idx == 2:
            for hp in range(ch // LANES):
                vp = acc[:, hp * LANES:(hp + 1) * LANES]
                v_ref[0, 2 * hp] = jnp.where(low, vp, 1.0).astype(BF16)
                v_ref[0, 2 * hp + 1] = jnp.where(low, 1.0, vp).astype(BF16)
        else:
            pb_ref[0, :, (idx - 3) * ch:(idx - 2) * ch] = acc.astype(BF16)


def _even_proj(x, g, sc, sh, w, tables):
    b, s, d = x.shape
    n = w.shape[1]
    tm = PROJ_TM
    nblk = tm // MOBA_BLOCK
    row = lambda bi, i: (bi, i, 0)
    vec = lambda bi, i: (bi, 0, 0)
    hd = lambda bi, i: (bi, 0, i, 0)
    tab = pl.BlockSpec((tm, LANES), lambda bi, i: (i, 0))
    return pl.pallas_call(
        _even_proj_body,
        grid=(b, s // tm),
        in_specs=[pl.BlockSpec((1, tm, d), row),
                  pl.BlockSpec((1, d), lambda bi, i: (0, 0)),
                  pl.BlockSpec((1, 1, d), vec),
                  pl.BlockSpec((1, 1, d), vec),
                  pl.BlockSpec((d, n), lambda bi, i: (0, 0)),
                  tab, tab, tab],
        out_specs=[pl.BlockSpec((1, tm, MOBA_W), row),
                   pl.BlockSpec((1, MOBA_HEADS, tm, LANES), hd),
                   pl.BlockSpec((1, MOBA_HEADS, tm, LANES), hd),
                   pl.BlockSpec((1, tm, DIFF_W), row),
                   pl.BlockSpec((1, 1, nblk, MOBA_W), lambda bi, i: (bi, i, 0, 0))],
        out_shape=[jax.ShapeDtypeStruct((b, s, MOBA_W), BF16),
                   jax.ShapeDtypeStruct((b, MOBA_HEADS, s, LANES), BF16),
                   jax.ShapeDtypeStruct((b, MOBA_HEADS, s, LANES), BF16),
                   jax.ShapeDtypeStruct((b, s, DIFF_W), BF16),
                   jax.ShapeDtypeStruct((b, s // tm, nblk, MOBA_W), F32)],
        compiler_params=_params("parallel", "parallel"),
        name="even_proj",
    )(x, g, sc, sh, w, *tables)


def _moba_body(qa_ref, qb_ref, k_ref, v_ref, km_ref, oa_ref, ob_ref):
    a = pl.program_id(2)
    tq = qa_ref.shape[1]
    nb = km_ref.shape[1]
    tk = ATT_TK
    nt = k_ref.shape[2] // tq
    per = tk // tq
    km = km_ref[0]
    lane = lax.broadcasted_iota(jnp.int32, (tq, LANES), 1)
    lane_k = lax.broadcasted_iota(jnp.int32, (nb, LANES), 1)
    n_idx = lax.broadcasted_iota(jnp.int32, (nb, tq), 0)
    low = lane < HEAD_DIM

    def tile(i, q_ref, rows):
        q = q_ref[0]
        c0 = pl.multiple_of((i // per) * tk, tk)
        qpos = i * tq + lax.broadcasted_iota(jnp.int32, (tq, 1), 0)
        causal = c0 + lax.broadcasted_iota(jnp.int32, (1, tk), 1) <= qpos
        own_blk = (i * tq + lax.broadcasted_iota(jnp.int32, (nb, tq), 1)) // MOBA_BLOCK
        q_t, st_t = [], []
        for e in range(2):
            own = low if e == 0 else (lane >= HEAD_DIM)
            own_k = (lane_k < HEAD_DIM) if e == 0 else (lane_k >= HEAD_DIM)
            qm = jnp.where(own, q, jnp.zeros_like(q))
            km_hi, km_lo = _split_bf16(jnp.where(own_k, km, 0.0))
            gs = _dot_nt(km_hi, qm) + _dot_nt(km_lo, qm)
            gs = jnp.where(n_idx < own_blk, gs, NEG_INF)
            keep = (_rank_below(gs, MOBA_TOPK, rows) & (n_idx < own_blk)) | (n_idx == own_blk)
            bias_t = jnp.where(keep, 0.0, NEG_INF)
            off = HEAD_DIM * (1 - e)
            parts = [jnp.zeros((off, tq), F32)] if off else []
            pad = jnp.concatenate(parts + [bias_t, jnp.zeros((LANES - off - nb, tq), F32)], axis=0)
            q_t.append(jnp.where(own, q, jnp.transpose(pad).astype(BF16)))
            yield
        for e in range(2):
            st_t.append(_online_step(q_t[e], k_ref[0, e, pl.ds(c0, tk), :], v_ref[0, e, pl.ds(c0, tk), :],
                                     causal, None, None))
            yield
        return q_t, st_t

    long_tile = _Staged(tile(nt - 1 - a, qb_ref, nb)).result()
    short_tile = _Staged(tile(a, qa_ref, nb // 2))
    n_chunks = k_ref.shape[2] // tk
    acc_b, acc_a = _attn_pair(n_chunks - 1, (nt // 2 - 1) // per, a // per, tk, long_tile, short_tile,
                              lambda e, k0: k_ref[0, e, pl.ds(k0, tk), :],
                              lambda e, k0: v_ref[0, e, pl.ds(k0, tk), :], late_pv=True)
    for (acc0, acc1), o_ref in ((acc_a, oa_ref), (acc_b, ob_ref)):
        num = jnp.where(low, acc0, acc1)
        den = jnp.where(low, pltpu.roll(acc0, HEAD_DIM, 1), pltpu.roll(acc1, HEAD_DIM, 1))
        o_ref[0] = (num / den).astype(o_ref.dtype)


def _moba_attention(q, kx, vx, kmean):
    b, s, _ = q.shape
    nb = s // MOBA_BLOCK
    pairs = MOBA_W // LANES
    tq = ATT_TQ
    nt = s // tq
    assert tq % MOBA_BLOCK == 0 and ATT_TK % tq == 0 and nt % (2 * ATT_TK // tq) == 0
    half = jax.ShapeDtypeStruct((b, s // 2, MOBA_W), BF16)
    return pl.pallas_call(
        _moba_body,
        grid=(b, pairs, nt // 2),
        in_specs=[pl.BlockSpec((1, tq, LANES), lambda bi, hp, a: (bi, a, hp)),
                  pl.BlockSpec((1, tq, LANES), lambda bi, hp, a: (bi, nt - 1 - a, hp)),
                  pl.BlockSpec((1, 2, s, LANES), lambda bi, hp, a: (bi, hp, 0, 0)),
                  pl.BlockSpec((1, 2, s, LANES), lambda bi, hp, a: (bi, hp, 0, 0)),
                  pl.BlockSpec((1, nb, LANES), lambda bi, hp, a: (bi, 0, hp))],
        out_specs=[pl.BlockSpec((1, tq, LANES), lambda bi, hp, a: (bi, a, hp)),
                   pl.BlockSpec((1, tq, LANES), lambda bi, hp, a: (bi, nt // 2 - 1 - a, hp))],
        out_shape=[half, half],
        compiler_params=_params("parallel", "parallel", "arbitrary"),
        name="moba_attn",
    )(q, q, kx, vx, kmean)


def _diff_body(lam_ref, g_ref, qa_ref, qb_ref, k_ref, v_ref, oa_ref, ob_ref, *, lambda_init):
    a = pl.program_id(2)
    tq = qa_ref.shape[1]
    tk = ATT_TK
    nt = k_ref.shape[1] // tq
    per = tk // tq
    lp = lam_ref[...]
    lam = (jnp.exp(jnp.sum(lp[0:1] * lp[1:2], axis=1, keepdims=True))
           - jnp.exp(jnp.sum(lp[2:3] * lp[3:4], axis=1, keepdims=True)) + lambda_init)
    lane = lax.broadcasted_iota(jnp.int32, (tq, LANES), 1)
    ones = jnp.ones((tk, LANES), BF16)

    def key(m, k0):
        return k_ref[0, pl.ds(k0, tk), :]

    def value(m, k0):
        return jnp.concatenate([v_ref[0, pl.ds(k0, tk), :], ones], axis=1)

    def tile(i, q_ref):
        q = q_ref[0]
        q_t = [jnp.where(lane < HEAD_DIM, q, jnp.zeros_like(q)), jnp.where(lane >= HEAD_DIM, q, jnp.zeros_like(q))]
        c0 = pl.multiple_of((i // per) * tk, tk)
        qpos = i * tq + lax.broadcasted_iota(jnp.int32, (tq, 1), 0)
        causal = c0 + lax.broadcasted_iota(jnp.int32, (1, tk), 1) <= qpos
        st_t = []
        for m in range(2):
            st_t.append(_online_step(q_t[m], key(m, c0), value(m, c0), causal, None, None))
            yield
        return q_t, st_t

    long_tile = _Staged(tile(nt - 1 - a, qb_ref)).result()
    short_tile = _Staged(tile(a, qa_ref))
    n_chunks = k_ref.shape[1] // tk
    acc_b, acc_a = _attn_pair(n_chunks - 1, (nt // 2 - 1) // per, a // per, tk, long_tile, short_tile,
                              key, value, late_pv=False)
    for (a0, a1), o_ref in ((acc_a, oa_ref), (acc_b, ob_ref)):
        o = a0[:, :LANES] / a0[:, LANES:] - lam * (a1[:, :LANES] / a1[:, LANES:])
        y = o * lax.rsqrt(jnp.mean(o * o, axis=-1, keepdims=True) + NORM_EPS)
        o_ref[0] = ((y * g_ref[...]) * (1.0 - lambda_init)).astype(o_ref.dtype)


def _diff_attention(pb, lam_p, subln_g, lambda_init):
    b, s, _ = pb.shape
    tq = ATT_TQ
    nt = s // tq
    assert ATT_TK % tq == 0 and nt % (2 * ATT_TK // tq) == 0
    koff = DIFF_QK_W // LANES
    voff = 2 * koff
    half = jax.ShapeDtypeStruct((b, s // 2, DIFF_V_W), BF16)
    return pl.pallas_call(
        functools.partial(_diff_body, lambda_init=lambda_init),
        grid=(b, DIFF_HEADS, nt // 2),
        in_specs=[pl.BlockSpec((4, HEAD_DIM), lambda bi, h, a: (0, 0)),
                  pl.BlockSpec((1, LANES), lambda bi, h, a: (0, 0)),
                  pl.BlockSpec((1, tq, LANES), lambda bi, h, a: (bi, a, h)),
                  pl.BlockSpec((1, tq, LANES), lambda bi, h, a: (bi, nt - 1 - a, h)),
                  pl.BlockSpec((1, s, LANES), lambda bi, h, a: (bi, 0, koff + h)),
                  pl.BlockSpec((1, s, LANES), lambda bi, h, a: (bi, 0, voff + h))],
        out_specs=[pl.BlockSpec((1, tq, LANES), lambda bi, h, a: (bi, a, h)),
                   pl.BlockSpec((1, tq, LANES), lambda bi, h, a: (bi, nt // 2 - 1 - a, h))],
        out_shape=[half, half],
        compiler_params=_params("parallel", "parallel", "arbitrary"),
        name="diff_attn",
    )(lam_p, subln_g.reshape(1, LANES), pb, pb, pb, pb)


def _first_argmax_onehot(v, iota):
    mx = jnp.max(v, axis=1, keepdims=True)
    idx = jnp.min(jnp.where(v == mx, iota, float(v.shape[1])), axis=1, keepdims=True)
    return iota == idx, mx


def _moe_body(*refs, widths, final):
    nm = 2 * len(widths)
    x_ref, g1_ref = refs[0], refs[1]
    mix_refs = refs[2:2 + nm]
    (wo_ref, g_ref, sc_ref, sh_ref, gate_ref, wr_hi_ref, wr_lo_ref, br_ref,
     wg_ref, wu_ref, wd_ref) = refs[2 + nm:13 + nm]
    fg_ref = refs[13 + nm] if final else None
    o_ref, a_ref = refs[-2], refs[-1]
    first = pl.program_id(1) < pl.num_programs(1) // 2
    y = None
    r0 = 0
    for k, wd in enumerate(widths):
        mix = jnp.where(first, mix_refs[2 * k][0], mix_refs[2 * k + 1][0])
        t = _dot(mix, wo_ref[r0:r0 + wd, :])
        y = t if y is None else y + t
        r0 += wd
    x = x_ref[0] + g1_ref[0] * y
    h = _norm_mod(x, g_ref[...], sc_ref[0], sh_ref[0])
    h_hi, h_lo = _split_bf16(h)
    wr_hi = wr_hi_ref[...]
    r = (_dot(h_hi, wr_hi) + (_dot(h_lo, wr_hi) + _dot(h_hi, wr_lo_ref[...]))) + br_ref[...]
    tm = x.shape[0]
    gl = r[:, 0:MOE_GROUPS]
    iota = lax.broadcasted_iota(jnp.int32, (tm, MOE_GROUPS), 1).astype(F32)
    g_oh, g_mx = _first_argmax_onehot(gl, iota)
    gw = 1.0 / jnp.sum(jnp.exp(gl - g_mx), axis=1, keepdims=True)
    el_g = jnp.zeros((tm, MOE_PER_GROUP), F32)
    for g in range(MOE_GROUPS):
        lo = MOE_GROUPS + g * MOE_PER_GROUP
        el_g = el_g + jnp.where(g_oh[:, g:g + 1], r[:, lo:lo + MOE_PER_GROUP], 0.0)
    oh1, v1 = _first_argmax_onehot(el_g, iota)
    oh2, v2 = _first_argmax_onehot(jnp.where(oh1, -jnp.inf, el_g), iota)
    e2 = jnp.exp(v2 - v1)
    den = 1.0 + e2
    w_grp = jnp.where(oh1, 1.0 / den, 0.0) + jnp.where(oh2, e2 / den, 0.0)
    gsc = jnp.where(g_oh, gw, 0.0)

    hb = h_hi
    for e in range(MOE_EXPERTS):
        g, k = divmod(e, MOE_PER_GROUP)
        comb = gsc[:, g:g + 1] * w_grp[:, k:k + 1]
        gt = _dot(hb, wg_ref[e])
        up = _dot(hb, wu_ref[e])
        a = ((gt * (1.0 / (1.0 + jnp.exp(-gt)))) * up) * comb
        a_ref[:, e * MOE_FF:(e + 1) * MOE_FF] = a.astype(BF16)
    out = x + gate_ref[0] * _dot(a_ref[...], wd_ref[...])
    if final:
        out = (out * lax.rsqrt(jnp.mean(out * out, axis=-1, keepdims=True) + NORM_EPS)) * fg_ref[...]
    o_ref[0] = out


def _mix_moe(x, g1, mixes, w_out, g, sc, sh, gate, wg, bg, we, be, w_gate, w_up, w_down, final_g):
    b, s, d = x.shape
    tm = PROJ_TM
    nr = MOE_GROUPS + MOE_EXPERTS
    wr = jnp.pad(jnp.concatenate([wg, we], axis=1), ((0, 0), (0, LANES - nr)))
    br = jnp.pad(jnp.concatenate([bg, be], axis=0), (0, LANES - nr)).reshape(1, LANES)
    wr_hi, wr_lo = _split_bf16(wr)
    widths = tuple(m[0].shape[2] for m in mixes)
    nh = s // tm // 2
    final = final_g is not None
    row = lambda bi, i: (bi, i, 0)
    vec = lambda bi, i: (bi, 0, 0)
    const2 = lambda bi, i: (0, 0)
    const3 = lambda bi, i: (0, 0, 0)
    once = pl.Buffered(1)
    in_specs = ([pl.BlockSpec((1, tm, d), row), pl.BlockSpec((1, 1, d), vec)]
                + [spec for wd in widths for spec in
                   (pl.BlockSpec((1, tm, wd), lambda bi, i: (bi, jnp.minimum(i, nh - 1), 0)),
                    pl.BlockSpec((1, tm, wd), lambda bi, i: (bi, jnp.maximum(i - nh, 0), 0)))]
                + [pl.BlockSpec(w_out.shape, const2, pipeline_mode=once),
                   pl.BlockSpec((1, d), const2),
                   pl.BlockSpec((1, 1, d), vec),
                   pl.BlockSpec((1, 1, d), vec),
                   pl.BlockSpec((1, 1, d), vec),
                   pl.BlockSpec((d, LANES), const2),
                   pl.BlockSpec((d, LANES), const2),
                   pl.BlockSpec((1, LANES), const2),
                   pl.BlockSpec((MOE_EXPERTS, d, MOE_FF), const3, pipeline_mode=once),
                   pl.BlockSpec((MOE_EXPERTS, d, MOE_FF), const3, pipeline_mode=once),
                   pl.BlockSpec((MOE_EXPERTS * MOE_FF, d), const2, pipeline_mode=once)])
    args = [x, g1, *[half for m in mixes for half in m], w_out.astype(BF16), g, sc, sh, gate, wr_hi, wr_lo, br,
            w_gate.astype(BF16), w_up.astype(BF16), w_down.astype(BF16).reshape(MOE_EXPERTS * MOE_FF, d)]
    if final:
        in_specs.append(pl.BlockSpec((1, d), const2))
        args.append(final_g.reshape(1, d))
    return pl.pallas_call(
        functools.partial(_moe_body, widths=widths, final=final),
        grid=(b, s // tm),
        in_specs=in_specs,
        out_specs=pl.BlockSpec((1, tm, d), row),
        out_shape=jax.ShapeDtypeStruct((b, s, d), F32),
        scratch_shapes=[pltpu.VMEM((tm, MOE_EXPERTS * MOE_FF), BF16)],
        compiler_params=_params("parallel", "parallel"),
        name="mix_moe",
    )(*args)


ODD_PAD = 2688


def _odd_proj_body(x_ref, g_ref, sc_ref, sh_ref, w_ref, c_ref, s1_ref, s2_ref,
                   q_ref, kvf_ref, kvx_ref, gates_ref):
    tm = x_ref.shape[1]
    h = _norm_mod(x_ref[0], g_ref[...], sc_ref[0], sh_ref[0]).astype(BF16)
    c, s1, s2 = c_ref[...], s1_ref[...], s2_ref[...]
    half = NSA_Q_W // 2
    for idx in range(2):
        acc = _dot(h, w_ref[:, idx * half:(idx + 1) * half])
        q_ref[0, :, idx * half:(idx + 1) * half] = (_rope(acc, c, s1, s2) * QK_SCALE).astype(BF16)
    lane = lax.broadcasted_iota(jnp.int32, (tm, LANES), 1)
    low = lane < HEAD_DIM
    blk = (pl.program_id(1) * tm + lax.broadcasted_iota(jnp.int32, (tm, LANES), 0)) // SLC_BLOCK
    ind = jnp.where(lane == blk + HEAD_DIM, 1.0, 0.0)
    for idx in range(6):
        c0 = NSA_Q_W + idx * NSA_KV_W
        acc = _dot(h, w_ref[:, c0:c0 + NSA_KV_W])
        if idx % 2 == 0:
            acc = _rope(acc, c, s1, s2)
        if idx < 2:
            for g in range(NSA_KV_HEADS):
                kvf_ref[0, idx * NSA_KV_HEADS + g] = acc[:, g * HEAD_DIM:(g + 1) * HEAD_DIM]
            continue
        fill = 1.0 if idx % 2 == 1 else (ind if idx == 2 else 0.0)
        for gp in range(NSA_KV_HEADS // 2):
            pair = acc[:, gp * LANES:(gp + 1) * LANES]
            swapped = pltpu.roll(pair, HEAD_DIM, 1)
            kvx_ref[0, (idx - 2) * NSA_KV_HEADS + 2 * gp] = jnp.where(low, pair, fill).astype(BF16)
            kvx_ref[0, (idx - 2) * NSA_KV_HEADS + 2 * gp + 1] = jnp.where(low, swapped, fill).astype(BF16)
    c0 = NSA_Q_W + 6 * NSA_KV_W
    gl = _dot(h, w_ref[:, c0:c0 + LANES])
    gates = 1.0 / (1.0 + jnp.exp(-gl))
    per_group = 3 * NSA_GROUP
    for g in range(NSA_KV_HEADS):
        gates_ref[0, g] = pltpu.roll(gates, (LANES - g * per_group) % LANES, 1)


def _odd_proj(x, g, sc, sh, w, tables):
    b, s, d = x.shape
    n = w.shape[1]
    tm = PROJ_TM
    row = lambda bi, i: (bi, i, 0)
    vec = lambda bi, i: (bi, 0, 0)
    tab = pl.BlockSpec((tm, LANES), lambda bi, i: (i, 0))
    hd = lambda bi, i: (bi, 0, i, 0)
    return pl.pallas_call(
        _odd_proj_body,
        grid=(b, s // tm),
        in_specs=[pl.BlockSpec((1, tm, d), row),
                  pl.BlockSpec((1, d), lambda bi, i: (0, 0)),
                  pl.BlockSpec((1, 1, d), vec),
                  pl.BlockSpec((1, 1, d), vec),
                  pl.BlockSpec((d, n), lambda bi, i: (0, 0)),
                  tab, tab, tab],
        out_specs=[pl.BlockSpec((1, tm, NSA_Q_W), row),
                   pl.BlockSpec((1, 2 * NSA_KV_HEADS, tm, HEAD_DIM), hd),
                   pl.BlockSpec((1, 4 * NSA_KV_HEADS, tm, LANES), hd),
                   pl.BlockSpec((1, NSA_KV_HEADS, tm, LANES), hd)],
        out_shape=[jax.ShapeDtypeStruct((b, s, NSA_Q_W), BF16),
                   jax.ShapeDtypeStruct((b, 2 * NSA_KV_HEADS, s, HEAD_DIM), F32),
                   jax.ShapeDtypeStruct((b, 4 * NSA_KV_HEADS, s, LANES), BF16),
                   jax.ShapeDtypeStruct((b, NSA_KV_HEADS, s, LANES), F32)],
        compiler_params=_params("parallel", "parallel"),
        name="odd_proj",
    )(x, g, sc, sh, w, *tables)


def _compress_body(x_ref, pos_ref, w1_ref, b1_ref, w2_ref, b2_ref, o_ref):
    nrow = x_ref.shape[2] // CMP_STRIDE
    x = jnp.concatenate([x_ref[0, 0, pl.ds(l, nrow, stride=CMP_STRIDE), :] for l in range(CMP_STRIDE)], axis=1)
    half = CMP_STRIDE * HEAD_DIM
    xa = (x + pos_ref[0, 0:1]).astype(BF16)
    xb = (x + pos_ref[0, 1:2]).astype(BF16)
    a = _dot(xa, w1_ref[0, 0:half])
    bm = _dot(xb, w1_ref[0, half:2 * half])
    pre = (a + pltpu.roll(bm, nrow - 1, 0)) + b1_ref[0]
    hid = 0.5 * pre * (1.0 + jnp.tanh(math.sqrt(2.0 / math.pi) * (pre + 0.044715 * (pre * pre * pre))))
    o_ref[0, 0] = (_dot(hid.astype(BF16), w2_ref[0]) + b2_ref[0]).astype(o_ref.dtype)


def _compress(kvf, pos, w1, b1, w2, b2):
    b, n2, s, hd = kvf.shape
    g = n2 // 2
    nchunk = s // CMP_STRIDE
    half = CMP_STRIDE * hd
    kv = lambda bi, n: (n // g, 0, 0)
    return pl.pallas_call(
        _compress_body,
        grid=(b, n2),
        in_specs=[pl.BlockSpec((1, 1, s, hd), lambda bi, n: (bi, n, 0, 0)),
                  pl.BlockSpec((1, 2, half), kv),
                  pl.BlockSpec((1, 2 * half, CMP_HIDDEN), kv),
                  pl.BlockSpec((1, 1, CMP_HIDDEN), kv),
                  pl.BlockSpec((1, CMP_HIDDEN, hd), kv),
                  pl.BlockSpec((1, 1, hd), kv)],
        out_specs=pl.BlockSpec((1, 1, nchunk, hd), lambda bi, n: (bi, n, 0, 0)),
        out_shape=jax.ShapeDtypeStruct((b, n2, nchunk, hd), BF16),
        compiler_params=_params("parallel", "parallel"),
        name="nsa_compress",
    )(kvf, pos.reshape(2, 2, half), w1.astype(BF16), b1.reshape(2, 1, CMP_HIDDEN),
      w2.astype(BF16), b2.reshape(2, 1, hd))


def _nsa_tile(i, first_half, q_ref, gt_ref, kc, vc, ks_ref, vs_ref, kw_ref, vw_ref, ovt):
    tq = q_ref.shape[1]
    r = NSA_GROUP
    q0 = i * tq
    qf = q_ref[0]
    q4 = jnp.concatenate([qf[:, h * HEAD_DIM:(h + 1) * HEAD_DIM] for h in range(r)], axis=0)
    qpos_c = q0 + lax.broadcasted_iota(jnp.int32, (tq, 1), 0)
    qpos4 = jnp.concatenate([qpos_c] * r, axis=0)

    if first_half:
        half_nc = kc.shape[0] // 2
        kc, vc, ovt = kc[:half_nc], vc[:half_nc], ovt[:, :half_nc]
    nc = kc.shape[0]
    s_c = _dot_nt(q4, kc)
    cmp_end = lax.broadcasted_iota(jnp.int32, (1, nc), 1) * CMP_STRIDE + (CMP_BLOCK - 1)
    s_c = jnp.where(cmp_end <= qpos4, s_c, NEG_INF)
    e_c = jnp.exp2(s_c - jnp.max(s_c, axis=1, keepdims=True))
    p_c = e_c / jnp.sum(e_c, axis=1, keepdims=True)
    p_c = jnp.where(qpos4 >= CMP_BLOCK - 1, p_c, 0.0)
    o_c = _dot(p_c.astype(BF16), vc)
    yield

    p_sum = p_c[0:tq]
    for h in range(1, r):
        p_sum = p_sum + p_c[h * tq:(h + 1) * tq]
    ps_hi, ps_lo = _split_bf16(p_sum)
    imp = _dot_nt(ovt, ps_hi) + _dot_nt(ovt, ps_lo)
    ns = imp.shape[0]
    blk = lax.broadcasted_iota(jnp.int32, (ns, tq), 0)
    qpos_r = q0 + lax.broadcasted_iota(jnp.int32, (ns, tq), 1)
    own = qpos_r // SLC_BLOCK
    started = blk * SLC_BLOCK <= qpos_r
    forced = (blk == 0) | (blk == own) | (blk == own - 1)
    imp = jnp.where(started, jnp.where(forced, FORCE_SCORE, imp), NEG_INF)
    bias_t = jnp.where(_rank_below(imp, SLC_TOPN, ns // 2 if first_half else ns), 0.0, NEG_INF)
    parts = [jnp.zeros((HEAD_DIM, tq), F32), bias_t]
    if ns < LANES - HEAD_DIM:
        parts.append(jnp.zeros((LANES - HEAD_DIM - ns, tq), F32))
    bias = jnp.transpose(jnp.concatenate(parts, axis=0)).astype(BF16)
    lane4 = lax.broadcasted_iota(jnp.int32, (r * tq, LANES), 1)
    qz = jnp.concatenate([q4, jnp.zeros_like(q4)], axis=1)
    qs = jnp.where(lane4 < HEAD_DIM, qz, jnp.concatenate([bias] * r, axis=0))
    yield

    tk = NSA_TK
    d0 = pl.multiple_of((q0 // tk) * tk, tk)
    causal = d0 + lax.broadcasted_iota(jnp.int32, (1, tk), 1) <= qpos4
    st = _online_step(qs, ks_ref[0, 0, pl.ds(d0, tk), :], vs_ref[0, 0, pl.ds(d0, tk), :], causal, None, None)
    yield

    span = WINDOW + tq
    w0 = pl.multiple_of(jnp.maximum(q0 - WINDOW, 0), tq)
    kpos_w = w0 + lax.broadcasted_iota(jnp.int32, (1, tq), 1)
    s_w = _dot_nt(qz, kw_ref[0, 0, pl.ds(w0, span), :])
    cols = [jnp.where(kpos_w > qpos4 - WINDOW, s_w[:, :tq], NEG_INF)]
    if first_half:
        cols = [jnp.where(kpos_w <= qpos4, cols[0], NEG_INF)]
        cols += [jnp.where(kpos_w + c * tq <= qpos4, s_w[:, c * tq:(c + 1) * tq], NEG_INF)
                 for c in range(1, span // tq)]
    else:
        cols += [s_w[:, tq:span - tq],
                 jnp.where(kpos_w + (span - tq) <= qpos4, s_w[:, span - tq:], NEG_INF)]
    s_w = jnp.concatenate(cols, axis=1)
    p_w = jnp.exp2(s_w - jnp.max(s_w, axis=1, keepdims=True)).astype(BF16)
    acc_w = _dot(p_w, vw_ref[0, 0, pl.ds(w0, span), :])
    o_w = acc_w[:, :HEAD_DIM] / acc_w[:, HEAD_DIM:]
    gt = gt_ref[0, 0]

    def finish(acc_s):
        o_s = acc_s[:, :HEAD_DIM] / acc_s[:, HEAD_DIM:]
        outs = []
        for h in range(r):
            sl = slice(h * tq, (h + 1) * tq)
            outs.append(gt[:, 3 * h:3 * h + 1] * o_c[sl] + gt[:, 3 * h + 1:3 * h + 2] * o_s[sl]
                        + gt[:, 3 * h + 2:3 * h + 3] * o_w[sl])
        return jnp.concatenate(outs, axis=1)

    return [qs], [st], finish


def _nsa_body(qa_ref, qb_ref, kc_ref, vc_ref, ks_ref, vs_ref, kw_ref, vw_ref, ga_ref, gb_ref, ovt_ref,
              oa_ref, ob_ref):
    a = pl.program_id(2)
    tq = qa_ref.shape[1]
    tk = NSA_TK
    nt = ks_ref.shape[2] // tq
    kc, vc, ovt = kc_ref[0, 0], vc_ref[0, 0], ovt_ref[...]
    per = tk // tq
    n_chunks = ks_ref.shape[2] // tk
    long_tile = _Staged(_nsa_tile(nt - 1 - a, False, qb_ref, gb_ref, kc, vc, ks_ref, vs_ref, kw_ref, vw_ref, ovt))
    short_tile = _Staged(_nsa_tile(a, True, qa_ref, ga_ref, kc, vc, ks_ref, vs_ref, kw_ref, vw_ref, ovt))
    acc_b, acc_a = _attn_pair(n_chunks - 1, (nt // 2 - 1) // per, a // per, tk, long_tile.result(), short_tile,
                              lambda e, k0: ks_ref[0, 0, pl.ds(k0, tk), :],
                              lambda e, k0: vs_ref[0, 0, pl.ds(k0, tk), :], late_pv=False)
    oa_ref[0] = short_tile.result()[2](acc_a[0]).astype(oa_ref.dtype)
    ob_ref[0] = long_tile.result()[2](acc_b[0]).astype(ob_ref.dtype)


def _nsa_attention(q, cmp, kvx, gates):
    b, s, _ = q.shape
    g = NSA_KV_HEADS
    tq = NSA_TQ
    nt = s // tq
    nc = cmp.shape[2]
    ns = s // SLC_BLOCK
    assert ns <= LANES - HEAD_DIM
    assert nt % (2 * NSA_TK // tq) == 0 and s // 2 >= WINDOW
    cw = NSA_GROUP * HEAD_DIM
    cs = jnp.arange(nc)[None, :] * CMP_STRIDE
    ss = jnp.arange(ns)[:, None] * SLC_BLOCK
    ovt = ((cs <= ss + SLC_BLOCK - 1) & (cs + CMP_BLOCK - 1 >= ss)).astype(BF16)
    head = lambda off: (lambda bi, gi, a: (bi, off + gi, 0, 0))
    half = jax.ShapeDtypeStruct((b, s // 2, NSA_Q_W), BF16)
    return pl.pallas_call(
        _nsa_body,
        grid=(b, g, nt // 2),
        in_specs=[pl.BlockSpec((1, tq, cw), lambda bi, gi, a: (bi, a, gi)),
                  pl.BlockSpec((1, tq, cw), lambda bi, gi, a: (bi, nt - 1 - a, gi)),
                  pl.BlockSpec((1, 1, nc, HEAD_DIM), head(0)),
                  pl.BlockSpec((1, 1, nc, HEAD_DIM), head(g)),
                  pl.BlockSpec((1, 1, s, LANES), head(0)),
                  pl.BlockSpec((1, 1, s, LANES), head(g)),
                  pl.BlockSpec((1, 1, s, LANES), head(2 * g)),
                  pl.BlockSpec((1, 1, s, LANES), head(3 * g)),
                  pl.BlockSpec((1, 1, tq, LANES), lambda bi, gi, a: (bi, gi, a, 0)),
                  pl.BlockSpec((1, 1, tq, LANES), lambda bi, gi, a: (bi, gi, nt - 1 - a, 0)),
                  pl.BlockSpec((ns, nc), lambda bi, gi, a: (0, 0))],
        out_specs=[pl.BlockSpec((1, tq, cw), lambda bi, gi, a: (bi, a, gi)),
                   pl.BlockSpec((1, tq, cw), lambda bi, gi, a: (bi, nt // 2 - 1 - a, gi))],
        out_shape=[half, half],
        compiler_params=_params("parallel", "parallel", "arbitrary"),
        name="nsa_attn",
    )(q, q, cmp, cmp, kvx, kvx, kvx, kvx, gates, gates, ovt)


def kernel(x, c, norm1_g, norm2_g, final_g, ada_w, ada_b, ev_w_in, ev_w_out, ev_lambda, ev_subln_g,
           od_w_in, od_w_out, od_cmp_pos, od_cmp_w1, od_cmp_b1, od_cmp_w2, od_cmp_b2,
           moe_wg, moe_bg, moe_we, moe_be, moe_w_gate, moe_w_up, moe_w_down):
    b, s, d = x.shape
    tables = _rope_tables(s)
    mod = _ada_mod(c, ada_w, ada_b)
    for l in range(DEPTH):
        sh1, sc1, g1, sh2, sc2, g2 = (mod[l, :, None, k * d:(k + 1) * d] for k in range(6))
        i = l // 2
        if l % 2 == 0:
            lambda_init = 0.8 - 0.6 * math.exp(-0.3 * l)
            qa, kx, vx, pb, kmean = _even_proj(x, norm1_g[l].reshape(1, d), sc1, sh1,
                                               ev_w_in[i].astype(BF16), tables)
            oa = _moba_attention(qa, kx, vx, kmean.reshape(b, s // MOBA_BLOCK, MOBA_W))
            ob = _diff_attention(pb, ev_lambda[i], ev_subln_g[i], lambda_init)
            mixes = (oa, ob)
            w_out = ev_w_out[i]
        else:
            w = jnp.pad(od_w_in[i], ((0, 0), (0, ODD_PAD - ODD_IN))).astype(BF16)
            q, kvf, kvx, gates = _odd_proj(x, norm1_g[l].reshape(1, d), sc1, sh1, w, tables)
            cmp = _compress(kvf, od_cmp_pos[i], od_cmp_w1[i], od_cmp_b1[i], od_cmp_w2[i], od_cmp_b2[i])
            mixes = (_nsa_attention(q, cmp, kvx, gates),)
            w_out = od_w_out[i]
        x = _mix_moe(x, g1, mixes, w_out, norm2_g[l].reshape(1, d), sc2, sh2, g2, moe_wg[l], moe_bg[l],
                     moe_we[l], moe_be[l], moe_w_gate[l], moe_w_up[l], moe_w_down[l],
                     final_g if l == DEPTH - 1 else None)
    return x
```

```python
import functools
import math

import jax
import jax.numpy as jnp
from jax import lax
from jax.experimental import pallas as pl
from jax.experimental.pallas import tpu as pltpu

F32 = jnp.float32
BF16 = jnp.bfloat16

DEPTH = 4
HEAD_DIM = 64
ROPE_DIM = HEAD_DIM // 4
ROPE_HALF = ROPE_DIM // 2
ROPE_THETA = 500000.0
NORM_EPS = 1e-6
NEG_INF = -1e30
FORCE_SCORE = 1e6
QK_SCALE = HEAD_DIM ** -0.5 * math.log2(math.e)

MOBA_HEADS = 8
MOBA_BLOCK = 256
MOBA_TOPK = 3
DIFF_HEADS = 4
MOBA_W = MOBA_HEADS * HEAD_DIM
DIFF_QK_W = DIFF_HEADS * 2 * HEAD_DIM
DIFF_V_W = DIFF_HEADS * 2 * HEAD_DIM
DIFF_W = 2 * DIFF_QK_W + DIFF_V_W

NSA_HEADS = 16
NSA_GROUP = 4
NSA_KV_HEADS = 4
CMP_BLOCK = 32
CMP_STRIDE = 16
CMP_HIDDEN = 256
SLC_BLOCK = 64
SLC_TOPN = 16
WINDOW = 512
NSA_Q_W = NSA_HEADS * HEAD_DIM
NSA_KV_W = NSA_KV_HEADS * HEAD_DIM
ODD_IN = NSA_Q_W + 6 * NSA_KV_W + 3 * NSA_HEADS

MOE_GROUPS = 4
MOE_PER_GROUP = 4
MOE_EXPERTS = 16
MOE_FF = 256

LANES = 128
VMEM_LIMIT = 56 * 1024 * 1024

PROJ_TM = 1024
MOE_TM = 512
ATT_TQ = 512
ATT_TK = 512
NSA_TQ = 256
NSA_TK = 512


def _params(*sem):
    return pltpu.CompilerParams(dimension_semantics=sem, vmem_limit_bytes=VMEM_LIMIT)


def _dot(a, b):
    return jnp.dot(a, b, preferred_element_type=F32)


def _dot_nt(a, b):
    return lax.dot_general(a, b, (((1,), (1,)), ((), ())), preferred_element_type=F32)


def _split_bf16(x):
    hi = x.astype(BF16)
    lo = (x - hi.astype(F32)).astype(BF16)
    return hi, lo


def _norm_mod(x, g, sc, sh):
    y = x * lax.rsqrt(jnp.mean(x * x, axis=-1, keepdims=True) + NORM_EPS)
    return (y * g) * (1.0 + sc) + sh


def _rope(t, c, s1, s2):
    w = t.shape[1]
    k = w // LANES
    cw = jnp.concatenate([c] * k, axis=1) if k > 1 else c
    s1w = jnp.concatenate([s1] * k, axis=1) if k > 1 else s1
    s2w = jnp.concatenate([s2] * k, axis=1) if k > 1 else s2
    return t * cw + pltpu.roll(t, ROPE_HALF, 1) * s1w + pltpu.roll(t, w - ROPE_HALF, 1) * s2w


def _rope_tables(seq):
    pos = jnp.arange(seq, dtype=F32)
    inv = ROPE_THETA ** (-jnp.arange(0, ROPE_DIM, 2, dtype=F32) / ROPE_DIM)
    ang = pos[:, None] * inv[None, :]
    cos, sin = jnp.cos(ang), jnp.sin(ang)
    ones = jnp.ones((seq, HEAD_DIM - ROPE_DIM), F32)
    zeros8 = jnp.zeros((seq, ROPE_HALF), F32)
    zeros = jnp.zeros((seq, HEAD_DIM - ROPE_DIM), F32)
    c = jnp.concatenate([cos, cos, ones], axis=1)
    s1 = jnp.concatenate([zeros8, sin, zeros], axis=1)
    s2 = jnp.concatenate([-sin, zeros8, zeros], axis=1)
    rep = LANES // HEAD_DIM
    return tuple(jnp.tile(t, (1, rep)) for t in (c, s1, s2))


def _rank_below(v, k, rows):
    n = v.shape[0]
    sub = 8
    groups = [v[g:g + sub] for g in range(0, n, sub)]
    cnts = [jnp.zeros(g.shape, F32) for g in groups]
    idx = lax.broadcasted_iota(jnp.int32, groups[0].shape, 0)
    for m in range(rows):
        rm = v[m:m + 1, :]
        for j, g in enumerate(groups):
            if j * sub > m:
                beat = rm >= g
            elif j * sub + sub - 1 < m:
                beat = rm > g
            else:
                beat = (rm > g) | ((rm == g) & (idx > m - j * sub))
            cnts[j] = cnts[j] + jnp.where(beat, 1.0, 0.0)
    return jnp.concatenate(cnts, axis=0) < k


def _online_step(q, k, v, mask, m, acc):
    s = _dot_nt(q, k)
    if mask is not None:
        s = jnp.where(mask, s, NEG_INF)
    m_new = jnp.max(s, axis=1, keepdims=True)
    if m is not None:
        m_new = jnp.maximum(m, m_new)
    p = jnp.exp2(s - m_new).astype(v.dtype)
    pv = _dot(p, v)
    if m is None:
        return m_new, pv
    return m_new, jnp.exp2(m - m_new) * acc + pv


class _Staged:
    def __init__(self, gen):
        self.gen, self.done, self.value = gen, False, None

    def advance(self):
        if not self.done:
            try:
                next(self.gen)
            except StopIteration as stop:
                self.done, self.value = True, stop.value

    def result(self):
        while not self.done:
            self.advance()
        return self.value


def _attn_pair(n_past, max_short, cnt_short, tk, long_tile, short_tile, key, value, late_pv):
    qs_l, st_l = long_tile[:2]
    ns = len(qs_l)
    n_static = n_past - max_short
    cnt_long = n_past - cnt_short
    q_cur = list(qs_l)
    m_cur = [st_l[e][0] for e in range(ns)]
    acc_cur = [st_l[e][1] for e in range(ns)]
    out_long = list(acc_cur)
    pend = None

    def flush():
        return [acc_cur[e] + _dot(pend[0][e], value(e, pend[1])) for e in range(ns)]

    for u in range(n_past):
        if pend is not None:
            acc_cur = flush()
        if u >= n_static:
            qs_s, st_s = short_tile.result()[:2]
            sw = u == cnt_long
            out_long = [jnp.where(sw, acc_cur[e], out_long[e]) for e in range(ns)]
            acc_cur = [jnp.where(sw, st_s[e][1], acc_cur[e]) for e in range(ns)]
            m_cur = [jnp.where(sw, st_s[e][0], m_cur[e]) for e in range(ns)]
            q_cur = [jnp.where(sw, qs_s[e], q_cur[e]) for e in range(ns)]
            chunk = jnp.where(u < cnt_long, u, u - cnt_long)
        else:
            chunk = u
        k0 = pl.multiple_of(chunk * tk, tk) if u >= n_static else chunk * tk
        if late_pv:
            ss = [_dot_nt(q_cur[e], key(e, k0)) for e in range(ns)]
            ps = []
            for e in range(ns):
                m_new = jnp.maximum(m_cur[e], jnp.max(ss[e], axis=1, keepdims=True))
                ps.append(jnp.exp2(ss[e] - m_new).astype(BF16))
                acc_cur[e] = jnp.exp2(m_cur[e] - m_new) * acc_cur[e]
                m_cur[e] = m_new
            pend = (ps, k0)
        else:
            for e in range(ns):
                m_cur[e], acc_cur[e] = _online_step(q_cur[e], key(e, k0), value(e, k0), None,
                                                    m_cur[e], acc_cur[e])
        if u < n_static:
            short_tile.advance()
    if pend is not None:
        acc_cur = flush()
    st_s = short_tile.result()[1]
    none_short = cnt_short == 0
    out_long = [jnp.where(none_short, acc_cur[e], out_long[e]) for e in range(ns)]
    out_short = [jnp.where(none_short, st_s[e][1], acc_cur[e]) for e in range(ns)]
    return out_long, out_short


def _ada_body(c_ref, w_ref, b_ref, o_ref):
    c = c_ref[...]
    cs = c * (1.0 / (1.0 + jnp.exp(-c)))
    o_ref[0] = jnp.dot(cs, w_ref[0], preferred_element_type=F32,
                       precision=lax.Precision.HIGHEST) + b_ref[0]


def _ada_mod(c, ada_w, ada_b):
    b, d = c.shape
    depth, _, n = ada_w.shape
    rows = 8
    tn = 1536
    cp = jnp.pad(c, ((0, rows - b), (0, 0)))
    out = pl.pallas_call(
        _ada_body,
        grid=(depth, n // tn),
        in_specs=[pl.BlockSpec((rows, d), lambda l, j: (0, 0)),
                  pl.BlockSpec((1, d, tn), lambda l, j: (l, 0, j)),
                  pl.BlockSpec((1, 1, tn), lambda l, j: (l, 0, j))],
        out_specs=pl.BlockSpec((1, rows, tn), lambda l, j: (l, 0, j)),
        out_shape=jax.ShapeDtypeStruct((depth, rows, n), F32),
        compiler_params=_params("parallel", "parallel"),
        name="ada_mod",
    )(cp, ada_w, ada_b.reshape(depth, 1, n))
    return out[:, :b]


def _even_proj_body(x_ref, g_ref, sc_ref, sh_ref, w_ref, c_ref, s1_ref, s2_ref,
                    q_ref, k_ref, v_ref, pb_ref, km_ref):
    tm = x_ref.shape[1]
    h = _norm_mod(x_ref[0], g_ref[...], sc_ref[0], sh_ref[0]).astype(BF16)
    c, s1, s2 = c_ref[...], s1_ref[...], s2_ref[...]
    ch = MOBA_W
    lane = lax.broadcasted_iota(jnp.int32, (tm, LANES), 1)
    low = lane < HEAD_DIM
    blk = (pl.program_id(1) * tm + lax.broadcasted_iota(jnp.int32, (tm, LANES), 0)) // MOBA_BLOCK
    ind_hi = jnp.where(lane == blk + HEAD_DIM, 1.0, 0.0)
    ind_lo = jnp.where(lane == blk, 1.0, 0.0)
    for idx, kind in enumerate(("q", "k", "v", "q", "k", "v")):
        acc = _dot(h, w_ref[:, idx * ch:(idx + 1) * ch])
        if kind != "v":
            acc = _rope(acc, c, s1, s2)
        if kind == "q":
            acc = acc * QK_SCALE
        if idx == 0:
            q_ref[0] = acc.astype(BF16)
        elif idx == 1:
            nblk = tm // MOBA_BLOCK
            km_ref[0, 0] = jnp.concatenate(
                [jnp.mean(acc[n * MOBA_BLOCK:(n + 1) * MOBA_BLOCK], axis=0, keepdims=True)
                 for n in range(nblk)], axis=0)
            for hp in range(ch // LANES):
                kp = acc[:, hp * LANES:(hp + 1) * LANES]
                k_ref[0, 2 * hp] = jnp.where(low, kp, ind_hi).astype(BF16)
                k_ref[0, 2 * hp + 1] = jnp.where(low, ind_lo, kp).astype(BF16)
        elif idx == 2:
            for hp in range(ch // LANES):
                vp = acc[:, hp * LANES:(hp + 1) * LANES]
                v_ref[0, 2 * hp] = jnp.where(low, vp, 1.0).astype(BF16)
                v_ref[0, 2 * hp + 1] = jnp.where(low, 1.0, vp).astype(BF16)
        else:
            pb_ref[0, :, (idx - 3) * ch:(idx - 2) * ch] = acc.astype(BF16)


def _even_proj(x, g, sc, sh, w, tables):
    b, s, d = x.shape
    n = w.shape[1]
    tm = PROJ_TM
    nblk = tm // MOBA_BLOCK
    row = lambda bi, i: (bi, i, 0)
    vec = lambda bi, i: (bi, 0, 0)
    hd = lambda bi, i: (bi, 0, i, 0)
    tab = pl.BlockSpec((tm, LANES), lambda bi, i: (i, 0))
    return pl.pallas_call(
        _even_proj_body,
        grid=(b, s // tm),
        in_specs=[pl.BlockSpec((1, tm, d), row),
                  pl.BlockSpec((1, d), lambda bi, i: (0, 0)),
                  pl.BlockSpec((1, 1, d), vec),
                  pl.BlockSpec((1, 1, d), vec),
                  pl.BlockSpec((d, n), lambda bi, i: (0, 0)),
                  tab, tab, tab],
        out_specs=[pl.BlockSpec((1, tm, MOBA_W), row),
                   pl.BlockSpec((1, MOBA_HEADS, tm, LANES), hd),
                   pl.BlockSpec((1, MOBA_HEADS, tm, LANES), hd),
                   pl.BlockSpec((1, tm, DIFF_W), row),
                   pl.BlockSpec((1, 1, nblk, MOBA_W), lambda bi, i: (bi, i, 0, 0))],
        out_shape=[jax.ShapeDtypeStruct((b, s, MOBA_W), BF16),
                   jax.ShapeDtypeStruct((b, MOBA_HEADS, s, LANES), BF16),
                   jax.ShapeDtypeStruct((b, MOBA_HEADS, s, LANES), BF16),
                   jax.ShapeDtypeStruct((b, s, DIFF_W), BF16),
                   jax.ShapeDtypeStruct((b, s // tm, nblk, MOBA_W), F32)],
        compiler_params=_params("parallel", "parallel"),
        name="even_proj",
    )(x, g, sc, sh, w, *tables)


def _moba_body(qa_ref, qb_ref, k_ref, v_ref, km_ref, oa_ref, ob_ref):
    a = pl.program_id(2)
    tq = qa_ref.shape[1]
    nb = km_ref.shape[1]
    tk = ATT_TK
    nt = k_ref.shape[2] // tq
    per = tk // tq
    km = km_ref[0]
    lane = lax.broadcasted_iota(jnp.int32, (tq, LANES), 1)
    lane_k = lax.broadcasted_iota(jnp.int32, (nb, LANES), 1)
    n_idx = lax.broadcasted_iota(jnp.int32, (nb, tq), 0)
    low = lane < HEAD_DIM

    def tile(i, q_ref, rows):
        q = q_ref[0]
        c0 = pl.multiple_of((i // per) * tk, tk)
        qpos = i * tq + lax.broadcasted_iota(jnp.int32, (tq, 1), 0)
        causal = c0 + lax.broadcasted_iota(jnp.int32, (1, tk), 1) <= qpos
        own_blk = (i * tq + lax.broadcasted_iota(jnp.int32, (nb, tq), 1)) // MOBA_BLOCK
        q_t, st_t = [], []
        for e in range(2):
            own = low if e == 0 else (lane >= HEAD_DIM)
            own_k = (lane_k < HEAD_DIM) if e == 0 else (lane_k >= HEAD_DIM)
            qm = jnp.where(own, q, jnp.zeros_like(q))
            km_hi, km_lo = _split_bf16(jnp.where(own_k, km, 0.0))
            gs = _dot_nt(km_hi, qm) + _dot_nt(km_lo, qm)
            gs = jnp.where(n_idx < own_blk, gs, NEG_INF)
            keep = (_rank_below(gs, MOBA_TOPK, rows) & (n_idx < own_blk)) | (n_idx == own_blk)
            bias_t = jnp.where(keep, 0.0, NEG_INF)
            off = HEAD_DIM * (1 - e)
            parts = [jnp.zeros((off, tq), F32)] if off else []
            pad = jnp.concatenate(parts + [bias_t, jnp.zeros((LANES - off - nb, tq), F32)], axis=0)
            q_t.append(jnp.where(own, q, jnp.transpose(pad).astype(BF16)))
            yield
        for e in range(2):
            st_t.append(_online_step(q_t[e], k_ref[0, e, pl.ds(c0, tk), :], v_ref[0, e, pl.ds(c0, tk), :],
                                     causal, None, None))
            yield
        return q_t, st_t

    long_tile = _Staged(tile(nt - 1 - a, qb_ref, nb)).result()
    short_tile = _Staged(tile(a, qa_ref, nb // 2))
    n_chunks = k_ref.shape[2] // tk
    acc_b, acc_a = _attn_pair(n_chunks - 1, (nt // 2 - 1) // per, a // per, tk, long_tile, short_tile,
                              lambda e, k0: k_ref[0, e, pl.ds(k0, tk), :],
                              lambda e, k0: v_ref[0, e, pl.ds(k0, tk), :], late_pv=True)
    for (acc0, acc1), o_ref in ((acc_a, oa_ref), (acc_b, ob_ref)):
        num = jnp.where(low, acc0, acc1)
        den = jnp.where(low, pltpu.roll(acc0, HEAD_DIM, 1), pltpu.roll(acc1, HEAD_DIM, 1))
        o_ref[0] = (num / den).astype(o_ref.dtype)


def _moba_attention(q, kx, vx, kmean):
    b, s, _ = q.shape
    nb = s // MOBA_BLOCK
    pairs = MOBA_W // LANES
    tq = ATT_TQ
    nt = s // tq
    assert tq % MOBA_BLOCK == 0 and ATT_TK % tq == 0 and nt % (2 * ATT_TK // tq) == 0
    half = jax.ShapeDtypeStruct((b, s // 2, MOBA_W), BF16)
    return pl.pallas_call(
        _moba_body,
        grid=(b, pairs, nt // 2),
        in_specs=[pl.BlockSpec((1, tq, LANES), lambda bi, hp, a: (bi, a, hp)),
                  pl.BlockSpec((1, tq, LANES), lambda bi, hp, a: (bi, nt - 1 - a, hp)),
                  pl.BlockSpec((1, 2, s, LANES), lambda bi, hp, a: (bi, hp, 0, 0)),
                  pl.BlockSpec((1, 2, s, LANES), lambda bi, hp, a: (bi, hp, 0, 0)),
                  pl.BlockSpec((1, nb, LANES), lambda bi, hp, a: (bi, 0, hp))],
        out_specs=[pl.BlockSpec((1, tq, LANES), lambda bi, hp, a: (bi, a, hp)),
                   pl.BlockSpec((1, tq, LANES), lambda bi, hp, a: (bi, nt // 2 - 1 - a, hp))],
        out_shape=[half, half],
        compiler_params=_params("parallel", "parallel", "arbitrary"),
        name="moba_attn",
    )(q, q, kx, vx, kmean)


def _diff_body(lam_ref, g_ref, qa_ref, qb_ref, k_ref, v_ref, oa_ref, ob_ref, *, lambda_init):
    a = pl.program_id(2)
    tq = qa_ref.shape[1]
    tk = ATT_TK
    nt = k_ref.shape[1] // tq
    per = tk // tq
    lp = lam_ref[...]
    lam = (jnp.exp(jnp.sum(lp[0:1] * lp[1:2], axis=1, keepdims=True))
           - jnp.exp(jnp.sum(lp[2:3] * lp[3:4], axis=1, keepdims=True)) + lambda_init)
    lane = lax.broadcasted_iota(jnp.int32, (tq, LANES), 1)
    ones = jnp.ones((tk, LANES), BF16)

    def key(m, k0):
        return k_ref[0, pl.ds(k0, tk), :]

    def value(m, k0):
        return jnp.concatenate([v_ref[0, pl.ds(k0, tk), :], ones], axis=1)

    def tile(i, q_ref):
        q = q_ref[0]
        q_t = [jnp.where(lane < HEAD_DIM, q, jnp.zeros_like(q)), jnp.where(lane >= HEAD_DIM, q, jnp.zeros_like(q))]
        c0 = pl.multiple_of((i // per) * tk, tk)
        qpos = i * tq + lax.broadcasted_iota(jnp.int32, (tq, 1), 0)
        causal = c0 + lax.broadcasted_iota(jnp.int32, (1, tk), 1) <= qpos
        st_t = []
        for m in range(2):
            st_t.append(_online_step(q_t[m], key(m, c0), value(m, c0), causal, None, None))
            yield
        return q_t, st_t

    long_tile = _Staged(tile(nt - 1 - a, qb_ref)).result()
    short_tile = _Staged(tile(a, qa_ref))
    n_chunks = k_ref.shape[1] // tk
    acc_b, acc_a = _attn_pair(n_chunks - 1, (nt // 2 - 1) // per, a // per, tk, long_tile, short_tile,
                              key, value, late_pv=False)
    for (a0, a1), o_ref in ((acc_a, oa_ref), (acc_b, ob_ref)):
        o = a0[:, :LANES] / a0[:, LANES:] - lam * (a1[:, :LANES] / a1[:, LANES:])
        y = o * lax.rsqrt(jnp.mean(o * o, axis=-1, keepdims=True) + NORM_EPS)
        o_ref[0] = ((y * g_ref[...]) * (1.0 - lambda_init)).astype(o_ref.dtype)


def _diff_attention(pb, lam_p, subln_g, lambda_init):
    b, s, _ = pb.shape
    tq = ATT_TQ
    nt = s // tq
    assert ATT_TK % tq == 0 and nt % (2 * ATT_TK // tq) == 0
    koff = DIFF_QK_W // LANES
    voff = 2 * koff
    half = jax.ShapeDtypeStruct((b, s // 2, DIFF_V_W), BF16)
    return pl.pallas_call(
        functools.partial(_diff_body, lambda_init=lambda_init),
        grid=(b, DIFF_HEADS, nt // 2),
        in_specs=[pl.BlockSpec((4, HEAD_DIM), lambda bi, h, a: (0, 0)),
                  pl.BlockSpec((1, LANES), lambda bi, h, a: (0, 0)),
                  pl.BlockSpec((1, tq, LANES), lambda bi, h, a: (bi, a, h)),
                  pl.BlockSpec((1, tq, LANES), lambda bi, h, a: (bi, nt - 1 - a, h)),
                  pl.BlockSpec((1, s, LANES), lambda bi, h, a: (bi, 0, koff + h)),
                  pl.BlockSpec((1, s, LANES), lambda bi, h, a: (bi, 0, voff + h))],
        out_specs=[pl.BlockSpec((1, tq, LANES), lambda bi, h, a: (bi, a, h)),
                   pl.BlockSpec((1, tq, LANES), lambda bi, h, a: (bi, nt // 2 - 1 - a, h))],
        out_shape=[half, half],
        compiler_params=_params("parallel", "parallel", "arbitrary"),
        name="diff_attn",
    )(lam_p, subln_g.reshape(1, LANES), pb, pb, pb, pb)


def _first_argmax_onehot(v, iota):
    mx = jnp.max(v, axis=1, keepdims=True)
    idx = jnp.min(jnp.where(v == mx, iota, float(v.shape[1])), axis=1, keepdims=True)
    return iota == idx, mx


def _moe_body(*refs, widths, final):
    nm = 2 * len(widths)
    x_ref, g1_ref = refs[0], refs[1]
    mix_refs = refs[2:2 + nm]
    (wo_ref, g_ref, sc_ref, sh_ref, gate_ref, wr_ref, br_ref,
     wg_ref, wu_ref, wd_ref) = refs[2 + nm:12 + nm]
    fg_ref = refs[12 + nm] if final else None
    o_ref, a_ref = refs[-2], refs[-1]
    first = pl.program_id(1) < pl.num_programs(1) // 2
    y = None
    r0 = 0
    for k, wd in enumerate(widths):
        mix = jnp.where(first, mix_refs[2 * k][0], mix_refs[2 * k + 1][0])
        t = _dot(mix, wo_ref[r0:r0 + wd, :])
        y = t if y is None else y + t
        r0 += wd
    x = x_ref[0] + g1_ref[0] * y
    h = _norm_mod(x, g_ref[...], sc_ref[0], sh_ref[0])
    h_hi, h_lo = _split_bf16(h)
    tm = x.shape[0]
    rr = _dot(jnp.concatenate([h_hi, h_lo], axis=0), wr_ref[...])
    r = (rr[:tm, :LANES] + (rr[tm:, :LANES] + rr[:tm, LANES:])) + br_ref[...]
    gl = r[:, 0:MOE_GROUPS]
    iota = lax.broadcasted_iota(jnp.int32, (tm, MOE_GROUPS), 1).astype(F32)
    g_oh, g_mx = _first_argmax_onehot(gl, iota)
    gw = 1.0 / jnp.sum(jnp.exp(gl - g_mx), axis=1, keepdims=True)
    el_g = jnp.zeros((tm, MOE_PER_GROUP), F32)
    for g in range(MOE_GROUPS):
        lo = MOE_GROUPS + g * MOE_PER_GROUP
        el_g = el_g + jnp.where(g_oh[:, g:g + 1], r[:, lo:lo + MOE_PER_GROUP], 0.0)
    oh1, v1 = _first_argmax_onehot(el_g, iota)
    oh2, v2 = _first_argmax_onehot(jnp.where(oh1, -jnp.inf, el_g), iota)
    e2 = jnp.exp(v2 - v1)
    den = 1.0 + e2
    w_grp = jnp.where(oh1, 1.0 / den, 0.0) + jnp.where(oh2, e2 / den, 0.0)
    gsc = jnp.where(g_oh, gw, 0.0)

    hb = h_hi
    for e in range(MOE_EXPERTS):
        g, k = divmod(e, MOE_PER_GROUP)
        comb = gsc[:, g:g + 1] * w_grp[:, k:k + 1]
        gt = _dot(hb, wg_ref[e])
        up = _dot(hb, wu_ref[e])
        a = ((gt * (1.0 / (1.0 + jnp.exp(-gt)))) * up) * comb
        a_ref[:, e * MOE_FF:(e + 1) * MOE_FF] = a.astype(BF16)
    out = x + gate_ref[0] * _dot(a_ref[...], wd_ref[...])
    if final:
        out = (out * lax.rsqrt(jnp.mean(out * out, axis=-1, keepdims=True) + NORM_EPS)) * fg_ref[...]
    o_ref[0] = out


def _mix_moe(x, g1, mixes, w_out, g, sc, sh, gate, wg, bg, we, be, w_gate, w_up, w_down, final_g):
    b, s, d = x.shape
    tm = MOE_TM
    nr = MOE_GROUPS + MOE_EXPERTS
    wr = jnp.pad(jnp.concatenate([wg, we], axis=1), ((0, 0), (0, LANES - nr)))
    br = jnp.pad(jnp.concatenate([bg, be], axis=0), (0, LANES - nr)).reshape(1, LANES)
    wr_cat = jnp.concatenate(_split_bf16(wr), axis=1)
    widths = tuple(m[0].shape[2] for m in mixes)
    nh = s // tm // 2
    final = final_g is not None
    row = lambda bi, i: (bi, i, 0)
    vec = lambda bi, i: (bi, 0, 0)
    const2 = lambda bi, i: (0, 0)
    const3 = lambda bi, i: (0, 0, 0)
    once = pl.Buffered(1)
    in_specs = ([pl.BlockSpec((1, tm, d), row), pl.BlockSpec((1, 1, d), vec)]
                + [spec for wd in widths for spec in
                   (pl.BlockSpec((1, tm, wd), lambda bi, i: (bi, jnp.minimum(i, nh - 1), 0)),
                    pl.BlockSpec((1, tm, wd), lambda bi, i: (bi, jnp.maximum(i - nh, 0), 0)))]
                + [pl.BlockSpec(w_out.shape, const2, pipeline_mode=once),
                   pl.BlockSpec((1, d), const2),
                   pl.BlockSpec((1, 1, d), vec),
                   pl.BlockSpec((1, 1, d), vec),
                   pl.BlockSpec((1, 1, d), vec),
                   pl.BlockSpec((d, 2 * LANES), const2),
                   pl.BlockSpec((1, LANES), const2),
                   pl.BlockSpec((MOE_EXPERTS, d, MOE_FF), const3, pipeline_mode=once),
                   pl.BlockSpec((MOE_EXPERTS, d, MOE_FF), const3, pipeline_mode=once),
                   pl.BlockSpec((MOE_EXPERTS * MOE_FF, d), const2, pipeline_mode=once)])
    args = [x, g1, *[half for m in mixes for half in m], w_out.astype(BF16), g, sc, sh, gate, wr_cat, br,
            w_gate.astype(BF16), w_up.astype(BF16), w_down.astype(BF16).reshape(MOE_EXPERTS * MOE_FF, d)]
    if final:
        in_specs.append(pl.BlockSpec((1, d), const2))
        args.append(final_g.reshape(1, d))
    return pl.pallas_call(
        functools.partial(_moe_body, widths=widths, final=final),
        grid=(b, s // tm),
        in_specs=in_specs,
        out_specs=pl.BlockSpec((1, tm, d), row),
        out_shape=jax.ShapeDtypeStruct((b, s, d), F32),
        scratch_shapes=[pltpu.VMEM((tm, MOE_EXPERTS * MOE_FF), BF16)],
        compiler_params=_params("parallel", "parallel"),
        name="mix_moe",
    )(*args)


ODD_PAD = 2688


def _odd_proj_body(x_ref, g_ref, sc_ref, sh_ref, w_ref, c_ref, s1_ref, s2_ref,
                   q_ref, kvf_ref, kvx_ref, gates_ref):
    tm = x_ref.shape[1]
    h = _norm_mod(x_ref[0], g_ref[...], sc_ref[0], sh_ref[0]).astype(BF16)
    c, s1, s2 = c_ref[...], s1_ref[...], s2_ref[...]
    half = NSA_Q_W // 2
    for idx in range(2):
        acc = _dot(h, w_ref[:, idx * half:(idx + 1) * half])
        q_ref[0, :, idx * half:(idx + 1) * half] = (_rope(acc, c, s1, s2) * QK_SCALE).astype(BF16)
    lane = lax.broadcasted_iota(jnp.int32, (tm, LANES), 1)
    low = lane < HEAD_DIM
    blk = (pl.program_id(1) * tm + lax.broadcasted_iota(jnp.int32, (tm, LANES), 0)) // SLC_BLOCK
    ind = jnp.where(lane == blk + HEAD_DIM, 1.0, 0.0)
    for idx in range(6):
        c0 = NSA_Q_W + idx * NSA_KV_W
        acc = _dot(h, w_ref[:, c0:c0 + NSA_KV_W])
        if idx % 2 == 0:
            acc = _rope(acc, c, s1, s2)
        if idx < 2:
            for g in range(NSA_KV_HEADS):
                kvf_ref[0, idx * NSA_KV_HEADS + g] = acc[:, g * HEAD_DIM:(g + 1) * HEAD_DIM]
            continue
        fill = 1.0 if idx % 2 == 1 else (ind if idx == 2 else 0.0)
        for gp in range(NSA_KV_HEADS // 2):
            pair = acc[:, gp * LANES:(gp + 1) * LANES]
            swapped = pltpu.roll(pair, HEAD_DIM, 1)
            kvx_ref[0, (idx - 2) * NSA_KV_HEADS + 2 * gp] = jnp.where(low, pair, fill).astype(BF16)
            kvx_ref[0, (idx - 2) * NSA_KV_HEADS + 2 * gp + 1] = jnp.where(low, swapped, fill).astype(BF16)
    c0 = NSA_Q_W + 6 * NSA_KV_W
    gl = _dot(h, w_ref[:, c0:c0 + LANES])
    gates = 1.0 / (1.0 + jnp.exp(-gl))
    per_group = 3 * NSA_GROUP
    for g in range(NSA_KV_HEADS):
        gates_ref[0, g] = pltpu.roll(gates, (LANES - g * per_group) % LANES, 1)


def _odd_proj(x, g, sc, sh, w, tables):
    b, s, d = x.shape
    n = w.shape[1]
    tm = PROJ_TM
    row = lambda bi, i: (bi, i, 0)
    vec = lambda bi, i: (bi, 0, 0)
    tab = pl.BlockSpec((tm, LANES), lambda bi, i: (i, 0))
    hd = lambda bi, i: (bi, 0, i, 0)
    return pl.pallas_call(
        _odd_proj_body,
        grid=(b, s // tm),
        in_specs=[pl.BlockSpec((1, tm, d), row),
                  pl.BlockSpec((1, d), lambda bi, i: (0, 0)),
                  pl.BlockSpec((1, 1, d), vec),
                  pl.BlockSpec((1, 1, d), vec),
                  pl.BlockSpec((d, n), lambda bi, i: (0, 0)),
                  tab, tab, tab],
        out_specs=[pl.BlockSpec((1, tm, NSA_Q_W), row),
                   pl.BlockSpec((1, 2 * NSA_KV_HEADS, tm, HEAD_DIM), hd),
                   pl.BlockSpec((1, 4 * NSA_KV_HEADS, tm, LANES), hd),
                   pl.BlockSpec((1, NSA_KV_HEADS, tm, LANES), hd)],
        out_shape=[jax.ShapeDtypeStruct((b, s, NSA_Q_W), BF16),
                   jax.ShapeDtypeStruct((b, 2 * NSA_KV_HEADS, s, HEAD_DIM), F32),
                   jax.ShapeDtypeStruct((b, 4 * NSA_KV_HEADS, s, LANES), BF16),
                   jax.ShapeDtypeStruct((b, NSA_KV_HEADS, s, LANES), F32)],
        compiler_params=_params("parallel", "parallel"),
        name="odd_proj",
    )(x, g, sc, sh, w, *tables)


def _compress_body(x_ref, pos_ref, w1_ref, b1_ref, w2_ref, b2_ref, o_ref):
    nrow = x_ref.shape[2] // CMP_STRIDE
    x = jnp.concatenate([x_ref[0, 0, pl.ds(l, nrow, stride=CMP_STRIDE), :] for l in range(CMP_STRIDE)], axis=1)
    half = CMP_STRIDE * HEAD_DIM
    xa = (x + pos_ref[0, 0:1]).astype(BF16)
    xb = (x + pos_ref[0, 1:2]).astype(BF16)
    a = _dot(xa, w1_ref[0, 0:half])
    bm = _dot(xb, w1_ref[0, half:2 * half])
    pre = (a + pltpu.roll(bm, nrow - 1, 0)) + b1_ref[0]
    hid = 0.5 * pre * (1.0 + jnp.tanh(math.sqrt(2.0 / math.pi) * (pre + 0.044715 * (pre * pre * pre))))
    o_ref[0, 0] = (_dot(hid.astype(BF16), w2_ref[0]) + b2_ref[0]).astype(o_ref.dtype)


def _compress(kvf, pos, w1, b1, w2, b2):
    b, n2, s, hd = kvf.shape
    g = n2 // 2
    nchunk = s // CMP_STRIDE
    half = CMP_STRIDE * hd
    kv = lambda bi, n: (n // g, 0, 0)
    return pl.pallas_call(
        _compress_body,
        grid=(b, n2),
        in_specs=[pl.BlockSpec((1, 1, s, hd), lambda bi, n: (bi, n, 0, 0)),
                  pl.BlockSpec((1, 2, half), kv),
                  pl.BlockSpec((1, 2 * half, CMP_HIDDEN), kv),
                  pl.BlockSpec((1, 1, CMP_HIDDEN), kv),
                  pl.BlockSpec((1, CMP_HIDDEN, hd), kv),
                  pl.BlockSpec((1, 1, hd), kv)],
        out_specs=pl.BlockSpec((1, 1, nchunk, hd), lambda bi, n: (bi, n, 0, 0)),
        out_shape=jax.ShapeDtypeStruct((b, n2, nchunk, hd), BF16),
        compiler_params=_params("parallel", "parallel"),
        name="nsa_compress",
    )(kvf, pos.reshape(2, 2, half), w1.astype(BF16), b1.reshape(2, 1, CMP_HIDDEN),
      w2.astype(BF16), b2.reshape(2, 1, hd))


def _nsa_tile(i, first_half, q_ref, gt_ref, kc, vc, ks_ref, vs_ref, kw_ref, vw_ref, ovt):
    tq = q_ref.shape[1]
    r = NSA_GROUP
    q0 = i * tq
    qf = q_ref[0]
    q4 = jnp.concatenate([qf[:, h * HEAD_DIM:(h + 1) * HEAD_DIM] for h in range(r)], axis=0)
    qpos_c = q0 + lax.broadcasted_iota(jnp.int32, (tq, 1), 0)
    qpos4 = jnp.concatenate([qpos_c] * r, axis=0)

    if first_half:
        half_nc = kc.shape[0] // 2
        kc, vc, ovt = kc[:half_nc], vc[:half_nc], ovt[:, :half_nc]
    nc = kc.shape[0]
    s_c = _dot_nt(q4, kc)
    cmp_end = lax.broadcasted_iota(jnp.int32, (1, nc), 1) * CMP_STRIDE + (CMP_BLOCK - 1)
    s_c = jnp.where(cmp_end <= qpos4, s_c, NEG_INF)
    e_c = jnp.exp2(s_c - jnp.max(s_c, axis=1, keepdims=True))
    p_c = e_c / jnp.sum(e_c, axis=1, keepdims=True)
    p_c = jnp.where(qpos4 >= CMP_BLOCK - 1, p_c, 0.0)
    o_c = _dot(p_c.astype(BF16), vc)
    yield

    p_sum = p_c[0:tq]
    for h in range(1, r):
        p_sum = p_sum + p_c[h * tq:(h + 1) * tq]
    ps_hi, ps_lo = _split_bf16(p_sum)
    imp = _dot_nt(ovt, ps_hi) + _dot_nt(ovt, ps_lo)
    ns = imp.shape[0]
    blk = lax.broadcasted_iota(jnp.int32, (ns, tq), 0)
    qpos_r = q0 + lax.broadcasted_iota(jnp.int32, (ns, tq), 1)
    own = qpos_r // SLC_BLOCK
    started = blk * SLC_BLOCK <= qpos_r
    forced = (blk == 0) | (blk == own) | (blk == own - 1)
    imp = jnp.where(started, jnp.where(forced, FORCE_SCORE, imp), NEG_INF)
    bias_t = jnp.where(_rank_below(imp, SLC_TOPN, ns // 2 if first_half else ns), 0.0, NEG_INF)
    parts = [jnp.zeros((HEAD_DIM, tq), F32), bias_t]
    if ns < LANES - HEAD_DIM:
        parts.append(jnp.zeros((LANES - HEAD_DIM - ns, tq), F32))
    bias = jnp.transpose(jnp.concatenate(parts, axis=0)).astype(BF16)
    lane4 = lax.broadcasted_iota(jnp.int32, (r * tq, LANES), 1)
    qz = jnp.concatenate([q4, jnp.zeros_like(q4)], axis=1)
    qs = jnp.where(lane4 < HEAD_DIM, qz, jnp.concatenate([bias] * r, axis=0))
    yield

    tk = NSA_TK
    d0 = pl.multiple_of((q0 // tk) * tk, tk)
    causal = d0 + lax.broadcasted_iota(jnp.int32, (1, tk), 1) <= qpos4
    st = _online_step(qs, ks_ref[0, 0, pl.ds(d0, tk), :], vs_ref[0, 0, pl.ds(d0, tk), :], causal, None, None)
    yield

    span = WINDOW + tq
    w0 = pl.multiple_of(jnp.maximum(q0 - WINDOW, 0), tq)
    kpos_w = w0 + lax.broadcasted_iota(jnp.int32, (1, tq), 1)
    s_w = _dot_nt(qz, kw_ref[0, 0, pl.ds(w0, span), :])
    cols = [jnp.where(kpos_w > qpos4 - WINDOW, s_w[:, :tq], NEG_INF)]
    if first_half:
        cols = [jnp.where(kpos_w <= qpos4, cols[0], NEG_INF)]
        cols += [jnp.where(kpos_w + c * tq <= qpos4, s_w[:, c * tq:(c + 1) * tq], NEG_INF)
                 for c in range(1, span // tq)]
    else:
        cols += [s_w[:, tq:span - tq],
                 jnp.where(kpos_w + (span - tq) <= qpos4, s_w[:, span - tq:], NEG_INF)]
    s_w = jnp.concatenate(cols, axis=1)
    p_w = jnp.exp2(s_w - jnp.max(s_w, axis=1, keepdims=True)).astype(BF16)
    acc_w = _dot(p_w, vw_ref[0, 0, pl.ds(w0, span), :])
    o_w = acc_w[:, :HEAD_DIM] / acc_w[:, HEAD_DIM:]
    gt = gt_ref[0, 0]

    def finish(acc_s):
        o_s = acc_s[:, :HEAD_DIM] / acc_s[:, HEAD_DIM:]
        outs = []
        for h in range(r):
            sl = slice(h * tq, (h + 1) * tq)
            outs.append(gt[:, 3 * h:3 * h + 1] * o_c[sl] + gt[:, 3 * h + 1:3 * h + 2] * o_s[sl]
                        + gt[:, 3 * h + 2:3 * h + 3] * o_w[sl])
        return jnp.concatenate(outs, axis=1)

    return [qs], [st], finish


def _nsa_body(qa_ref, qb_ref, kc_ref, vc_ref, ks_ref, vs_ref, kw_ref, vw_ref, ga_ref, gb_ref, ovt_ref,
              oa_ref, ob_ref):
    a = pl.program_id(2)
    tq = qa_ref.shape[1]
    tk = NSA_TK
    nt = ks_ref.shape[2] // tq
    kc, vc, ovt = kc_ref[0, 0], vc_ref[0, 0], ovt_ref[...]
    per = tk // tq
    n_chunks = ks_ref.shape[2] // tk
    long_tile = _Staged(_nsa_tile(nt - 1 - a, False, qb_ref, gb_ref, kc, vc, ks_ref, vs_ref, kw_ref, vw_ref, ovt))
    short_tile = _Staged(_nsa_tile(a, True, qa_ref, ga_ref, kc, vc, ks_ref, vs_ref, kw_ref, vw_ref, ovt))
    acc_b, acc_a = _attn_pair(n_chunks - 1, (nt // 2 - 1) // per, a // per, tk, long_tile.result(), short_tile,
                              lambda e, k0: ks_ref[0, 0, pl.ds(k0, tk), :],
                              lambda e, k0: vs_ref[0, 0, pl.ds(k0, tk), :], late_pv=False)
    oa_ref[0] = short_tile.result()[2](acc_a[0]).astype(oa_ref.dtype)
    ob_ref[0] = long_tile.result()[2](acc_b[0]).astype(ob_ref.dtype)


def _nsa_attention(q, cmp, kvx, gates):
    b, s, _ = q.shape
    g = NSA_KV_HEADS
    tq = NSA_TQ
    nt = s // tq
    nc = cmp.shape[2]
    ns = s // SLC_BLOCK
    assert ns <= LANES - HEAD_DIM
    assert nt % (2 * NSA_TK // tq) == 0 and s // 2 >= WINDOW
    cw = NSA_GROUP * HEAD_DIM
    cs = jnp.arange(nc)[None, :] * CMP_STRIDE
    ss = jnp.arange(ns)[:, None] * SLC_BLOCK
    ovt = ((cs <= ss + SLC_BLOCK - 1) & (cs + CMP_BLOCK - 1 >= ss)).astype(BF16)
    head = lambda off: (lambda bi, gi, a: (bi, off + gi, 0, 0))
    half = jax.ShapeDtypeStruct((b, s // 2, NSA_Q_W), BF16)
    return pl.pallas_call(
        _nsa_body,
        grid=(b, g, nt // 2),
        in_specs=[pl.BlockSpec((1, tq, cw), lambda bi, gi, a: (bi, a, gi)),
                  pl.BlockSpec((1, tq, cw), lambda bi, gi, a: (bi, nt - 1 - a, gi)),
                  pl.BlockSpec((1, 1, nc, HEAD_DIM), head(0)),
                  pl.BlockSpec((1, 1, nc, HEAD_DIM), head(g)),
                  pl.BlockSpec((1, 1, s, LANES), head(0)),
                  pl.BlockSpec((1, 1, s, LANES), head(g)),
                  pl.BlockSpec((1, 1, s, LANES), head(2 * g)),
                  pl.BlockSpec((1, 1, s, LANES), head(3 * g)),
                  pl.BlockSpec((1, 1, tq, LANES), lambda bi, gi, a: (bi, gi, a, 0)),
                  pl.BlockSpec((1, 1, tq, LANES), lambda bi, gi, a: (bi, gi, nt - 1 - a, 0)),
                  pl.BlockSpec((ns, nc), lambda bi, gi, a: (0, 0))],
        out_specs=[pl.BlockSpec((1, tq, cw), lambda bi, gi, a: (bi, a, gi)),
                   pl.BlockSpec((1, tq, cw), lambda bi, gi, a: (bi, nt // 2 - 1 - a, gi))],
        out_shape=[half, half],
        compiler_params=_params("parallel", "parallel", "arbitrary"),
        name="nsa_attn",
    )(q, q, cmp, cmp, kvx, kvx, kvx, kvx, gates, gates, ovt)


def kernel(x, c, norm1_g, norm2_g, final_g, ada_w, ada_b, ev_w_in, ev_w_out, ev_lambda, ev_subln_g,
           od_w_in, od_w_out, od_cmp_pos, od_cmp_w1, od_cmp_b1, od_cmp_w2, od_cmp_b2,
           moe_wg, moe_bg, moe_we, moe_be, moe_w_gate, moe_w_up, moe_w_down):
    b, s, d = x.shape
    tables = _rope_tables(s)
    mod = _ada_mod(c, ada_w, ada_b)
    for l in range(DEPTH):
        sh1, sc1, g1, sh2, sc2, g2 = (mod[l, :, None, k * d:(k + 1) * d] for k in range(6))
        i = l // 2
        if l % 2 == 0:
            lambda_init = 0.8 - 0.6 * math.exp(-0.3 * l)
            qa, kx, vx, pb, kmean = _even_proj(x, norm1_g[l].reshape(1, d), sc1, sh1,
                                               ev_w_in[i].astype(BF16), tables)
            oa = _moba_attention(qa, kx, vx, kmean.reshape(b, s // MOBA_BLOCK, MOBA_W))
            ob = _diff_attention(pb, ev_lambda[i], ev_subln_g[i], lambda_init)
            mixes = (oa, ob)
            w_out = ev_w_out[i]
        else:
            w = jnp.pad(od_w_in[i], ((0, 0), (0, ODD_PAD - ODD_IN))).astype(BF16)
            q, kvf, kvx, gates = _odd_proj(x, norm1_g[l].reshape(1, d), sc1, sh1, w, tables)
            cmp = _compress(kvf, od_cmp_pos[i], od_cmp_w1[i], od_cmp_b1[i], od_cmp_w2[i], od_cmp_b2[i])
            mixes = (_nsa_attention(q, cmp, kvx, gates),)
            w_out = od_w_out[i]
        x = _mix_moe(x, g1, mixes, w_out, norm2_g[l].reshape(1, d), sc2, sh2, g2, moe_wg[l], moe_bg[l],
                     moe_we[l], moe_be[l], moe_w_gate[l], moe_w_up[l], moe_w_down[l],
                     final_g if l == DEPTH - 1 else None)
    return x
```

```python
import functools
import math

import jax
import jax.numpy as jnp
from jax import lax
from jax.experimental import pallas as pl
from jax.experimental.pallas import tpu as pltpu

F32 = jnp.float32
BF16 = jnp.bfloat16

DEPTH = 4
HEAD_DIM = 64
ROPE_DIM = HEAD_DIM // 4
ROPE_HALF = ROPE_DIM // 2
ROPE_THETA = 500000.0
NORM_EPS = 1e-6
NEG_INF = -1e30
FORCE_SCORE = 1e6
QK_SCALE = HEAD_DIM ** -0.5 * math.log2(math.e)

MOBA_HEADS = 8
MOBA_BLOCK = 256
MOBA_TOPK = 3
DIFF_HEADS = 4
MOBA_W = MOBA_HEADS * HEAD_DIM
DIFF_QK_W = DIFF_HEADS * 2 * HEAD_DIM
DIFF_V_W = DIFF_HEADS * 2 * HEAD_DIM
DIFF_W = 2 * DIFF_QK_W + DIFF_V_W

NSA_HEADS = 16
NSA_GROUP = 4
NSA_KV_HEADS = 4
CMP_BLOCK = 32
CMP_STRIDE = 16
CMP_HIDDEN = 256
SLC_BLOCK = 64
SLC_TOPN = 16
WINDOW = 512
NSA_Q_W = NSA_HEADS * HEAD_DIM
NSA_KV_W = NSA_KV_HEADS * HEAD_DIM
ODD_IN = NSA_Q_W + 6 * NSA_KV_W + 3 * NSA_HEADS

MOE_GROUPS = 4
MOE_PER_GROUP = 4
MOE_EXPERTS = 16
MOE_FF = 256

LANES = 128
VMEM_LIMIT = 56 * 1024 * 1024

PROJ_TM = 1024
MOE_TM = 512
ATT_TQ = 512
ATT_TK = 512
NSA_TQ = 256
NSA_TK = 512


def _params(*sem):
    return pltpu.CompilerParams(dimension_semantics=sem, vmem_limit_bytes=VMEM_LIMIT)


def _dot(a, b):
    return jnp.dot(a, b, preferred_element_type=F32)


def _dot_nt(a, b):
    return lax.dot_general(a, b, (((1,), (1,)), ((), ())), preferred_element_type=F32)


def _split_bf16(x):
    hi = x.astype(BF16)
    lo = (x - hi.astype(F32)).astype(BF16)
    return hi, lo


def _norm_mod(x, g, sc, sh):
    y = x * lax.rsqrt(jnp.mean(x * x, axis=-1, keepdims=True) + NORM_EPS)
    return (y * g) * (1.0 + sc) + sh


def _rope(t, c, s1, s2):
    w = t.shape[1]
    k = w // LANES
    cw = jnp.concatenate([c] * k, axis=1) if k > 1 else c
    s1w = jnp.concatenate([s1] * k, axis=1) if k > 1 else s1
    s2w = jnp.concatenate([s2] * k, axis=1) if k > 1 else s2
    return t * cw + pltpu.roll(t, ROPE_HALF, 1) * s1w + pltpu.roll(t, w - ROPE_HALF, 1) * s2w


def _rope_tables(seq):
    pos = jnp.arange(seq, dtype=F32)
    inv = ROPE_THETA ** (-jnp.arange(0, ROPE_DIM, 2, dtype=F32) / ROPE_DIM)
    ang = pos[:, None] * inv[None, :]
    cos, sin = jnp.cos(ang), jnp.sin(ang)
    ones = jnp.ones((seq, HEAD_DIM - ROPE_DIM), F32)
    zeros8 = jnp.zeros((seq, ROPE_HALF), F32)
    zeros = jnp.zeros((seq, HEAD_DIM - ROPE_DIM), F32)
    c = jnp.concatenate([cos, cos, ones], axis=1)
    s1 = jnp.concatenate([zeros8, sin, zeros], axis=1)
    s2 = jnp.concatenate([-sin, zeros8, zeros], axis=1)
    rep = LANES // HEAD_DIM
    return tuple(jnp.tile(t, (1, rep)) for t in (c, s1, s2))


def _rank_below(v, k, rows):
    n = v.shape[0]
    sub = 8
    groups = [v[g:g + sub] for g in range(0, n, sub)]
    cnts = [jnp.zeros(g.shape, F32) for g in groups]
    idx = lax.broadcasted_iota(jnp.int32, groups[0].shape, 0)
    for m in range(rows):
        rm = v[m:m + 1, :]
        for j, g in enumerate(groups):
            if j * sub > m:
                beat = rm >= g
            elif j * sub + sub - 1 < m:
                beat = rm > g
            else:
                beat = (rm > g) | ((rm == g) & (idx > m - j * sub))
            cnts[j] = cnts[j] + jnp.where(beat, 1.0, 0.0)
    return jnp.concatenate(cnts, axis=0) < k


def _online_step(q, k, v, mask, m, acc):
    s = _dot_nt(q, k)
    if mask is not None:
        s = jnp.where(mask, s, NEG_INF)
    m_new = jnp.max(s, axis=1, keepdims=True)
    if m is not None:
        m_new = jnp.maximum(m, m_new)
    p = jnp.exp2((s - m_new).astype(v.dtype))
    pv = _dot(p, v)
    if m is None:
        return m_new, pv
    return m_new, jnp.exp2(m - m_new) * acc + pv


class _Staged:
    def __init__(self, gen):
        self.gen, self.done, self.value = gen, False, None

    def advance(self):
        if not self.done:
            try:
                next(self.gen)
            except StopIteration as stop:
                self.done, self.value = True, stop.value

    def result(self):
        while not self.done:
            self.advance()
        return self.value


def _attn_pair(n_past, max_short, cnt_short, tk, long_tile, short_tile, key, value, late_pv):
    qs_l, st_l = long_tile[:2]
    ns = len(qs_l)
    n_static = n_past - max_short
    cnt_long = n_past - cnt_short
    q_cur = list(qs_l)
    m_cur = [st_l[e][0] for e in range(ns)]
    acc_cur = [st_l[e][1] for e in range(ns)]
    out_long = list(acc_cur)
    pend = None

    def flush():
        return [acc_cur[e] + _dot(pend[0][e], value(e, pend[1])) for e in range(ns)]

    for u in range(n_past):
        if pend is not None:
            acc_cur = flush()
        if u >= n_static:
            qs_s, st_s = short_tile.result()[:2]
            sw = u == cnt_long
            out_long = [jnp.where(sw, acc_cur[e], out_long[e]) for e in range(ns)]
            acc_cur = [jnp.where(sw, st_s[e][1], acc_cur[e]) for e in range(ns)]
            m_cur = [jnp.where(sw, st_s[e][0], m_cur[e]) for e in range(ns)]
            q_cur = [jnp.where(sw, qs_s[e], q_cur[e]) for e in range(ns)]
            chunk = jnp.where(u < cnt_long, u, u - cnt_long)
        else:
            chunk = u
        k0 = pl.multiple_of(chunk * tk, tk) if u >= n_static else chunk * tk
        if late_pv:
            ss = [_dot_nt(q_cur[e], key(e, k0)) for e in range(ns)]
            ps = []
            for e in range(ns):
                m_new = jnp.maximum(m_cur[e], jnp.max(ss[e], axis=1, keepdims=True))
                ps.append(jnp.exp2((ss[e] - m_new).astype(BF16)))
                acc_cur[e] = jnp.exp2(m_cur[e] - m_new) * acc_cur[e]
                m_cur[e] = m_new
            pend = (ps, k0)
        else:
            for e in range(ns):
                m_cur[e], acc_cur[e] = _online_step(q_cur[e], key(e, k0), value(e, k0), None,
                                                    m_cur[e], acc_cur[e])
        if u < n_static:
            short_tile.advance()
    if pend is not None:
        acc_cur = flush()
    st_s = short_tile.result()[1]
    none_short = cnt_short == 0
    out_long = [jnp.where(none_short, acc_cur[e], out_long[e]) for e in range(ns)]
    out_short = [jnp.where(none_short, st_s[e][1], acc_cur[e]) for e in range(ns)]
    return out_long, out_short


def _ada_body(c_ref, w_ref, b_ref, o_ref):
    c = c_ref[...]
    cs = c * (1.0 / (1.0 + jnp.exp(-c)))
    o_ref[0] = jnp.dot(cs, w_ref[0], preferred_element_type=F32,
                       precision=lax.Precision.HIGHEST) + b_ref[0]


def _ada_mod(c, ada_w, ada_b):
    b, d = c.shape
    depth, _, n = ada_w.shape
    rows = 8
    tn = 1536
    cp = jnp.pad(c, ((0, rows - b), (0, 0)))
    out = pl.pallas_call(
        _ada_body,
        grid=(depth, n // tn),
        in_specs=[pl.BlockSpec((rows, d), lambda l, j: (0, 0)),
                  pl.BlockSpec((1, d, tn), lambda l, j: (l, 0, j)),
                  pl.BlockSpec((1, 1, tn), lambda l, j: (l, 0, j))],
        out_specs=pl.BlockSpec((1, rows, tn), lambda l, j: (l, 0, j)),
        out_shape=jax.ShapeDtypeStruct((depth, rows, n), F32),
        compiler_params=_params("parallel", "parallel"),
        name="ada_mod",
    )(cp, ada_w, ada_b.reshape(depth, 1, n))
    return out[:, :b]


def _even_proj_body(x_ref, g_ref, sc_ref, sh_ref, w_ref, c_ref, s1_ref, s2_ref,
                    q_ref, k_ref, v_ref, pb_ref, km_ref):
    tm = x_ref.shape[1]
    h = _norm_mod(x_ref[0], g_ref[...], sc_ref[0], sh_ref[0]).astype(BF16)
    c, s1, s2 = c_ref[...], s1_ref[...], s2_ref[...]
    ch = MOBA_W
    lane = lax.broadcasted_iota(jnp.int32, (tm, LANES), 1)
    low = lane < HEAD_DIM
    blk = (pl.program_id(1) * tm + lax.broadcasted_iota(jnp.int32, (tm, LANES), 0)) // MOBA_BLOCK
    ind_hi = jnp.where(lane == blk + HEAD_DIM, 1.0, 0.0)
    ind_lo = jnp.where(lane == blk, 1.0, 0.0)
    for idx, kind in enumerate(("q", "k", "v", "q", "k", "v")):
        acc = _dot(h, w_ref[:, idx * ch:(idx + 1) * ch])
        if kind != "v":
            acc = _rope(acc, c, s1, s2)
        if kind == "q":
            acc = acc * QK_SCALE
        if idx == 0:
            q_ref[0] = acc.astype(BF16)
        elif idx == 1:
            nblk = tm // MOBA_BLOCK
            km_ref[0, 0] = jnp.concatenate(
                [jnp.mean(acc[n * MOBA_BLOCK:(n + 1) * MOBA_BLOCK], axis=0, keepdims=True)
                 for n in range(nblk)], axis=0)
            for hp in range(ch // LANES):
                kp = acc[:, hp * LANES:(hp + 1) * LANES]
                k_ref[0, 2 * hp] = jnp.where(low, kp, ind_hi).astype(BF16)
                k_ref[0, 2 * hp + 1] = jnp.where(low, ind_lo, kp).astype(BF16)
        elif idx == 2:
            for hp in range(ch // LANES):
                vp = acc[:, hp * LANES:(hp + 1) * LANES]
                v_ref[0, 2 * hp] = jnp.where(low, vp, 1.0).astype(BF16)
                v_ref[0, 2 * hp + 1] = jnp.where(low, 1.0, vp).astype(BF16)
        else:
            pb_ref[0, :, (idx - 3) * ch:(idx - 2) * ch] = acc.astype(BF16)


def _even_proj(x, g, sc, sh, w, tables):
    b, s, d = x.shape
    n = w.shape[1]
    tm = PROJ_TM
    nblk = tm // MOBA_BLOCK
    row = lambda bi, i: (bi, i, 0)
    vec = lambda bi, i: (bi, 0, 0)
    hd = lambda bi, i: (bi, 0, i, 0)
    tab = pl.BlockSpec((tm, LANES), lambda bi, i: (i, 0))
    return pl.pallas_call(
        _even_proj_body,
        grid=(b, s // tm),
        in_specs=[pl.BlockSpec((1, tm, d), row),
                  pl.BlockSpec((1, d), lambda bi, i: (0, 0)),
                  pl.BlockSpec((1, 1, d), vec),
                  pl.BlockSpec((1, 1, d), vec),
                  pl.BlockSpec((d, n), lambda bi, i: (0, 0)),
                  tab, tab, tab],
        out_specs=[pl.BlockSpec((1, tm, MOBA_W), row),
                   pl.BlockSpec((1, MOBA_HEADS, tm, LANES), hd),
                   pl.BlockSpec((1, MOBA_HEADS, tm, LANES), hd),
                   pl.BlockSpec((1, tm, DIFF_W), row),
                   pl.BlockSpec((1, 1, nblk, MOBA_W), lambda bi, i: (bi, i, 0, 0))],
        out_shape=[jax.ShapeDtypeStruct((b, s, MOBA_W), BF16),
                   jax.ShapeDtypeStruct((b, MOBA_HEADS, s, LANES), BF16),
                   jax.ShapeDtypeStruct((b, MOBA_HEADS, s, LANES), BF16),
                   jax.ShapeDtypeStruct((b, s, DIFF_W), BF16),
                   jax.ShapeDtypeStruct((b, s // tm, nblk, MOBA_W), F32)],
        compiler_params=_params("parallel", "parallel"),
        name="even_proj",
    )(x, g, sc, sh, w, *tables)


def _moba_body(qa_ref, qb_ref, k_ref, v_ref, km_ref, oa_ref, ob_ref):
    a = pl.program_id(2)
    tq = qa_ref.shape[1]
    nb = km_ref.shape[1]
    tk = ATT_TK
    nt = k_ref.shape[2] // tq
    per = tk // tq
    km = km_ref[0]
    lane = lax.broadcasted_iota(jnp.int32, (tq, LANES), 1)
    lane_k = lax.broadcasted_iota(jnp.int32, (nb, LANES), 1)
    n_idx = lax.broadcasted_iota(jnp.int32, (nb, tq), 0)
    low = lane < HEAD_DIM

    def tile(i, q_ref, rows):
        q = q_ref[0]
        c0 = pl.multiple_of((i // per) * tk, tk)
        qpos = i * tq + lax.broadcasted_iota(jnp.int32, (tq, 1), 0)
        causal = c0 + lax.broadcasted_iota(jnp.int32, (1, tk), 1) <= qpos
        own_blk = (i * tq + lax.broadcasted_iota(jnp.int32, (nb, tq), 1)) // MOBA_BLOCK
        q_t, st_t = [], []
        for e in range(2):
            own = low if e == 0 else (lane >= HEAD_DIM)
            own_k = (lane_k < HEAD_DIM) if e == 0 else (lane_k >= HEAD_DIM)
            qm = jnp.where(own, q, jnp.zeros_like(q))
            km_hi, km_lo = _split_bf16(jnp.where(own_k, km, 0.0))
            gs = _dot_nt(km_hi, qm) + _dot_nt(km_lo, qm)
            gs = jnp.where(n_idx < own_blk, gs, NEG_INF)
            keep = (_rank_below(gs, MOBA_TOPK, rows) & (n_idx < own_blk)) | (n_idx == own_blk)
            bias_t = jnp.where(keep, 0.0, NEG_INF)
            off = HEAD_DIM * (1 - e)
            parts = [jnp.zeros((off, tq), F32)] if off else []
            pad = jnp.concatenate(parts + [bias_t, jnp.zeros((LANES - off - nb, tq), F32)], axis=0)
            q_t.append(jnp.where(own, q, jnp.transpose(pad).astype(BF16)))
            yield
        for e in range(2):
            st_t.append(_online_step(q_t[e], k_ref[0, e, pl.ds(c0, tk), :], v_ref[0, e, pl.ds(c0, tk), :],
                                     causal, None, None))
            yield
        return q_t, st_t

    long_tile = _Staged(tile(nt - 1 - a, qb_ref, nb)).result()
    short_tile = _Staged(tile(a, qa_ref, nb // 2))
    n_chunks = k_ref.shape[2] // tk
    acc_b, acc_a = _attn_pair(n_chunks - 1, (nt // 2 - 1) // per, a // per, tk, long_tile, short_tile,
                              lambda e, k0: k_ref[0, e, pl.ds(k0, tk), :],
                              lambda e, k0: v_ref[0, e, pl.ds(k0, tk), :], late_pv=True)
    for (acc0, acc1), o_ref in ((acc_a, oa_ref), (acc_b, ob_ref)):
        num = jnp.where(low, acc0, acc1)
        den = jnp.where(low, pltpu.roll(acc0, HEAD_DIM, 1), pltpu.roll(acc1, HEAD_DIM, 1))
        o_ref[0] = (num / den).astype(o_ref.dtype)


def _moba_attention(q, kx, vx, kmean):
    b, s, _ = q.shape
    nb = s // MOBA_BLOCK
    pairs = MOBA_W // LANES
    tq = ATT_TQ
    nt = s // tq
    assert tq % MOBA_BLOCK == 0 and ATT_TK % tq == 0 and nt % (2 * ATT_TK // tq) == 0
    half = jax.ShapeDtypeStruct((b, s // 2, MOBA_W), BF16)
    return pl.pallas_call(
        _moba_body,
        grid=(b, pairs, nt // 2),
        in_specs=[pl.BlockSpec((1, tq, LANES), lambda bi, hp, a: (bi, a, hp)),
                  pl.BlockSpec((1, tq, LANES), lambda bi, hp, a: (bi, nt - 1 - a, hp)),
                  pl.BlockSpec((1, 2, s, LANES), lambda bi, hp, a: (bi, hp, 0, 0)),
                  pl.BlockSpec((1, 2, s, LANES), lambda bi, hp, a: (bi, hp, 0, 0)),
                  pl.BlockSpec((1, nb, LANES), lambda bi, hp, a: (bi, 0, hp))],
        out_specs=[pl.BlockSpec((1, tq, LANES), lambda bi, hp, a: (bi, a, hp)),
                   pl.BlockSpec((1, tq, LANES), lambda bi, hp, a: (bi, nt // 2 - 1 - a, hp))],
        out_shape=[half, half],
        compiler_params=_params("parallel", "parallel", "arbitrary"),
        name="moba_attn",
    )(q, q, kx, vx, kmean)


def _diff_body(lam_ref, g_ref, qa_ref, qb_ref, k_ref, v_ref, oa_ref, ob_ref, *, lambda_init):
    a = pl.program_id(2)
    tq = qa_ref.shape[1]
    tk = ATT_TK
    nt = k_ref.shape[1] // tq
    per = tk // tq
    lp = lam_ref[...]
    lam = (jnp.exp(jnp.sum(lp[0:1] * lp[1:2], axis=1, keepdims=True))
           - jnp.exp(jnp.sum(lp[2:3] * lp[3:4], axis=1, keepdims=True)) + lambda_init)
    lane = lax.broadcasted_iota(jnp.int32, (tq, LANES), 1)
    ones = jnp.ones((tk, LANES), BF16)

    def key(m, k0):
        return k_ref[0, pl.ds(k0, tk), :]

    def value(m, k0):
        return jnp.concatenate([v_ref[0, pl.ds(k0, tk), :], ones], axis=1)

    def tile(i, q_ref):
        q = q_ref[0]
        q_t = [jnp.where(lane < HEAD_DIM, q, jnp.zeros_like(q)), jnp.where(lane >= HEAD_DIM, q, jnp.zeros_like(q))]
        c0 = pl.multiple_of((i // per) * tk, tk)
        qpos = i * tq + lax.broadcasted_iota(jnp.int32, (tq, 1), 0)
        causal = c0 + lax.broadcasted_iota(jnp.int32, (1, tk), 1) <= qpos
        st_t = []
        for m in range(2):
            st_t.append(_online_step(q_t[m], key(m, c0), value(m, c0), causal, None, None))
            yield
        return q_t, st_t

    long_tile = _Staged(tile(nt - 1 - a, qb_ref)).result()
    short_tile = _Staged(tile(a, qa_ref))
    n_chunks = k_ref.shape[1] // tk
    acc_b, acc_a = _attn_pair(n_chunks - 1, (nt // 2 - 1) // per, a // per, tk, long_tile, short_tile,
                              key, value, late_pv=False)
    for (a0, a1), o_ref in ((acc_a, oa_ref), (acc_b, ob_ref)):
        o = a0[:, :LANES] / a0[:, LANES:] - lam * (a1[:, :LANES] / a1[:, LANES:])
        y = o * lax.rsqrt(jnp.mean(o * o, axis=-1, keepdims=True) + NORM_EPS)
        o_ref[0] = ((y * g_ref[...]) * (1.0 - lambda_init)).astype(o_ref.dtype)


def _diff_attention(pb, lam_p, subln_g, lambda_init):
    b, s, _ = pb.shape
    tq = ATT_TQ
    nt = s // tq
    assert ATT_TK % tq == 0 and nt % (2 * ATT_TK // tq) == 0
    koff = DIFF_QK_W // LANES
    voff = 2 * koff
    half = jax.ShapeDtypeStruct((b, s // 2, DIFF_V_W), BF16)
    return pl.pallas_call(
        functools.partial(_diff_body, lambda_init=lambda_init),
        grid=(b, DIFF_HEADS, nt // 2),
        in_specs=[pl.BlockSpec((4, HEAD_DIM), lambda bi, h, a: (0, 0)),
                  pl.BlockSpec((1, LANES), lambda bi, h, a: (0, 0)),
                  pl.BlockSpec((1, tq, LANES), lambda bi, h, a: (bi, a, h)),
                  pl.BlockSpec((1, tq, LANES), lambda bi, h, a: (bi, nt - 1 - a, h)),
                  pl.BlockSpec((1, s, LANES), lambda bi, h, a: (bi, 0, koff + h)),
                  pl.BlockSpec((1, s, LANES), lambda bi, h, a: (bi, 0, voff + h))],
        out_specs=[pl.BlockSpec((1, tq, LANES), lambda bi, h, a: (bi, a, h)),
                   pl.BlockSpec((1, tq, LANES), lambda bi, h, a: (bi, nt // 2 - 1 - a, h))],
        out_shape=[half, half],
        compiler_params=_params("parallel", "parallel", "arbitrary"),
        name="diff_attn",
    )(lam_p, subln_g.reshape(1, LANES), pb, pb, pb, pb)


def _first_argmax_onehot(v, iota):
    mx = jnp.max(v, axis=1, keepdims=True)
    idx = jnp.min(jnp.where(v == mx, iota, float(v.shape[1])), axis=1, keepdims=True)
    return iota == idx, mx


def _moe_body(*refs, widths, final):
    nm = 2 * len(widths)
    x_ref, g1_ref = refs[0], refs[1]
    mix_refs = refs[2:2 + nm]
    (wo_ref, g_ref, sc_ref, sh_ref, gate_ref, wr_ref, br_ref,
     wg_ref, wu_ref, wd_ref) = refs[2 + nm:12 + nm]
    fg_ref = refs[12 + nm] if final else None
    o_ref, a_ref = refs[-2], refs[-1]
    first = pl.program_id(1) < pl.num_programs(1) // 2
    y = None
    r0 = 0
    for k, wd in enumerate(widths):
        mix = jnp.where(first, mix_refs[2 * k][0], mix_refs[2 * k + 1][0])
        t = _dot(mix, wo_ref[r0:r0 + wd, :])
        y = t if y is None else y + t
        r0 += wd
    x = x_ref[0] + g1_ref[0] * y
    h = _norm_mod(x, g_ref[...], sc_ref[0], sh_ref[0])
    h_hi, h_lo = _split_bf16(h)
    tm = x.shape[0]
    rr = _dot(jnp.concatenate([h_hi, h_lo], axis=0), wr_ref[...])
    r = (rr[:tm, :LANES] + (rr[tm:, :LANES] + rr[:tm, LANES:])) + br_ref[...]
    gl = r[:, 0:MOE_GROUPS]
    iota = lax.broadcasted_iota(jnp.int32, (tm, MOE_GROUPS), 1).astype(F32)
    g_oh, g_mx = _first_argmax_onehot(gl, iota)
    gw = 1.0 / jnp.sum(jnp.exp(gl - g_mx), axis=1, keepdims=True)
    el_g = jnp.zeros((tm, MOE_PER_GROUP), F32)
    for g in range(MOE_GROUPS):
        lo = MOE_GROUPS + g * MOE_PER_GROUP
        el_g = el_g + jnp.where(g_oh[:, g:g + 1], r[:, lo:lo + MOE_PER_GROUP], 0.0)
    oh1, v1 = _first_argmax_onehot(el_g, iota)
    oh2, v2 = _first_argmax_onehot(jnp.where(oh1, -jnp.inf, el_g), iota)
    e2 = jnp.exp(v2 - v1)
    den = 1.0 + e2
    w_grp = jnp.where(oh1, 1.0 / den, 0.0) + jnp.where(oh2, e2 / den, 0.0)
    gsc = jnp.where(g_oh, gw, 0.0)

    hb = h_hi
    for e in range(MOE_EXPERTS):
        g, k = divmod(e, MOE_PER_GROUP)
        comb = gsc[:, g:g + 1] * w_grp[:, k:k + 1]
        gt = _dot(hb, wg_ref[e])
        up = _dot(hb, wu_ref[e])
        a = ((gt * (1.0 / (1.0 + jnp.exp(-gt)))) * up) * comb
        a_ref[:, e * MOE_FF:(e + 1) * MOE_FF] = a.astype(BF16)
    out = x + gate_ref[0] * _dot(a_ref[...], wd_ref[...])
    if final:
        out = (out * lax.rsqrt(jnp.mean(out * out, axis=-1, keepdims=True) + NORM_EPS)) * fg_ref[...]
    o_ref[0] = out


def _mix_moe(x, g1, mixes, w_out, g, sc, sh, gate, wg, bg, we, be, w_gate, w_up, w_down, final_g):
    b, s, d = x.shape
    tm = MOE_TM
    nr = MOE_GROUPS + MOE_EXPERTS
    wr = jnp.pad(jnp.concatenate([wg, we], axis=1), ((0, 0), (0, LANES - nr)))
    br = jnp.pad(jnp.concatenate([bg, be], axis=0), (0, LANES - nr)).reshape(1, LANES)
    wr_cat = jnp.concatenate(_split_bf16(wr), axis=1)
    widths = tuple(m[0].shape[2] for m in mixes)
    nh = s // tm // 2
    final = final_g is not None
    row = lambda bi, i: (bi, i, 0)
    vec = lambda bi, i: (bi, 0, 0)
    const2 = lambda bi, i: (0, 0)
    const3 = lambda bi, i: (0, 0, 0)
    once = pl.Buffered(1)
    in_specs = ([pl.BlockSpec((1, tm, d), row), pl.BlockSpec((1, 1, d), vec)]
                + [spec for wd in widths for spec in
                   (pl.BlockSpec((1, tm, wd), lambda bi, i: (bi, jnp.minimum(i, nh - 1), 0)),
                    pl.BlockSpec((1, tm, wd), lambda bi, i: (bi, jnp.maximum(i - nh, 0), 0)))]
                + [pl.BlockSpec(w_out.shape, const2, pipeline_mode=once),
                   pl.BlockSpec((1, d), const2),
                   pl.BlockSpec((1, 1, d), vec),
                   pl.BlockSpec((1, 1, d), vec),
                   pl.BlockSpec((1, 1, d), vec),
                   pl.BlockSpec((d, 2 * LANES), const2),
                   pl.BlockSpec((1, LANES), const2),
                   pl.BlockSpec((MOE_EXPERTS, d, MOE_FF), const3, pipeline_mode=once),
                   pl.BlockSpec((MOE_EXPERTS, d, MOE_FF), const3, pipeline_mode=once),
                   pl.BlockSpec((MOE_EXPERTS * MOE_FF, d), const2, pipeline_mode=once)])
    args = [x, g1, *[half for m in mixes for half in m], w_out.astype(BF16), g, sc, sh, gate, wr_cat, br,
            w_gate.astype(BF16), w_up.astype(BF16), w_down.astype(BF16).reshape(MOE_EXPERTS * MOE_FF, d)]
    if final:
        in_specs.append(pl.BlockSpec((1, d), const2))
        args.append(final_g.reshape(1, d))
    return pl.pallas_call(
        functools.partial(_moe_body, widths=widths, final=final),
        grid=(b, s // tm),
        in_specs=in_specs,
        out_specs=pl.BlockSpec((1, tm, d), row),
        out_shape=jax.ShapeDtypeStruct((b, s, d), F32),
        scratch_shapes=[pltpu.VMEM((tm, MOE_EXPERTS * MOE_FF), BF16)],
        compiler_params=_params("parallel", "parallel"),
        name="mix_moe",
    )(*args)


ODD_PAD = 2688


def _odd_proj_body(x_ref, g_ref, sc_ref, sh_ref, w_ref, c_ref, s1_ref, s2_ref,
                   q_ref, kvf_ref, kvx_ref, gates_ref):
    tm = x_ref.shape[1]
    h = _norm_mod(x_ref[0], g_ref[...], sc_ref[0], sh_ref[0]).astype(BF16)
    c, s1, s2 = c_ref[...], s1_ref[...], s2_ref[...]
    half = NSA_Q_W // 2
    for idx in range(2):
        acc = _dot(h, w_ref[:, idx * half:(idx + 1) * half])
        q_ref[0, :, idx * half:(idx + 1) * half] = (_rope(acc, c, s1, s2) * QK_SCALE).astype(BF16)
    lane = lax.broadcasted_iota(jnp.int32, (tm, LANES), 1)
    low = lane < HEAD_DIM
    blk = (pl.program_id(1) * tm + lax.broadcasted_iota(jnp.int32, (tm, LANES), 0)) // SLC_BLOCK
    ind = jnp.where(lane == blk + HEAD_DIM, 1.0, 0.0)
    for idx in range(6):
        c0 = NSA_Q_W + idx * NSA_KV_W
        acc = _dot(h, w_ref[:, c0:c0 + NSA_KV_W])
        if idx % 2 == 0:
            acc = _rope(acc, c, s1, s2)
        if idx < 2:
            for g in range(NSA_KV_HEADS):
                kvf_ref[0, idx * NSA_KV_HEADS + g] = acc[:, g * HEAD_DIM:(g + 1) * HEAD_DIM]
            continue
        fill = 1.0 if idx % 2 == 1 else (ind if idx == 2 else 0.0)
        for gp in range(NSA_KV_HEADS // 2):
            pair = acc[:, gp * LANES:(gp + 1) * LANES]
            swapped = pltpu.roll(pair, HEAD_DIM, 1)
            kvx_ref[0, (idx - 2) * NSA_KV_HEADS + 2 * gp] = jnp.where(low, pair, fill).astype(BF16)
            kvx_ref[0, (idx - 2) * NSA_KV_HEADS + 2 * gp + 1] = jnp.where(low, swapped, fill).astype(BF16)
    c0 = NSA_Q_W + 6 * NSA_KV_W
    gl = _dot(h, w_ref[:, c0:c0 + LANES])
    gates = 1.0 / (1.0 + jnp.exp(-gl))
    per_group = 3 * NSA_GROUP
    for g in range(NSA_KV_HEADS):
        gates_ref[0, g] = pltpu.roll(gates, (LANES - g * per_group) % LANES, 1)


def _odd_proj(x, g, sc, sh, w, tables):
    b, s, d = x.shape
    n = w.shape[1]
    tm = PROJ_TM
    row = lambda bi, i: (bi, i, 0)
    vec = lambda bi, i: (bi, 0, 0)
    tab = pl.BlockSpec((tm, LANES), lambda bi, i: (i, 0))
    hd = lambda bi, i: (bi, 0, i, 0)
    return pl.pallas_call(
        _odd_proj_body,
        grid=(b, s // tm),
        in_specs=[pl.BlockSpec((1, tm, d), row),
                  pl.BlockSpec((1, d), lambda bi, i: (0, 0)),
                  pl.BlockSpec((1, 1, d), vec),
                  pl.BlockSpec((1, 1, d), vec),
                  pl.BlockSpec((d, n), lambda bi, i: (0, 0)),
                  tab, tab, tab],
        out_specs=[pl.BlockSpec((1, tm, NSA_Q_W), row),
                   pl.BlockSpec((1, 2 * NSA_KV_HEADS, tm, HEAD_DIM), hd),
                   pl.BlockSpec((1, 4 * NSA_KV_HEADS, tm, LANES), hd),
                   pl.BlockSpec((1, NSA_KV_HEADS, tm, LANES), hd)],
        out_shape=[jax.ShapeDtypeStruct((b, s, NSA_Q_W), BF16),
                   jax.ShapeDtypeStruct((b, 2 * NSA_KV_HEADS, s, HEAD_DIM), F32),
                   jax.ShapeDtypeStruct((b, 4 * NSA_KV_HEADS, s, LANES), BF16),
                   jax.ShapeDtypeStruct((b, NSA_KV_HEADS, s, LANES), F32)],
        compiler_params=_params("parallel", "parallel"),
        name="odd_proj",
    )(x, g, sc, sh, w, *tables)


def _compress_body(x_ref, pos_ref, w1_ref, b1_ref, w2_ref, b2_ref, o_ref):
    nrow = x_ref.shape[2] // CMP_STRIDE
    x = jnp.concatenate([x_ref[0, 0, pl.ds(l, nrow, stride=CMP_STRIDE), :] for l in range(CMP_STRIDE)], axis=1)
    half = CMP_STRIDE * HEAD_DIM
    xa = (x + pos_ref[0, 0:1]).astype(BF16)
    xb = (x + pos_ref[0, 1:2]).astype(BF16)
    a = _dot(xa, w1_ref[0, 0:half])
    bm = _dot(xb, w1_ref[0, half:2 * half])
    pre = (a + pltpu.roll(bm, nrow - 1, 0)) + b1_ref[0]
    hid = 0.5 * pre * (1.0 + jnp.tanh(math.sqrt(2.0 / math.pi) * (pre + 0.044715 * (pre * pre * pre))))
    o_ref[0, 0] = (_dot(hid.astype(BF16), w2_ref[0]) + b2_ref[0]).astype(o_ref.dtype)


def _compress(kvf, pos, w1, b1, w2, b2):
    b, n2, s, hd = kvf.shape
    g = n2 // 2
    nchunk = s // CMP_STRIDE
    half = CMP_STRIDE * hd
    kv = lambda bi, n: (n // g, 0, 0)
    return pl.pallas_call(
        _compress_body,
        grid=(b, n2),
        in_specs=[pl.BlockSpec((1, 1, s, hd), lambda bi, n: (bi, n, 0, 0)),
                  pl.BlockSpec((1, 2, half), kv),
                  pl.BlockSpec((1, 2 * half, CMP_HIDDEN), kv),
                  pl.BlockSpec((1, 1, CMP_HIDDEN), kv),
                  pl.BlockSpec((1, CMP_HIDDEN, hd), kv),
                  pl.BlockSpec((1, 1, hd), kv)],
        out_specs=pl.BlockSpec((1, 1, nchunk, hd), lambda bi, n: (bi, n, 0, 0)),
        out_shape=jax.ShapeDtypeStruct((b, n2, nchunk, hd), BF16),
        compiler_params=_params("parallel", "parallel"),
        name="nsa_compress",
    )(kvf, pos.reshape(2, 2, half), w1.astype(BF16), b1.reshape(2, 1, CMP_HIDDEN),
      w2.astype(BF16), b2.reshape(2, 1, hd))


def _nsa_tile(i, first_half, q_ref, gt_ref, kc, vc, ks_ref, vs_ref, kw_ref, vw_ref, ovt):
    tq = q_ref.shape[1]
    r = NSA_GROUP
    q0 = i * tq
    qf = q_ref[0]
    q4 = jnp.concatenate([qf[:, h * HEAD_DIM:(h + 1) * HEAD_DIM] for h in range(r)], axis=0)
    qpos_c = q0 + lax.broadcasted_iota(jnp.int32, (tq, 1), 0)
    qpos4 = jnp.concatenate([qpos_c] * r, axis=0)

    if first_half:
        half_nc = kc.shape[0] // 2
        kc, vc, ovt = kc[:half_nc], vc[:half_nc], ovt[:, :half_nc]
    nc = kc.shape[0]
    s_c = _dot_nt(q4, kc)
    cmp_end = lax.broadcasted_iota(jnp.int32, (1, nc), 1) * CMP_STRIDE + (CMP_BLOCK - 1)
    s_c = jnp.where(cmp_end <= qpos4, s_c, NEG_INF)
    e_c = jnp.exp2(s_c - jnp.max(s_c, axis=1, keepdims=True))
    p_c = e_c / jnp.sum(e_c, axis=1, keepdims=True)
    p_c = jnp.where(qpos4 >= CMP_BLOCK - 1, p_c, 0.0)
    o_c = _dot(p_c.astype(BF16), vc)
    yield

    p_sum = p_c[0:tq]
    for h in range(1, r):
        p_sum = p_sum + p_c[h * tq:(h + 1) * tq]
    ps_hi, ps_lo = _split_bf16(p_sum)
    imp = _dot_nt(ovt, ps_hi) + _dot_nt(ovt, ps_lo)
    ns = imp.shape[0]
    blk = lax.broadcasted_iota(jnp.int32, (ns, tq), 0)
    qpos_r = q0 + lax.broadcasted_iota(jnp.int32, (ns, tq), 1)
    own = qpos_r // SLC_BLOCK
    started = blk * SLC_BLOCK <= qpos_r
    forced = (blk == 0) | (blk == own) | (blk == own - 1)
    imp = jnp.where(started, jnp.where(forced, FORCE_SCORE, imp), NEG_INF)
    bias_t = jnp.where(_rank_below(imp, SLC_TOPN, ns // 2 if first_half else ns), 0.0, NEG_INF)
    parts = [jnp.zeros((HEAD_DIM, tq), F32), bias_t]
    if ns < LANES - HEAD_DIM:
        parts.append(jnp.zeros((LANES - HEAD_DIM - ns, tq), F32))
    bias = jnp.transpose(jnp.concatenate(parts, axis=0)).astype(BF16)
    lane4 = lax.broadcasted_iota(jnp.int32, (r * tq, LANES), 1)
    qz = jnp.concatenate([q4, jnp.zeros_like(q4)], axis=1)
    qs = jnp.where(lane4 < HEAD_DIM, qz, jnp.concatenate([bias] * r, axis=0))
    yield

    tk = NSA_TK
    d0 = pl.multiple_of((q0 // tk) * tk, tk)
    causal = d0 + lax.broadcasted_iota(jnp.int32, (1, tk), 1) <= qpos4
    st = _online_step(qs, ks_ref[0, 0, pl.ds(d0, tk), :], vs_ref[0, 0, pl.ds(d0, tk), :], causal, None, None)
    yield

    span = WINDOW + tq
    w0 = pl.multiple_of(jnp.maximum(q0 - WINDOW, 0), tq)
    kpos_w = w0 + lax.broadcasted_iota(jnp.int32, (1, tq), 1)
    s_w = _dot_nt(qz, kw_ref[0, 0, pl.ds(w0, span), :])
    cols = [jnp.where(kpos_w > qpos4 - WINDOW, s_w[:, :tq], NEG_INF)]
    if first_half:
        cols = [jnp.where(kpos_w <= qpos4, cols[0], NEG_INF)]
        cols += [jnp.where(kpos_w + c * tq <= qpos4, s_w[:, c * tq:(c + 1) * tq], NEG_INF)
                 for c in range(1, span // tq)]
    else:
        cols += [s_w[:, tq:span - tq],
                 jnp.where(kpos_w + (span - tq) <= qpos4, s_w[:, span - tq:], NEG_INF)]
    s_w = jnp.concatenate(cols, axis=1)
    p_w = jnp.exp2((s_w - jnp.max(s_w, axis=1, keepdims=True)).astype(BF16))
    acc_w = _dot(p_w, vw_ref[0, 0, pl.ds(w0, span), :])
    o_w = acc_w[:, :HEAD_DIM] / acc_w[:, HEAD_DIM:]
    gt = gt_ref[0, 0]

    def finish(acc_s):
        o_s = acc_s[:, :HEAD_DIM] / acc_s[:, HEAD_DIM:]
        outs = []
        for h in range(r):
            sl = slice(h * tq, (h + 1) * tq)
            outs.append(gt[:, 3 * h:3 * h + 1] * o_c[sl] + gt[:, 3 * h + 1:3 * h + 2] * o_s[sl]
                        + gt[:, 3 * h + 2:3 * h + 3] * o_w[sl])
        return jnp.concatenate(outs, axis=1)

    return [qs], [st], finish


def _nsa_body(qa_ref, qb_ref, kc_ref, vc_ref, ks_ref, vs_ref, kw_ref, vw_ref, ga_ref, gb_ref, ovt_ref,
              oa_ref, ob_ref):
    a = pl.program_id(2)
    tq = qa_ref.shape[1]
    tk = NSA_TK
    nt = ks_ref.shape[2] // tq
    kc, vc, ovt = kc_ref[0, 0], vc_ref[0, 0], ovt_ref[...]
    per = tk // tq
    n_chunks = ks_ref.shape[2] // tk
    long_tile = _Staged(_nsa_tile(nt - 1 - a, False, qb_ref, gb_ref, kc, vc, ks_ref, vs_ref, kw_ref, vw_ref, ovt))
    short_tile = _Staged(_nsa_tile(a, True, qa_ref, ga_ref, kc, vc, ks_ref, vs_ref, kw_ref, vw_ref, ovt))
    acc_b, acc_a = _attn_pair(n_chunks - 1, (nt // 2 - 1) // per, a // per, tk, long_tile.result(), short_tile,
                              lambda e, k0: ks_ref[0, 0, pl.ds(k0, tk), :],
                              lambda e, k0: vs_ref[0, 0, pl.ds(k0, tk), :], late_pv=False)
    oa_ref[0] = short_tile.result()[2](acc_a[0]).astype(oa_ref.dtype)
    ob_ref[0] = long_tile.result()[2](acc_b[0]).astype(ob_ref.dtype)


def _nsa_attention(q, cmp, kvx, gates):
    b, s, _ = q.shape
    g = NSA_KV_HEADS
    tq = NSA_TQ
    nt = s // tq
    nc = cmp.shape[2]
    ns = s // SLC_BLOCK
    assert ns <= LANES - HEAD_DIM
    assert nt % (2 * NSA_TK // tq) == 0 and s // 2 >= WINDOW
    cw = NSA_GROUP * HEAD_DIM
    cs = jnp.arange(nc)[None, :] * CMP_STRIDE
    ss = jnp.arange(ns)[:, None] * SLC_BLOCK
    ovt = ((cs <= ss + SLC_BLOCK - 1) & (cs + CMP_BLOCK - 1 >= ss)).astype(BF16)
    head = lambda off: (lambda bi, gi, a: (bi, off + gi, 0, 0))
    half = jax.ShapeDtypeStruct((b, s // 2, NSA_Q_W), BF16)
    return pl.pallas_call(
        _nsa_body,
        grid=(b, g, nt // 2),
        in_specs=[pl.BlockSpec((1, tq, cw), lambda bi, gi, a: (bi, a, gi)),
                  pl.BlockSpec((1, tq, cw), lambda bi, gi, a: (bi, nt - 1 - a, gi)),
                  pl.BlockSpec((1, 1, nc, HEAD_DIM), head(0)),
                  pl.BlockSpec((1, 1, nc, HEAD_DIM), head(g)),
                  pl.BlockSpec((1, 1, s, LANES), head(0)),
                  pl.BlockSpec((1, 1, s, LANES), head(g)),
                  pl.BlockSpec((1, 1, s, LANES), head(2 * g)),
                  pl.BlockSpec((1, 1, s, LANES), head(3 * g)),
                  pl.BlockSpec((1, 1, tq, LANES), lambda bi, gi, a: (bi, gi, a, 0)),
                  pl.BlockSpec((1, 1, tq, LANES), lambda bi, gi, a: (bi, gi, nt - 1 - a, 0)),
                  pl.BlockSpec((ns, nc), lambda bi, gi, a: (0, 0))],
        out_specs=[pl.BlockSpec((1, tq, cw), lambda bi, gi, a: (bi, a, gi)),
                   pl.BlockSpec((1, tq, cw), lambda bi, gi, a: (bi, nt // 2 - 1 - a, gi))],
        out_shape=[half, half],
        compiler_params=_params("parallel", "parallel", "arbitrary"),
        name="nsa_attn",
    )(q, q, cmp, cmp, kvx, kvx, kvx, kvx, gates, gates, ovt)


def kernel(x, c, norm1_g, norm2_g, final_g, ada_w, ada_b, ev_w_in, ev_w_out, ev_lambda, ev_subln_g,
           od_w_in, od_w_out, od_cmp_pos, od_cmp_w1, od_cmp_b1, od_cmp_w2, od_cmp_b2,
           moe_wg, moe_bg, moe_we, moe_be, moe_w_gate, moe_w_up, moe_w_down):
    b, s, d = x.shape
    tables = _rope_tables(s)
    mod = _ada_mod(c, ada_w, ada_b)
    for l in range(DEPTH):
        sh1, sc1, g1, sh2, sc2, g2 = (mod[l, :, None, k * d:(k + 1) * d] for k in range(6))
        i = l // 2
        if l % 2 == 0:
            lambda_init = 0.8 - 0.6 * math.exp(-0.3 * l)
            qa, kx, vx, pb, kmean = _even_proj(x, norm1_g[l].reshape(1, d), sc1, sh1,
                                               ev_w_in[i].astype(BF16), tables)
            oa = _moba_attention(qa, kx, vx, kmean.reshape(b, s // MOBA_BLOCK, MOBA_W))
            ob = _diff_attention(pb, ev_lambda[i], ev_subln_g[i], lambda_init)
            mixes = (oa, ob)
            w_out = ev_w_out[i]
        else:
            w = jnp.pad(od_w_in[i], ((0, 0), (0, ODD_PAD - ODD_IN))).astype(BF16)
            q, kvf, kvx, gates = _odd_proj(x, norm1_g[l].reshape(1, d), sc1, sh1, w, tables)
            cmp = _compress(kvf, od_cmp_pos[i], od_cmp_w1[i], od_cmp_b1[i], od_cmp_w2[i], od_cmp_b2[i])
            mixes = (_nsa_attention(q, cmp, kvx, gates),)
            w_out = od_w_out[i]
        x = _mix_moe(x, g1, mixes, w_out, norm2_g[l].reshape(1, d), sc2, sh2, g2, moe_wg[l], moe_bg[l],
                     moe_we[l], moe_be[l], moe_w_gate[l], moe_w_up[l], moe_w_down[l],
                     final_g if l == DEPTH - 1 else None)
    return x
```
